```python
import jax, jax.numpy as jnp
from jax import lax
import numpy as np

D_MODEL = 1024
BATCH = 16
SEQ = 2048
DEPTH = 2

GRID_W = 64
MLA_HEADS = 8
MLA_NOPE = 64
MLA_ROPE = 32
MLA_V = 64
MLA_QK = MLA_NOPE + MLA_ROPE
Q_LORA = 256
KV_LORA = 128
MLA_WIDTH = MLA_HEADS * MLA_V
ROPE_THETA = 10000.0
Q_BLOCK = 128
NA_HEADS = 8
NA_DH = 64
NA_WIDTH = NA_HEADS * NA_DH
NA_KR_MAX = 8
NA_KC = 16
MIX_WIDTH = MLA_WIDTH + NA_WIDTH
OFF_CQ = 0
OFF_CKV = OFF_CQ + Q_LORA
OFF_KPE = OFF_CKV + KV_LORA
OFF_NA = OFF_KPE + MLA_ROPE
IN_COLS = OFF_NA + 3 * NA_WIDTH
N_GROUPS = 4
EXPERTS_PER_GROUP = 4
N_EXPERTS = N_GROUPS * EXPERTS_PER_GROUP
TOP_K_IN_GROUP = 2
D_EXPERT = 256
EPS = 1e-6

kernel_name = 'hybrid_mla_natten_hiermoe_encoder'


def rms_norm(x, g):
    xf = x.astype(jnp.float32)
    y = xf * lax.rsqrt(jnp.mean(xf * xf, axis=-1, keepdims=True) + EPS)
    return (y * g.astype(jnp.float32)).astype(x.dtype)


def rope_2d_tables(seq):
    t = jnp.arange(seq)
    row = (t // GRID_W).astype(jnp.float32)
    col = (t % GRID_W).astype(jnp.float32)
    n_freq = MLA_ROPE // 4
    inv = ROPE_THETA ** (-jnp.arange(n_freq, dtype=jnp.float32) / n_freq)
    ang_r = row[:, None] * inv[None, :]
    ang_c = col[:, None] * inv[None, :]
    return (jnp.cos(ang_r), jnp.sin(ang_r), jnp.cos(ang_c), jnp.sin(ang_c))


def _rotate(x, cos, sin):
    x1, x2 = jnp.split(x, 2, axis=-1)
    return jnp.concatenate([x1 * cos - x2 * sin, x2 * cos + x1 * sin], axis=-1)


def apply_rope_2d(x, tabs):
    cos_r, sin_r, cos_c, sin_c = [t.astype(x.dtype) for t in tabs]
    xr, xc = jnp.split(x, 2, axis=-1)
    return jnp.concatenate([_rotate(xr, cos_r, sin_r), _rotate(xc, cos_c, sin_c)], axis=-1)


def mla_mix(c_q, c_kv, k_pe, g_q_a, w_q_b, g_kv_a, w_kv_b, g_q, g_k, tabs):
    B, S, _ = c_q.shape
    q = (rms_norm(c_q, g_q_a) @ w_q_b).reshape(B, S, MLA_HEADS, MLA_QK)
    kv = (rms_norm(c_kv, g_kv_a) @ w_kv_b).reshape(B, S, MLA_HEADS, MLA_NOPE + MLA_V)
    q_nope, q_pe = q[..., :MLA_NOPE], q[..., MLA_NOPE:]
    k_nope, v = kv[..., :MLA_NOPE], kv[..., MLA_NOPE:]
    scale = MLA_QK ** -0.5
    q_nope = rms_norm(q_nope, g_q[:MLA_NOPE]) * scale
    q_pe = apply_rope_2d(rms_norm(q_pe, g_q[MLA_NOPE:]), [t[:, None, :] for t in tabs]) * scale
    k_nope = rms_norm(k_nope, g_k[:MLA_NOPE])
    k_pe = apply_rope_2d(rms_norm(k_pe, g_k[MLA_NOPE:]), tabs)
    nb = S // Q_BLOCK
    qn_b = q_nope.reshape(B, nb, Q_BLOCK, MLA_HEADS, MLA_NOPE).transpose(1, 0, 2, 3, 4)
    qp_b = q_pe.reshape(B, nb, Q_BLOCK, MLA_HEADS, MLA_ROPE).transpose(1, 0, 2, 3, 4)

    def attend(blk):
        qn, qp = blk
        s = (jnp.einsum('bqhd,bkhd->bhqk', qn, k_nope)
             + jnp.einsum('bqhr,bkr->bhqk', qp, k_pe)).astype(jnp.float32)
        p = jax.nn.softmax(s, axis=-1).astype(v.dtype)
        return jnp.einsum('bhqk,bkhd->bqhd', p, v)

    o = lax.map(attend, (qn_b, qp_b))
    return o.transpose(1, 0, 2, 3, 4).reshape(B, S, MLA_WIDTH)


def na_mix(q, k, v, g_q, g_k, rpb):
    B, S, H, dh = q.shape
    rows = S // GRID_W
    kr = min(NA_KR_MAX, rows)
    q = rms_norm(q, g_q) * (dh ** -0.5)
    k = rms_norm(k, g_k)
    qg = q.reshape(B, rows, GRID_W, H, dh).transpose(1, 0, 2, 3, 4)
    kg = k.reshape(B, rows, GRID_W, H, dh)
    vg = v.reshape(B, rows, GRID_W, H, dh)
    cols = jnp.arange(GRID_W)
    cs = jnp.clip(cols - NA_KC // 2, 0, GRID_W - NA_KC)
    col_mask = (cols[None, :] >= cs[:, None]) & (cols[None, :] < cs[:, None] + NA_KC)
    dc = jnp.clip(cols[None, :] - cols[:, None], -(NA_KC - 1), NA_KC - 1) + NA_KC - 1
    rpb_c = rpb[:, :, dc]
    neg = jnp.finfo(jnp.float32).min

    def attend_row(args):
        r, q_row = args
        rs = jnp.clip(r - kr // 2, 0, rows - kr)
        k_band = lax.dynamic_slice_in_dim(kg, rs, kr, axis=1)
        v_band = lax.dynamic_slice_in_dim(vg, rs, kr, axis=1)
        dr = rs + jnp.arange(kr) - r + NA_KR_MAX - 1
        bias = jnp.take(rpb_c, dr, axis=1).transpose(0, 2, 1, 3)
        s = jnp.einsum('bchd,bijhd->bhcij', q_row, k_band).astype(jnp.float32)
        s = jnp.where(col_mask[:, None, :], s + bias.astype(jnp.float32)[None], neg)
        p = jax.nn.softmax(s.reshape(B, H, GRID_W, kr * GRID_W), axis=-1)
        p = p.reshape(B, H, GRID_W, kr, GRID_W).astype(v.dtype)
        return jnp.einsum('bhcij,bijhd->bchd', p, v_band)

    o = lax.map(attend_row, (jnp.arange(rows), qg))
    return o.transpose(1, 0, 2, 3, 4).reshape(B, S, H * dh)


def hier_moe(h, w_rg, b_rg, w_re, b_re, w_gate, w_up, w_down):
    B, S, _ = h.shape
    hf = h.astype(jnp.float32)
    p_g = jax.nn.softmax(hf @ w_rg.astype(jnp.float32) + b_rg.astype(jnp.float32), axis=-1)
    pg_top, g_idx = lax.top_k(p_g, 1)
    le = (hf @ w_re.astype(jnp.float32) + b_re.astype(jnp.float32)).reshape(
        B, S, N_GROUPS, EXPERTS_PER_GROUP)
    g_onehot = jax.nn.one_hot(g_idx[..., 0], N_GROUPS, dtype=jnp.float32)
    le_sel = jnp.sum(le * g_onehot[..., None], axis=2)
    p_e = jax.nn.softmax(le_sel, axis=-1)
    pe_top, e_idx = lax.top_k(p_e, TOP_K_IN_GROUP)
    wts = pg_top * pe_top / jnp.sum(pe_top, axis=-1, keepdims=True)
    eid = g_idx * EXPERTS_PER_GROUP + e_idx
    combine = jnp.sum(jax.nn.one_hot(eid, N_EXPERTS, dtype=jnp.float32) * wts[..., None], axis=-2)
    a = jnp.einsum('bsd,edf->bsef', h, w_gate)
    u = jnp.einsum('bsd,edf->bsef', h, w_up)
    hid = jax.nn.silu(a) * u * combine[..., None].astype(h.dtype)
    return jnp.einsum('bsef,efd->bsd', hid, w_down)


def setup_inputs(seed: int = 0) -> dict:
    key = jax.random.key(seed)
    ks = jax.random.split(key, 24)
    L, D = DEPTH, D_MODEL
    f32 = jnp.float32

    def nrm(k, shape, scale):
        return jax.random.normal(k, shape, f32) * scale

    def gain(k, shape):
        return 1.0 + 0.05 * jax.random.normal(k, shape, f32)

    return {
        'x': jax.random.normal(ks[0], (BATCH, SEQ, D), f32),
        'g_mix_norm': gain(ks[1], (L, D)),
        'w_in': nrm(ks[2], (L, D, IN_COLS), D ** -0.5),
        'g_q_a': gain(ks[3], (L, Q_LORA)),
        'w_q_b': nrm(ks[4], (L, Q_LORA, MLA_HEADS * MLA_QK), Q_LORA ** -0.5),
        'g_kv_a': gain(ks[5], (L, KV_LORA)),
        'w_kv_b': nrm(ks[6], (L, KV_LORA, MLA_HEADS * (MLA_NOPE + MLA_V)), KV_LORA ** -0.5),
        'g_mla_q': gain(ks[7], (L, MLA_QK)),
        'g_mla_k': gain(ks[8], (L, MLA_QK)),
        'g_na_q': gain(ks[9], (L, NA_DH)),
        'g_na_k': gain(ks[10], (L, NA_DH)),
        'na_rpb': nrm(ks[11], (L, NA_HEADS, 2 * NA_KR_MAX - 1, 2 * NA_KC - 1), 0.1),
        'g_mla_out': gain(ks[12], (L, MLA_WIDTH)),
        'g_na_out': gain(ks[13], (L, NA_WIDTH)),
        'w_out': nrm(ks[14], (L, MIX_WIDTH, D), MIX_WIDTH ** -0.5),
        'g_ffn_norm': gain(ks[15], (L, D)),
        'w_router_group': nrm(ks[16], (L, D, N_GROUPS), D ** -0.5),
        'b_router_group': nrm(ks[17], (L, N_GROUPS), 0.01),
        'w_router_expert': nrm(ks[18], (L, D, N_EXPERTS), D ** -0.5),
        'b_router_expert': nrm(ks[19], (L, N_EXPERTS), 0.01),
        'w_gate': nrm(ks[20], (L, N_EXPERTS, D, D_EXPERT), D ** -0.5),
        'w_up': nrm(ks[21], (L, N_EXPERTS, D, D_EXPERT), D ** -0.5),
        'w_down': nrm(ks[22], (L, N_EXPERTS, D_EXPERT, D), D_EXPERT ** -0.5),
    }


def reference(x, g_mix_norm, w_in, g_q_a, w_q_b, g_kv_a, w_kv_b, g_mla_q, g_mla_k,
              g_na_q, g_na_k, na_rpb, g_mla_out, g_na_out, w_out, g_ffn_norm,
              w_router_group, b_router_group, w_router_expert, b_router_expert,
              w_gate, w_up, w_down):
    B, S, _ = x.shape
    tabs = rope_2d_tables(S)
    for l in range(DEPTH):
        h = rms_norm(x, g_mix_norm[l])
        proj = h @ w_in[l]
        c_q = proj[..., OFF_CQ:OFF_CKV]
        c_kv = proj[..., OFF_CKV:OFF_KPE]
        k_pe = proj[..., OFF_KPE:OFF_NA]
        na = proj[..., OFF_NA:].reshape(B, S, 3, NA_HEADS, NA_DH)
        o_a = mla_mix(c_q, c_kv, k_pe, g_q_a[l], w_q_b[l], g_kv_a[l], w_kv_b[l],
                      g_mla_q[l], g_mla_k[l], tabs)
        o_b = na_mix(na[:, :, 0], na[:, :, 1], na[:, :, 2], g_na_q[l], g_na_k[l], na_rpb[l])
        mixed = jnp.concatenate([rms_norm(o_a, g_mla_out[l]), rms_norm(o_b, g_na_out[l])], axis=-1)
        x = x + mixed @ w_out[l]
        h2 = rms_norm(x, g_ffn_norm[l])
        x = x + hier_moe(h2, w_router_group[l], b_router_group[l], w_router_expert[l],
                         b_router_expert[l], w_gate[l], w_up[l], w_down[l])
    return x
```

```python
import functools
import math

import numpy as np
import jax
import jax.numpy as jnp
from jax import lax
from jax.experimental import pallas as pl
from jax.experimental.pallas import tpu as pltpu

D_MODEL = 1024
GRID_W = 64
MLA_HEADS = 8
MLA_NOPE = 64
MLA_ROPE = 32
MLA_V = 64
MLA_QK = MLA_NOPE + MLA_ROPE
Q_LORA = 256
KV_LORA = 128
MLA_WIDTH = MLA_HEADS * MLA_V
ROPE_THETA = 10000.0
NA_HEADS = 8
NA_DH = 64
NA_WIDTH = NA_HEADS * NA_DH
NA_KR_MAX = 8
NA_KC = 16
N_GROUPS = 4
EXPERTS_PER_GROUP = 4
D_EXPERT = 256
EPS = 1e-6

LANES = 128
MXU_TILE = 256
BF16_ROWS = 16
HEAD_PAD = LANES
MLA_PAD_WIDTH = MLA_HEADS * HEAD_PAD
PROJ_COLS = 2048
LOG2E = 1.4426950408889634
MASK_NEG = -1e30
VMEM_LIMIT = 56 * 1024 * 1024

C_CQ = 0
C_CKV = C_CQ + Q_LORA
C_KPE = C_CKV + KV_LORA
C_NAQ = C_KPE + LANES
C_NAK = C_NAQ + NA_WIDTH
C_NAV = C_NAK + NA_WIDTH

TM_PROJ = 512
TQ_MLA = 512
NA_ROWS_PER_STEP = 4
TM_MOE = 1024
MOE_CHUNK = 128
MOE_ROWS = TM_MOE + N_GROUPS * BF16_ROWS + MOE_CHUNK + 64
ROUTE_GID = 0
ROUTE_W0 = 1


def _cparams(sem):
    return pltpu.CompilerParams(dimension_semantics=sem, vmem_limit_bytes=VMEM_LIMIT)


def _full(shape):
    nd = len(shape)
    return pl.BlockSpec(shape, lambda *_: (0,) * nd)


def _bf16(x):
    return x.astype(jnp.bfloat16)


def _dot(a, b):
    return jnp.dot(a, b, preferred_element_type=jnp.float32)


def _dot_nt(a, b):
    return lax.dot_general(a, b, (((1,), (1,)), ((), ())), preferred_element_type=jnp.float32)


def _rms_rows(x, gain):
    ms = jnp.mean(x * x, axis=-1, keepdims=True)
    return x * lax.rsqrt(ms + EPS) * gain


def _segment_rms_scale(x, seg_mat):
    n = x.shape[-1]
    parts = []
    for j in range(0, n, seg_mat.shape[0]):
        w = min(seg_mat.shape[0], n - j)
        xs = x[:, j:j + w]
        parts.append(_dot(_bf16(xs * xs), seg_mat[:w, :w]))
    ms = parts[0] if len(parts) == 1 else jnp.concatenate(parts, axis=-1)
    return lax.rsqrt(ms + EPS)


def _rope_block(x, cos, sin_a, sin_b):
    return x * cos + pltpu.roll(x, LANES - 8, 1) * sin_a + pltpu.roll(x, 8, 1) * sin_b


def _proj_kernel(x_ref, gmix_ref, win_ref, gqa_ref, wqb_ref, gkva_ref, wkb_ref, wv_ref,
                 gq_ref, gk_ref, gkpe_ref, gnaq_ref, gnak_ref,
                 mq_ref, mk_ref, mkpe_ref, mna_ref, place_ref,
                 cq_ref, sqa_ref, sqb_ref, ck_ref, ska_ref, skb_ref,
                 q_out, k_out, v_out, naq_out, nak_out, nav_out):
    x = x_ref[...]
    h = _bf16(_rms_rows(x, gmix_ref[...]))
    proj = _dot(h, win_ref[...])

    cqn = _bf16(_rms_rows(proj[:, C_CQ:C_CKV], gqa_ref[...]))
    q = _dot(cqn, wqb_ref[...])
    q = q * _segment_rms_scale(q, mq_ref[...]) * gq_ref[...]
    cosq, sqa, sqb = cq_ref[...], sqa_ref[...], sqb_ref[...]
    for hd in range(MLA_HEADS):
        sl = slice(hd * HEAD_PAD, (hd + 1) * HEAD_PAD)
        q_out[:, sl] = _bf16(_rope_block(q[:, sl], cosq, sqa, sqb))

    ckvn = _bf16(_rms_rows(proj[:, C_CKV:C_KPE], gkva_ref[...]))
    kn = _dot(ckvn, wkb_ref[...])
    kn = kn * _segment_rms_scale(kn, mk_ref[...]) * gk_ref[...]
    v_out[...] = _bf16(_dot(ckvn, wv_ref[...]))
    kp = proj[:, C_KPE:C_NAQ]
    kp = kp * lax.rsqrt(_dot(_bf16(kp * kp), mkpe_ref[...]) + EPS) * gkpe_ref[...]
    kp = _rope_block(kp, ck_ref[...], ska_ref[...], skb_ref[...])
    k_out[...] = _bf16(kn + _dot(_bf16(kp), place_ref[...]))

    naq = proj[:, C_NAQ:C_NAK]
    naq_out[...] = _bf16(naq * _segment_rms_scale(naq, mna_ref[...]) * gnaq_ref[...])
    nak = proj[:, C_NAK:C_NAV]
    nak_out[...] = _bf16(nak * _segment_rms_scale(nak, mna_ref[...]) * gnak_ref[...])
    nav_out[...] = _bf16(proj[:, C_NAV:PROJ_COLS])


def _proj_call(x2d, lw, consts, seq):
    t = x2d.shape[0]
    tm = TM_PROJ
    n_seq_tiles = seq // tm
    row = lambda i: (i, 0)
    tab = lambda i: (i % n_seq_tiles, 0)
    weights = [lw["g_mix"], lw["w_in"], lw["g_q_a"], lw["w_qb"], lw["g_kv_a"], lw["w_kb"], lw["w_v"],
               lw["g_q"], lw["g_k"], lw["g_kpe"], lw["g_naq"], lw["g_nak"],
               consts["m_q"], consts["m_k"], consts["m_kpe"], consts["m_na"], consts["place"]]
    tables = [consts["cos_q"], consts["sin_qa"], consts["sin_qb"], consts["cos_k"], consts["sin_ka"], consts["sin_kb"]]
    in_specs = ([pl.BlockSpec((tm, D_MODEL), row)] + [_full(w.shape) for w in weights]
                + [pl.BlockSpec((tm, LANES), tab) for _ in tables])
    out_shapes = [jax.ShapeDtypeStruct((t, MLA_PAD_WIDTH), jnp.bfloat16),
                  jax.ShapeDtypeStruct((t, MLA_PAD_WIDTH), jnp.bfloat16),
                  jax.ShapeDtypeStruct((t, MLA_WIDTH), jnp.bfloat16),
                  jax.ShapeDtypeStruct((t, NA_WIDTH), jnp.bfloat16),
                  jax.ShapeDtypeStruct((t, NA_WIDTH), jnp.bfloat16),
                  jax.ShapeDtypeStruct((t, NA_WIDTH), jnp.bfloat16)]
    out_specs = [pl.BlockSpec((tm, s.shape[1]), row) for s in out_shapes]
    return pl.pallas_call(
        _proj_kernel, grid=(t // tm,), in_specs=in_specs, out_specs=out_specs, out_shape=out_shapes,
        compiler_params=_cparams(("parallel",)), name="proj",
    )(x2d, *weights, *tables)


def _mla_kernel(q_ref, k_ref, v_ref, o_ref):
    v2 = v_ref[...]
    outs = []
    for hh in range(2):
        sl = slice(hh * HEAD_PAD, (hh + 1) * HEAD_PAD)
        s = _dot_nt(q_ref[:, sl], k_ref[:, sl])
        m = jnp.max(s, axis=-1, keepdims=True)
        p = jnp.exp2(s - m)
        l = jnp.sum(p, axis=-1, keepdims=True)
        outs.append(_dot(_bf16(p), v2) * (1.0 / l))
    lane = lax.broadcasted_iota(jnp.int32, outs[0].shape, 1)
    o_ref[...] = jnp.where(lane < MLA_V, outs[0], outs[1])


def _mla_call(q, k, v, batch, seq):
    t = q.shape[0]
    tq = TQ_MLA
    nq = seq // tq
    pairs = MLA_HEADS // 2
    return pl.pallas_call(
        _mla_kernel, grid=(batch, pairs, nq),
        in_specs=[pl.BlockSpec((tq, 2 * HEAD_PAD), lambda b, p, i: (b * nq + i, p)),
                  pl.BlockSpec((seq, 2 * HEAD_PAD), lambda b, p, i: (b, p)),
                  pl.BlockSpec((seq, 2 * MLA_V), lambda b, p, i: (b, p))],
        out_specs=pl.BlockSpec((tq, 2 * MLA_V), lambda b, p, i: (b * nq + i, p)),
        out_shape=jax.ShapeDtypeStruct((t, MLA_WIDTH), jnp.float32),
        compiler_params=_cparams(("parallel", "parallel", "arbitrary")), name="mla_attn",
    )(q, k, v)


def _na_kernel(q_ref, k_ref, v_ref, bias_ref, o_ref, *, rows, kr):
    step = pl.program_id(1)
    band = kr * GRID_W
    for rr in range(NA_ROWS_PER_STEP):
        r = step * NA_ROWS_PER_STEP + rr
        rs = jnp.clip(r - kr // 2, 0, rows - kr)
        cls = r - rs
        start = pl.multiple_of(rs * GRID_W, GRID_W)
        qrow = q_ref[rr * GRID_W:(rr + 1) * GRID_W, :]
        for pr in range(NA_HEADS // 2):
            sl = slice(pr * LANES, (pr + 1) * LANES)
            q2 = qrow[:, sl]
            lane = lax.broadcasted_iota(jnp.int32, q2.shape, 1)
            zero = jnp.zeros_like(q2)
            qq = jnp.concatenate([jnp.where(lane < NA_DH, q2, zero), jnp.where(lane >= NA_DH, q2, zero)], axis=0)
            kp = k_ref[pl.ds(start, band), sl]
            vp = v_ref[pl.ds(start, band), sl]
            s = _dot_nt(qq, kp)
            s = s + bias_ref[cls, 2 * pr:2 * pr + 2].reshape(2 * GRID_W, band)
            m = jnp.max(s, axis=-1, keepdims=True)
            p = jnp.exp2(s - m)
            l = jnp.sum(p, axis=-1, keepdims=True)
            pv = _dot(_bf16(p), vp) * (1.0 / l)
            o_ref[rr * GRID_W:(rr + 1) * GRID_W, sl] = jnp.where(lane < NA_DH, pv[:GRID_W], pv[GRID_W:])


def _na_call(q, k, v, bias, batch, seq):
    t = q.shape[0]
    rows = seq // GRID_W
    kr = min(NA_KR_MAX, rows)
    steps = rows // NA_ROWS_PER_STEP
    blk = NA_ROWS_PER_STEP * GRID_W
    return pl.pallas_call(
        functools.partial(_na_kernel, rows=rows, kr=kr), grid=(batch, steps),
        in_specs=[pl.BlockSpec((blk, NA_WIDTH), lambda b, i: (b * steps + i, 0)),
                  pl.BlockSpec((seq, NA_WIDTH), lambda b, i: (b, 0)),
                  pl.BlockSpec((seq, NA_WIDTH), lambda b, i: (b, 0)),
                  _full(bias.shape)],
        out_specs=pl.BlockSpec((blk, NA_WIDTH), lambda b, i: (b * steps + i, 0)),
        out_shape=jax.ShapeDtypeStruct((t, NA_WIDTH), jnp.float32),
        compiler_params=_cparams(("parallel", "arbitrary")), name="na_attn",
    )(q, k, v, bias)


def _mix_kernel(x_ref, oa_ref, ob_ref, ga_ref, gb_ref, wout_ref, gffn_ref, wrh_ref, wrl_ref, br_ref,
                x1_out, h2_out, route_out):
    mixed = jnp.concatenate([_rms_rows(oa_ref[...], ga_ref[...]), _rms_rows(ob_ref[...], gb_ref[...])], axis=-1)
    x1 = x_ref[...] + _dot(_bf16(mixed), wout_ref[...])
    x1_out[...] = x1
    h2 = _rms_rows(x1, gffn_ref[...])
    hi = _bf16(h2)
    lo = _bf16(h2 - hi.astype(jnp.float32))
    h2_out[...] = hi
    logits = _dot(hi, wrh_ref[...]) + _dot(lo, wrh_ref[...]) + _dot(hi, wrl_ref[...]) + br_ref[...]

    lane = lax.broadcasted_iota(jnp.int32, logits.shape, 1)
    neg = jnp.float32(-jnp.inf)
    big = jnp.int32(LANES)
    is_grp = lane < N_GROUPS
    lg = jnp.where(is_grp, logits, neg)
    gmax = jnp.max(lg, axis=-1, keepdims=True)
    gid = jnp.min(jnp.where(lg == gmax, lane, big), axis=-1, keepdims=True)
    pg_top = 1.0 / jnp.sum(jnp.where(is_grp, jnp.exp(logits - gmax), 0.0), axis=-1, keepdims=True)
    base = N_GROUPS + EXPERTS_PER_GROUP * gid
    in_sel = (lane >= base) & (lane < base + EXPERTS_PER_GROUP)
    le = jnp.where(in_sel, logits, neg)
    m1 = jnp.max(le, axis=-1, keepdims=True)
    i1 = jnp.min(jnp.where(le == m1, lane, big), axis=-1, keepdims=True)
    le2 = jnp.where(lane == i1, neg, le)
    m2 = jnp.max(le2, axis=-1, keepdims=True)
    i2 = jnp.min(jnp.where(le2 == m2, lane, big), axis=-1, keepdims=True)
    e2 = jnp.exp(m2 - m1)
    w1 = pg_top / (1.0 + e2)
    w2 = pg_top * e2 / (1.0 + e2)
    j1 = i1 - base + ROUTE_W0
    j2 = i2 - base + ROUTE_W0
    rec = jnp.where(lane == ROUTE_GID, gid.astype(jnp.float32),
                    jnp.where(lane == j1, w1, jnp.where(lane == j2, w2, 0.0)))
    route_out[...] = rec


def _mix_call(x2d, oa, ob, lw):
    t = x2d.shape[0]
    tm = TM_PROJ
    row = lambda i: (i, 0)
    weights = [lw["g_mla_out"], lw["g_na_out"], lw["w_out"], lw["g_ffn"], lw["w_r_hi"], lw["w_r_lo"], lw["b_r"]]
    return pl.pallas_call(
        _mix_kernel, grid=(t // tm,),
        in_specs=[pl.BlockSpec((tm, D_MODEL), row), pl.BlockSpec((tm, MLA_WIDTH), row),
                  pl.BlockSpec((tm, NA_WIDTH), row)] + [_full(w.shape) for w in weights],
        out_specs=[pl.BlockSpec((tm, D_MODEL), row), pl.BlockSpec((tm, D_MODEL), row), pl.BlockSpec((tm, LANES), row)],
        out_shape=[jax.ShapeDtypeStruct((t, D_MODEL), jnp.float32),
                   jax.ShapeDtypeStruct((t, D_MODEL), jnp.bfloat16),
                   jax.ShapeDtypeStruct((t, LANES), jnp.float32)],
        compiler_params=_cparams(("parallel",)), name="mix_router",
    )(x2d, oa, ob, *weights)


def _moe_kernel(x1_ref, h2_ref, route_ref, wgu_ref, wd_ref, o_ref,
                xs_ref, cs_ref, ys_ref, pt_ref, cnt_ref):
    g = pl.program_id(1)
    tm = TM_MOE

    @pl.when(g == 0)
    def _partition():
        route = route_ref[...]
        lane = lax.broadcasted_iota(jnp.int32, route.shape, 1)
        gid = jnp.sum(jnp.where(lane == ROUTE_GID, route, 0.0), axis=-1, keepdims=True)
        onehot = jnp.where((lane.astype(jnp.float32) == gid) & (lane < N_GROUPS), 1.0, 0.0)
        r_i = lax.broadcasted_iota(jnp.int32, (tm, tm), 0)
        c_i = lax.broadcasted_iota(jnp.int32, (tm, tm), 1)
        tri = jnp.where(c_i < r_i, 1.0, 0.0).astype(jnp.bfloat16)
        rank = _dot(tri, _bf16(onehot))
        starts = []
        acc = jnp.int32(0)
        for gg in range(N_GROUPS):
            n = jnp.sum(jnp.where(lane == gg, onehot, 0.0)).astype(jnp.int32)
            cnt_ref[gg] = acc
            cnt_ref[N_GROUPS + gg] = lax.div(n + (MOE_CHUNK - 1), jnp.int32(MOE_CHUNK))
            starts.append(acc)
            acc = acc + lax.div(n + (BF16_ROWS - 1), jnp.int32(BF16_ROWS)) * BF16_ROWS
        start_vec = jnp.zeros_like(route)
        for gg in range(N_GROUPS):
            start_vec = jnp.where(lane == gg, starts[gg].astype(jnp.float32), start_vec)
        dest = jnp.sum(onehot * (start_vec + rank), axis=-1, keepdims=True)
        dest_i = dest.astype(jnp.int32)
        col = lax.broadcasted_iota(jnp.int32, (tm, MOE_ROWS), 1)
        pt_ref[...] = jnp.where(col == dest_i, 1.0, 0.0).astype(jnp.bfloat16)
        dest_row = jnp.transpose(jnp.broadcast_to(dest, (tm, LANES)))[0:1, :]
        rowi = lax.broadcasted_iota(jnp.int32, (MOE_ROWS, tm), 0)
        perm = jnp.where(rowi == dest_row.astype(jnp.int32), 1.0, 0.0).astype(jnp.bfloat16)
        xs_ref[...] = _bf16(_dot(perm, h2_ref[...]))
        c_hi = _bf16(route)
        c_lo = _bf16(route - c_hi.astype(jnp.float32))
        cs_ref[...] = _dot(perm, c_hi) + _dot(perm, c_lo)
        ys_ref[...] = jnp.zeros_like(ys_ref)

    start_g = cnt_ref[g]
    n_chunks = cnt_ref[N_GROUPS + g]

    def _chunk(c, carry):
        off = pl.multiple_of(start_g + c * MOE_CHUNK, BF16_ROWS)
        xc = xs_ref[pl.ds(off, MOE_CHUNK), :]
        cw = cs_ref[pl.ds(off, MOE_CHUNK), :]
        au = _dot(xc, wgu_ref[0])
        hid = []
        for j in range(EXPERTS_PER_GROUP):
            a = au[:, j * D_EXPERT:(j + 1) * D_EXPERT]
            u = au[:, (EXPERTS_PER_GROUP + j) * D_EXPERT:(EXPERTS_PER_GROUP + j + 1) * D_EXPERT]
            hid.append(a * (1.0 / (1.0 + jnp.exp(-a))) * u * cw[:, ROUTE_W0 + j:ROUTE_W0 + j + 1])
        hid = _bf16(jnp.concatenate(hid, axis=-1))
        ys_ref[pl.ds(off, MOE_CHUNK), :] = _bf16(_dot(hid, wd_ref[0]))
        return carry

    lax.fori_loop(0, n_chunks, _chunk, 0)

    @pl.when(g == N_GROUPS - 1)
    def _unpermute():
        o_ref[...] = x1_ref[...] + _dot(pt_ref[...], ys_ref[...])


def _moe_call(x1, h2, route, lw):
    t = x1.shape[0]
    tm = TM_MOE
    row = lambda i, g: (i, 0)
    return pl.pallas_call(
        _moe_kernel, grid=(t // tm, N_GROUPS),
        in_specs=[pl.BlockSpec((tm, D_MODEL), row), pl.BlockSpec((tm, D_MODEL), row), pl.BlockSpec((tm, LANES), row),
                  pl.BlockSpec((1, D_MODEL, 2 * EXPERTS_PER_GROUP * D_EXPERT), lambda i, g: (g, 0, 0)),
                  pl.BlockSpec((1, EXPERTS_PER_GROUP * D_EXPERT, D_MODEL), lambda i, g: (g, 0, 0))],
        out_specs=pl.BlockSpec((tm, D_MODEL), row),
        out_shape=jax.ShapeDtypeStruct((t, D_MODEL), jnp.float32),
        scratch_shapes=[pltpu.VMEM((MOE_ROWS, D_MODEL), jnp.bfloat16),
                        pltpu.VMEM((MOE_ROWS, LANES), jnp.float32),
                        pltpu.VMEM((MOE_ROWS, D_MODEL), jnp.bfloat16),
                        pltpu.VMEM((tm, MOE_ROWS), jnp.bfloat16),
                        pltpu.SMEM((2 * N_GROUPS,), jnp.int32)],
        compiler_params=_cparams(("parallel", "arbitrary")), name="moe",
    )(x1, h2, route, lw["w_gu"], lw["w_d"])


def _segment_matrix(width, segments):
    m = np.zeros((width, width), np.float32)
    for lo, hi in segments:
        m[lo:hi, lo:hi] = 1.0 / (hi - lo)
    return jnp.asarray(m, jnp.bfloat16)


def _constants(seq):
    t = np.arange(seq)
    row = (t // GRID_W).astype(np.float32)
    col = (t % GRID_W).astype(np.float32)
    n_freq = MLA_ROPE // 4
    inv = (np.float32(ROPE_THETA) ** (-np.arange(n_freq, dtype=np.float32) / n_freq)).astype(np.float32)
    ang_r = row[:, None] * inv[None, :]
    ang_c = col[:, None] * inv[None, :]

    def tables(base):
        cos = np.zeros((seq, LANES), np.float32)
        cos[:, :base] = 1.0
        sa = np.zeros((seq, LANES), np.float32)
        sb = np.zeros((seq, LANES), np.float32)
        for k, ang in enumerate((ang_r, ang_c)):
            o = base + 2 * n_freq * k
            cos[:, o:o + n_freq] = np.cos(ang)
            cos[:, o + n_freq:o + 2 * n_freq] = np.cos(ang)
            sa[:, o:o + n_freq] = -np.sin(ang)
            sb[:, o + n_freq:o + 2 * n_freq] = np.sin(ang)
        return jnp.asarray(cos), jnp.asarray(sa), jnp.asarray(sb)

    cos_q, sin_qa, sin_qb = tables(MLA_NOPE)
    cos_k, sin_ka, sin_kb = tables(0)
    seg_q, seg_k = [], []
    for hb in range(0, MXU_TILE, HEAD_PAD):
        seg_q += [(hb, hb + MLA_NOPE), (hb + MLA_NOPE, hb + MLA_QK)]
        seg_k += [(hb, hb + MLA_NOPE)]
    place = np.zeros((LANES, MLA_PAD_WIDTH), np.float32)
    for hd in range(MLA_HEADS):
        for j in range(MLA_ROPE):
            place[j, hd * HEAD_PAD + MLA_NOPE + j] = 1.0
    return dict(cos_q=cos_q, sin_qa=sin_qa, sin_qb=sin_qb, cos_k=cos_k, sin_ka=sin_ka, sin_kb=sin_kb,
                m_q=_segment_matrix(MXU_TILE, seg_q), m_k=_segment_matrix(MXU_TILE, seg_k),
                m_kpe=_segment_matrix(LANES, [(0, MLA_ROPE)]),
                m_na=_segment_matrix(MXU_TILE, [(o, o + NA_DH) for o in range(0, MXU_TILE, NA_DH)]),
                place=jnp.asarray(place, jnp.bfloat16))


def _na_bias_table(rpb, rows):
    kr = min(NA_KR_MAX, rows)
    cols = jnp.arange(GRID_W)
    cs = jnp.clip(cols - NA_KC // 2, 0, GRID_W - NA_KC)
    col_mask = (cols[None, :] >= cs[:, None]) & (cols[None, :] < cs[:, None] + NA_KC)
    dc = jnp.clip(cols[None, :] - cols[:, None], -(NA_KC - 1), NA_KC - 1) + NA_KC - 1
    rpb_c = rpb[:, :, dc]
    tabs = []
    for c in range(kr):
        dr = jnp.arange(kr) + (NA_KR_MAX - 1) - c
        b = rpb_c[:, dr].transpose(0, 2, 1, 3)
        b = jnp.where(col_mask[None, :, None, :], b * LOG2E, MASK_NEG)
        tabs.append(b.reshape(NA_HEADS, GRID_W, kr * GRID_W))
    return jnp.stack(tabs).astype(jnp.float32)


def _pad_heads(w, width, pad):
    k = w.shape[0]
    w = w.reshape(k, -1, width)
    return jnp.pad(w, ((0, 0), (0, 0), (0, pad - width))).reshape(k, -1)


def _layer_weights(l, p, rows):
    f32 = jnp.float32
    w_in = p["w_in"][l]
    zeros = jnp.zeros((D_MODEL, LANES - MLA_ROPE), f32)
    w_in_p = jnp.concatenate([w_in[:, :Q_LORA + KV_LORA + MLA_ROPE], zeros, w_in[:, Q_LORA + KV_LORA + MLA_ROPE:]], axis=1)
    w_kv = p["w_kv_b"][l].reshape(KV_LORA, MLA_HEADS, MLA_NOPE + MLA_V)
    w_kb = jnp.pad(w_kv[:, :, :MLA_NOPE], ((0, 0), (0, 0), (0, HEAD_PAD - MLA_NOPE))).reshape(KV_LORA, MLA_PAD_WIDTH)
    w_v = w_kv[:, :, MLA_NOPE:].reshape(KV_LORA, MLA_WIDTH)
    scale = MLA_QK ** -0.5
    g_q = jnp.tile(jnp.pad(p["g_mla_q"][l] * (scale * LOG2E), (0, HEAD_PAD - MLA_QK)), MLA_HEADS)[None]
    g_k = jnp.tile(jnp.pad(p["g_mla_k"][l][:MLA_NOPE], (0, HEAD_PAD - MLA_NOPE)), MLA_HEADS)[None]
    g_kpe = jnp.pad(p["g_mla_k"][l][MLA_NOPE:], (0, LANES - MLA_ROPE))[None]
    w_r = jnp.concatenate([p["w_router_group"][l], p["w_router_expert"][l]], axis=1)
    w_r = jnp.pad(w_r, ((0, 0), (0, LANES - w_r.shape[1])))
    w_r_hi = w_r.astype(jnp.bfloat16)
    w_r_lo = (w_r - w_r_hi.astype(f32)).astype(jnp.bfloat16)
    b_r = jnp.pad(jnp.concatenate([p["b_router_group"][l], p["b_router_expert"][l]]), (0, LANES - N_GROUPS * (1 + EXPERTS_PER_GROUP)))[None]
    wg = p["w_gate"][l].reshape(N_GROUPS, EXPERTS_PER_GROUP, D_MODEL, D_EXPERT).transpose(0, 2, 1, 3).reshape(N_GROUPS, D_MODEL, -1)
    wu = p["w_up"][l].reshape(N_GROUPS, EXPERTS_PER_GROUP, D_MODEL, D_EXPERT).transpose(0, 2, 1, 3).reshape(N_GROUPS, D_MODEL, -1)
    w_gu = jnp.concatenate([wg, wu], axis=-1).astype(jnp.bfloat16)
    w_d = p["w_down"][l].reshape(N_GROUPS, EXPERTS_PER_GROUP * D_EXPERT, D_MODEL).astype(jnp.bfloat16)
    return dict(
        g_mix=p["g_mix_norm"][l][None], w_in=w_in_p.astype(jnp.bfloat16),
        g_q_a=p["g_q_a"][l][None], w_qb=_pad_heads(p["w_q_b"][l], MLA_QK, HEAD_PAD).astype(jnp.bfloat16),
        g_kv_a=p["g_kv_a"][l][None], w_kb=w_kb.astype(jnp.bfloat16), w_v=w_v.astype(jnp.bfloat16),
        g_q=g_q, g_k=g_k, g_kpe=g_kpe,
        g_naq=jnp.tile(p["g_na_q"][l] * (NA_DH ** -0.5 * LOG2E), NA_HEADS)[None],
        g_nak=jnp.tile(p["g_na_k"][l], NA_HEADS)[None],
        na_bias=_na_bias_table(p["na_rpb"][l], rows),
        g_mla_out=p["g_mla_out"][l][None], g_na_out=p["g_na_out"][l][None],
        w_out=p["w_out"][l].astype(jnp.bfloat16), g_ffn=p["g_ffn_norm"][l][None],
        w_r_hi=w_r_hi, w_r_lo=w_r_lo, b_r=b_r, w_gu=w_gu, w_d=w_d)


def kernel(x, g_mix_norm, w_in, g_q_a, w_q_b, g_kv_a, w_kv_b, g_mla_q, g_mla_k, g_na_q, g_na_k, na_rpb, g_mla_out, g_na_out, w_out, g_ffn_norm, w_router_group, b_router_group, w_router_expert, b_router_expert, w_gate, w_up, w_down):
    batch, seq, d = x.shape
    assert d == D_MODEL and seq % TM_PROJ == 0 and seq % TQ_MLA == 0 and (batch * seq) % TM_MOE == 0
    rows = seq // GRID_W
    assert rows % NA_ROWS_PER_STEP == 0 and rows >= NA_KR_MAX
    p = dict(g_mix_norm=g_mix_norm, w_in=w_in, g_q_a=g_q_a, w_q_b=w_q_b, g_kv_a=g_kv_a, w_kv_b=w_kv_b,
             g_mla_q=g_mla_q, g_mla_k=g_mla_k, g_na_q=g_na_q, g_na_k=g_na_k, na_rpb=na_rpb,
             g_mla_out=g_mla_out, g_na_out=g_na_out, w_out=w_out, g_ffn_norm=g_ffn_norm,
             w_router_group=w_router_group, b_router_group=b_router_group,
             w_router_expert=w_router_expert, b_router_expert=b_router_expert,
             w_gate=w_gate, w_up=w_up, w_down=w_down)
    consts = _constants(seq)
    xf = x.reshape(batch * seq, d)
    for l in range(w_in.shape[0]):
        lw = _layer_weights(l, p, rows)
        q, k, v, naq, nak, nav = _proj_call(xf, lw, consts, seq)
        o_a = _mla_call(q, k, v, batch, seq)
        o_b = _na_call(naq, nak, nav, lw["na_bias"], batch, seq)
        x1, h2, route = _mix_call(xf, o_a, o_b, lw)
        xf = _moe_call(x1, h2, route, lw)
    return xf.reshape(batch, seq, d)
```

```python
import functools
import math

import numpy as np
import jax
import jax.numpy as jnp
from jax import lax
from jax.experimental import pallas as pl
from jax.experimental.pallas import tpu as pltpu

D_MODEL = 1024
GRID_W = 64
MLA_HEADS = 8
MLA_NOPE = 64
MLA_ROPE = 32
MLA_V = 64
MLA_QK = MLA_NOPE + MLA_ROPE
Q_LORA = 256
KV_LORA = 128
MLA_WIDTH = MLA_HEADS * MLA_V
ROPE_THETA = 10000.0
NA_HEADS = 8
NA_DH = 64
NA_WIDTH = NA_HEADS * NA_DH
NA_KR_MAX = 8
NA_KC = 16
N_GROUPS = 4
EXPERTS_PER_GROUP = 4
D_EXPERT = 256
EPS = 1e-6

LANES = 128
MXU_TILE = 256
BF16_ROWS = 16
HEAD_PAD = LANES
MLA_PAD_WIDTH = MLA_HEADS * HEAD_PAD
PROJ_COLS = 2048
LOG2E = 1.4426950408889634
MASK_NEG = -1e30
VMEM_LIMIT = 56 * 1024 * 1024

C_CQ = 0
C_CKV = C_CQ + Q_LORA
C_KPE = C_CKV + KV_LORA
C_NAQ = C_KPE + LANES
C_NAK = C_NAQ + NA_WIDTH
C_NAV = C_NAK + NA_WIDTH

TM_PROJ = 512
TQ_MLA = 512
MLA_HEADS_PER_STEP = 4
NA_ROWS_PER_STEP = 4
TM_MOE = 1024
MOE_CHUNK = 128
MOE_ROWS = TM_MOE + N_GROUPS * BF16_ROWS + MOE_CHUNK + 64
ROUTE_GID = 0
ROUTE_W0 = 1
ROUTE_LO = 8
H2_COLS = D_MODEL + LANES


def _cparams(sem):
    return pltpu.CompilerParams(dimension_semantics=sem, vmem_limit_bytes=VMEM_LIMIT)


def _full(shape):
    nd = len(shape)
    return pl.BlockSpec(shape, lambda *_: (0,) * nd)


def _bf16(x):
    return x.astype(jnp.bfloat16)


def _dot(a, b):
    return jnp.dot(a, b, preferred_element_type=jnp.float32)


def _dot_nt(a, b):
    return lax.dot_general(a, b, (((1,), (1,)), ((), ())), preferred_element_type=jnp.float32)


def _rms_rows(x, gain):
    ms = jnp.mean(x * x, axis=-1, keepdims=True)
    return x * lax.rsqrt(ms + EPS) * gain


def _segment_rms_scale(x, seg_mat):
    n = x.shape[-1]
    parts = []
    for j in range(0, n, seg_mat.shape[0]):
        w = min(seg_mat.shape[0], n - j)
        xs = x[:, j:j + w]
        parts.append(_dot(_bf16(xs * xs), seg_mat[:w, :w]))
    ms = parts[0] if len(parts) == 1 else jnp.concatenate(parts, axis=-1)
    return lax.rsqrt(ms + EPS)


def _rope_block(x, cos, sin_a, sin_b):
    return x * cos + pltpu.roll(x, LANES - 8, 1) * sin_a + pltpu.roll(x, 8, 1) * sin_b


def _proj_kernel(x_ref, gmix_ref, win_ref, gqa_ref, wqb_ref, gkva_ref, wkb_ref, wv_ref,
                 gq_ref, gk_ref, gkpe_ref, gnaq_ref, gnak_ref,
                 mq_ref, mk_ref, mkpe_ref, mna_ref, place_ref,
                 cq_ref, sqa_ref, sqb_ref, ck_ref, ska_ref, skb_ref,
                 q_out, k_out, v_out, naq_out, nak_out, nav_out):
    x = x_ref[...]
    h = _bf16(_rms_rows(x, gmix_ref[...]))
    proj = _dot(h, win_ref[...])

    cqn = _bf16(_rms_rows(proj[:, C_CQ:C_CKV], gqa_ref[...]))
    q = _dot(cqn, wqb_ref[...])
    q = q * _segment_rms_scale(q, mq_ref[...]) * gq_ref[...]
    cosq, sqa, sqb = cq_ref[...], sqa_ref[...], sqb_ref[...]
    for hd in range(MLA_HEADS):
        sl = slice(hd * HEAD_PAD, (hd + 1) * HEAD_PAD)
        q_out[:, sl] = _bf16(_rope_block(q[:, sl], cosq, sqa, sqb))

    ckvn = _bf16(_rms_rows(proj[:, C_CKV:C_KPE], gkva_ref[...]))
    kn = _dot(ckvn, wkb_ref[...])
    kn = kn * _segment_rms_scale(kn, mk_ref[...]) * gk_ref[...]
    v_out[...] = _bf16(_dot(ckvn, wv_ref[...]))
    kp = proj[:, C_KPE:C_NAQ]
    kp = kp * lax.rsqrt(_dot(_bf16(kp * kp), mkpe_ref[...]) + EPS) * gkpe_ref[...]
    kp = _rope_block(kp, ck_ref[...], ska_ref[...], skb_ref[...])
    k_out[...] = _bf16(kn + _dot(_bf16(kp), place_ref[...]))

    naq = proj[:, C_NAQ:C_NAK]
    naq_out[...] = _bf16(naq * _segment_rms_scale(naq, mna_ref[...]) * gnaq_ref[...])
    nak = proj[:, C_NAK:C_NAV]
    nak_out[...] = _bf16(nak * _segment_rms_scale(nak, mna_ref[...]) * gnak_ref[...])
    nav_out[...] = _bf16(proj[:, C_NAV:PROJ_COLS])


def _proj_call(x2d, lw, consts, seq):
    t = x2d.shape[0]
    tm = TM_PROJ
    n_seq_tiles = seq // tm
    row = lambda i: (i, 0)
    tab = lambda i: (i % n_seq_tiles, 0)
    weights = [lw["g_mix"], lw["w_in"], lw["g_q_a"], lw["w_qb"], lw["g_kv_a"], lw["w_kb"], lw["w_v"],
               lw["g_q"], lw["g_k"], lw["g_kpe"], lw["g_naq"], lw["g_nak"],
               consts["m_q"], consts["m_k"], consts["m_kpe"], consts["m_na"], consts["place"]]
    tables = [consts["cos_q"], consts["sin_qa"], consts["sin_qb"], consts["cos_k"], consts["sin_ka"], consts["sin_kb"]]
    in_specs = ([pl.BlockSpec((tm, D_MODEL), row)] + [_full(w.shape) for w in weights]
                + [pl.BlockSpec((tm, LANES), tab) for _ in tables])
    out_shapes = [jax.ShapeDtypeStruct((t, MLA_PAD_WIDTH), jnp.bfloat16),
                  jax.ShapeDtypeStruct((t, MLA_PAD_WIDTH), jnp.bfloat16),
                  jax.ShapeDtypeStruct((t, MLA_WIDTH), jnp.bfloat16),
                  jax.ShapeDtypeStruct((t, NA_WIDTH), jnp.bfloat16),
                  jax.ShapeDtypeStruct((t, NA_WIDTH), jnp.bfloat16),
                  jax.ShapeDtypeStruct((t, NA_WIDTH), jnp.bfloat16)]
    out_specs = [pl.BlockSpec((tm, s.shape[1]), row) for s in out_shapes]
    return pl.pallas_call(
        _proj_kernel, grid=(t // tm,), in_specs=in_specs, out_specs=out_specs, out_shape=out_shapes,
        compiler_params=_cparams(("parallel",)), name="proj",
    )(x2d, *weights, *tables)


def _mla_kernel(q_ref, k_ref, v_ref, o_ref):
    scores = []
    for hh in range(MLA_HEADS_PER_STEP):
        sl = slice(hh * HEAD_PAD, (hh + 1) * HEAD_PAD)
        scores.append(_dot_nt(q_ref[:, sl], k_ref[:, sl]))
    outs = []
    for hh, s in enumerate(scores):
        m = jnp.max(s, axis=-1, keepdims=True)
        p = jnp.exp2(s - m)
        l = jnp.sum(p, axis=-1, keepdims=True)
        pair = slice((hh // 2) * LANES, (hh // 2 + 1) * LANES)
        outs.append(_dot(_bf16(p), v_ref[:, pair]) * (1.0 / l))
    lane = lax.broadcasted_iota(jnp.int32, outs[0].shape, 1)
    for pp in range(MLA_HEADS_PER_STEP // 2):
        o_ref[:, pp * LANES:(pp + 1) * LANES] = jnp.where(lane < MLA_V, outs[2 * pp], outs[2 * pp + 1])


def _mla_call(q, k, v, batch, seq):
    t = q.shape[0]
    tq = TQ_MLA
    nq = seq // tq
    hps = MLA_HEADS_PER_STEP
    return pl.pallas_call(
        _mla_kernel, grid=(batch, MLA_HEADS // hps, nq),
        in_specs=[pl.BlockSpec((tq, hps * HEAD_PAD), lambda b, p, i: (b * nq + i, p)),
                  pl.BlockSpec((seq, hps * HEAD_PAD), lambda b, p, i: (b, p)),
                  pl.BlockSpec((seq, hps * MLA_V), lambda b, p, i: (b, p))],
        out_specs=pl.BlockSpec((tq, hps * MLA_V), lambda b, p, i: (b * nq + i, p)),
        out_shape=jax.ShapeDtypeStruct((t, MLA_WIDTH), jnp.float32),
        compiler_params=_cparams(("parallel", "parallel", "arbitrary")), name="mla_attn",
    )(q, k, v)


def _na_kernel(q_ref, k_ref, v_ref, bias_ref, o_ref, *, rows, kr):
    step = pl.program_id(1)
    band = kr * GRID_W
    lane = lax.broadcasted_iota(jnp.int32, (GRID_W, LANES), 1)
    units = []
    for rr in range(NA_ROWS_PER_STEP):
        r = step * NA_ROWS_PER_STEP + rr
        rs = jnp.clip(r - kr // 2, 0, rows - kr)
        cls = r - rs
        start = pl.multiple_of(rs * GRID_W, GRID_W)
        qrow = q_ref[rr * GRID_W:(rr + 1) * GRID_W, :]
        for pr in range(NA_HEADS // 2):
            sl = slice(pr * LANES, (pr + 1) * LANES)
            q2 = qrow[:, sl]
            zero = jnp.zeros_like(q2)
            qq = jnp.concatenate([jnp.where(lane < NA_DH, q2, zero), jnp.where(lane >= NA_DH, q2, zero)], axis=0)
            s = _dot_nt(qq, k_ref[pl.ds(start, band), sl])
            units.append((rr, pr, start, cls, s))
    probs = []
    for rr, pr, start, cls, s in units:
        s = s + bias_ref[cls, 2 * pr:2 * pr + 2].reshape(2 * GRID_W, band)
        m = jnp.max(s, axis=-1, keepdims=True)
        p = jnp.exp2(s - m)
        l = jnp.sum(p, axis=-1, keepdims=True)
        probs.append((_bf16(p), 1.0 / l))
    for (rr, pr, start, cls, s), (p, rl) in zip(units, probs):
        sl = slice(pr * LANES, (pr + 1) * LANES)
        pv = _dot(p, v_ref[pl.ds(start, band), sl]) * rl
        o_ref[rr * GRID_W:(rr + 1) * GRID_W, sl] = jnp.where(lane < NA_DH, pv[:GRID_W], pv[GRID_W:])


def _na_call(q, k, v, bias, batch, seq):
    t = q.shape[0]
    rows = seq // GRID_W
    kr = min(NA_KR_MAX, rows)
    steps = rows // NA_ROWS_PER_STEP
    blk = NA_ROWS_PER_STEP * GRID_W
    return pl.pallas_call(
        functools.partial(_na_kernel, rows=rows, kr=kr), grid=(batch, steps),
        in_specs=[pl.BlockSpec((blk, NA_WIDTH), lambda b, i: (b * steps + i, 0)),
                  pl.BlockSpec((seq, NA_WIDTH), lambda b, i: (b, 0)),
                  pl.BlockSpec((seq, NA_WIDTH), lambda b, i: (b, 0)),
                  _full(bias.shape)],
        out_specs=pl.BlockSpec((blk, NA_WIDTH), lambda b, i: (b * steps + i, 0)),
        out_shape=jax.ShapeDtypeStruct((t, NA_WIDTH), jnp.float32),
        compiler_params=_cparams(("parallel", "arbitrary")), name="na_attn",
    )(q, k, v, bias)


def _mix_kernel(x_ref, oa_ref, ob_ref, ga_ref, gb_ref, wout_ref, gffn_ref, wrh_ref, wrl_ref, br_ref,
                x1_out, h2_out):
    mixed = jnp.concatenate([_rms_rows(oa_ref[...], ga_ref[...]), _rms_rows(ob_ref[...], gb_ref[...])], axis=-1)
    x1 = x_ref[...] + _dot(_bf16(mixed), wout_ref[...])
    x1_out[...] = x1
    h2 = _rms_rows(x1, gffn_ref[...])
    hi = _bf16(h2)
    lo = _bf16(h2 - hi.astype(jnp.float32))
    h2_out[:, :D_MODEL] = hi
    logits = _dot(hi, wrh_ref[...]) + _dot(lo, wrh_ref[...]) + _dot(hi, wrl_ref[...]) + br_ref[...]

    lane = lax.broadcasted_iota(jnp.int32, logits.shape, 1)
    neg = jnp.float32(-jnp.inf)
    big = jnp.int32(LANES)
    is_grp = lane < N_GROUPS
    lg = jnp.where(is_grp, logits, neg)
    gmax = jnp.max(lg, axis=-1, keepdims=True)
    gid = jnp.min(jnp.where(lg == gmax, lane, big), axis=-1, keepdims=True)
    pg_top = 1.0 / jnp.sum(jnp.where(is_grp, jnp.exp(logits - gmax), 0.0), axis=-1, keepdims=True)
    base = N_GROUPS + EXPERTS_PER_GROUP * gid
    in_sel = (lane >= base) & (lane < base + EXPERTS_PER_GROUP)
    le = jnp.where(in_sel, logits, neg)
    m1 = jnp.max(le, axis=-1, keepdims=True)
    i1 = jnp.min(jnp.where(le == m1, lane, big), axis=-1, keepdims=True)
    le2 = jnp.where(lane == i1, neg, le)
    m2 = jnp.max(le2, axis=-1, keepdims=True)
    i2 = jnp.min(jnp.where(le2 == m2, lane, big), axis=-1, keepdims=True)
    e2 = jnp.exp(m2 - m1)
    w1 = pg_top / (1.0 + e2)
    w2 = pg_top * e2 / (1.0 + e2)
    j1 = i1 - base + ROUTE_W0
    j2 = i2 - base + ROUTE_W0
    w1h = _bf16(w1).astype(jnp.float32)
    w2h = _bf16(w2).astype(jnp.float32)
    rec = jnp.where(lane == ROUTE_GID, gid.astype(jnp.float32),
                    jnp.where(lane == j1, w1h, jnp.where(lane == j2, w2h,
                    jnp.where(lane == j1 + ROUTE_LO, w1 - w1h, jnp.where(lane == j2 + ROUTE_LO, w2 - w2h, 0.0)))))
    h2_out[:, D_MODEL:] = _bf16(rec)


def _mix_call(x2d, oa, ob, lw):
    t = x2d.shape[0]
    tm = TM_PROJ
    row = lambda i: (i, 0)
    weights = [lw["g_mla_out"], lw["g_na_out"], lw["w_out"], lw["g_ffn"], lw["w_r_hi"], lw["w_r_lo"], lw["b_r"]]
    return pl.pallas_call(
        _mix_kernel, grid=(t // tm,),
        in_specs=[pl.BlockSpec((tm, D_MODEL), row), pl.BlockSpec((tm, MLA_WIDTH), row),
                  pl.BlockSpec((tm, NA_WIDTH), row)] + [_full(w.shape) for w in weights],
        out_specs=[pl.BlockSpec((tm, D_MODEL), row), pl.BlockSpec((tm, H2_COLS), row)],
        out_shape=[jax.ShapeDtypeStruct((t, D_MODEL), jnp.float32),
                   jax.ShapeDtypeStruct((t, H2_COLS), jnp.bfloat16)],
        compiler_params=_cparams(("parallel",)), name="mix_router",
    )(x2d, oa, ob, *weights)


def _moe_kernel(x1_ref, h2_ref, wg_ref, wu_ref, wd_ref, o_ref,
                xs_ref, ys_ref, pt_ref, cnt_ref):
    g = pl.program_id(1)
    tm = TM_MOE

    @pl.when(g == 0)
    def _partition():
        route = h2_ref[:, D_MODEL:].astype(jnp.float32)
        lane = lax.broadcasted_iota(jnp.int32, route.shape, 1)
        gid = jnp.sum(jnp.where(lane == ROUTE_GID, route, 0.0), axis=-1, keepdims=True)
        onehot = jnp.where((lane.astype(jnp.float32) == gid) & (lane < N_GROUPS), 1.0, 0.0)
        r_i = lax.broadcasted_iota(jnp.int32, (tm, tm), 0)
        c_i = lax.broadcasted_iota(jnp.int32, (tm, tm), 1)
        tri = jnp.where(c_i < r_i, 1.0, 0.0).astype(jnp.bfloat16)
        rank = _dot(tri, _bf16(onehot))
        starts = []
        acc = jnp.int32(0)
        for gg in range(N_GROUPS):
            n = jnp.sum(jnp.where(lane == gg, onehot, 0.0)).astype(jnp.int32)
            cnt_ref[gg] = acc
            cnt_ref[N_GROUPS + gg] = lax.div(n + (MOE_CHUNK - 1), jnp.int32(MOE_CHUNK))
            starts.append(acc)
            acc = acc + lax.div(n + (BF16_ROWS - 1), jnp.int32(BF16_ROWS)) * BF16_ROWS
        start_vec = jnp.zeros_like(route)
        for gg in range(N_GROUPS):
            start_vec = jnp.where(lane == gg, starts[gg].astype(jnp.float32), start_vec)
        dest = jnp.sum(onehot * (start_vec + rank), axis=-1, keepdims=True)
        dest_i = dest.astype(jnp.int32)
        col = lax.broadcasted_iota(jnp.int32, (tm, MOE_ROWS), 1)
        pt_ref[...] = jnp.where(col == dest_i, 1.0, 0.0).astype(jnp.bfloat16)
        dest_row = jnp.transpose(jnp.broadcast_to(dest, (tm, LANES)))[0:1, :]
        rowi = lax.broadcasted_iota(jnp.int32, (MOE_ROWS, tm), 0)
        perm = jnp.where(rowi == dest_row.astype(jnp.int32), 1.0, 0.0).astype(jnp.bfloat16)
        xs_ref[...] = _bf16(_dot(perm, h2_ref[...]))
        ys_ref[...] = jnp.zeros_like(ys_ref)

    start_g = cnt_ref[g]
    n_chunks = cnt_ref[N_GROUPS + g]

    def _chunk(c, carry):
        off = pl.multiple_of(start_g + c * MOE_CHUNK, BF16_ROWS)
        xc = xs_ref[pl.ds(off, MOE_CHUNK), :D_MODEL]
        cw = xs_ref[pl.ds(off, MOE_CHUNK), D_MODEL:].astype(jnp.float32)
        hid = []
        for j in range(EXPERTS_PER_GROUP):
            a = _dot(xc, wg_ref[j])
            u = _dot(xc, wu_ref[j])
            c = cw[:, ROUTE_W0 + j:ROUTE_W0 + j + 1] + cw[:, ROUTE_W0 + ROUTE_LO + j:ROUTE_W0 + ROUTE_LO + j + 1]
            hid.append(a * (1.0 / (1.0 + jnp.exp(-a))) * u * c)
        hid = _bf16(jnp.concatenate(hid, axis=-1))
        w_down = wd_ref[...].reshape(EXPERTS_PER_GROUP * D_EXPERT, D_MODEL)
        ys_ref[pl.ds(off, MOE_CHUNK), :] = _bf16(_dot(hid, w_down))
        return carry

    lax.fori_loop(0, n_chunks, _chunk, 0)

    @pl.when(g == N_GROUPS - 1)
    def _unpermute():
        o_ref[...] = x1_ref[...] + _dot(pt_ref[...], ys_ref[...])


def _moe_call(x1, h2, lw):
    t = x1.shape[0]
    tm = TM_MOE
    row = lambda i, g: (i, 0)
    return pl.pallas_call(
        _moe_kernel, grid=(t // tm, N_GROUPS),
        in_specs=[pl.BlockSpec((tm, D_MODEL), row), pl.BlockSpec((tm, H2_COLS), row),
                  pl.BlockSpec((EXPERTS_PER_GROUP, D_MODEL, D_EXPERT), lambda i, g: (g, 0, 0)),
                  pl.BlockSpec((EXPERTS_PER_GROUP, D_MODEL, D_EXPERT), lambda i, g: (g, 0, 0)),
                  pl.BlockSpec((EXPERTS_PER_GROUP, D_EXPERT, D_MODEL), lambda i, g: (g, 0, 0))],
        out_specs=pl.BlockSpec((tm, D_MODEL), row),
        out_shape=jax.ShapeDtypeStruct((t, D_MODEL), jnp.float32),
        scratch_shapes=[pltpu.VMEM((MOE_ROWS, H2_COLS), jnp.bfloat16),
                        pltpu.VMEM((MOE_ROWS, D_MODEL), jnp.bfloat16),
                        pltpu.VMEM((tm, MOE_ROWS), jnp.bfloat16),
                        pltpu.SMEM((2 * N_GROUPS,), jnp.int32)],
        compiler_params=_cparams(("parallel", "arbitrary")), name="moe",
    )(x1, h2, lw["w_g"], lw["w_u"], lw["w_d"])


def _segment_matrix(width, segments):
    m = np.zeros((width, width), np.float32)
    for lo, hi in segments:
        m[lo:hi, lo:hi] = 1.0 / (hi - lo)
    return jnp.asarray(m, jnp.bfloat16)


def _constants(seq):
    t = np.arange(seq)
    row = (t // GRID_W).astype(np.float32)
    col = (t % GRID_W).astype(np.float32)
    n_freq = MLA_ROPE // 4
    inv = (np.float32(ROPE_THETA) ** (-np.arange(n_freq, dtype=np.float32) / n_freq)).astype(np.float32)
    ang_r = row[:, None] * inv[None, :]
    ang_c = col[:, None] * inv[None, :]

    def tables(base):
        cos = np.zeros((seq, LANES), np.float32)
        cos[:, :base] = 1.0
        sa = np.zeros((seq, LANES), np.float32)
        sb = np.zeros((seq, LANES), np.float32)
        for k, ang in enumerate((ang_r, ang_c)):
            o = base + 2 * n_freq * k
            cos[:, o:o + n_freq] = np.cos(ang)
            cos[:, o + n_freq:o + 2 * n_freq] = np.cos(ang)
            sa[:, o:o + n_freq] = -np.sin(ang)
            sb[:, o + n_freq:o + 2 * n_freq] = np.sin(ang)
        return jnp.asarray(cos), jnp.asarray(sa), jnp.asarray(sb)

    cos_q, sin_qa, sin_qb = tables(MLA_NOPE)
    cos_k, sin_ka, sin_kb = tables(0)
    seg_q, seg_k = [], []
    for hb in range(0, MXU_TILE, HEAD_PAD):
        seg_q += [(hb, hb + MLA_NOPE), (hb + MLA_NOPE, hb + MLA_QK)]
        seg_k += [(hb, hb + MLA_NOPE)]
    place = np.zeros((LANES, MLA_PAD_WIDTH), np.float32)
    for hd in range(MLA_HEADS):
        for j in range(MLA_ROPE):
            place[j, hd * HEAD_PAD + MLA_NOPE + j] = 1.0
    return dict(cos_q=cos_q, sin_qa=sin_qa, sin_qb=sin_qb, cos_k=cos_k, sin_ka=sin_ka, sin_kb=sin_kb,
                m_q=_segment_matrix(MXU_TILE, seg_q), m_k=_segment_matrix(MXU_TILE, seg_k),
                m_kpe=_segment_matrix(LANES, [(0, MLA_ROPE)]),
                m_na=_segment_matrix(MXU_TILE, [(o, o + NA_DH) for o in range(0, MXU_TILE, NA_DH)]),
                place=jnp.asarray(place, jnp.bfloat16))


def _na_bias_table(rpb, rows):
    kr = min(NA_KR_MAX, rows)
    cols = np.arange(GRID_W)
    cs = np.clip(cols - NA_KC // 2, 0, GRID_W - NA_KC)
    col_mask = (cols[None, :] >= cs[:, None]) & (cols[None, :] < cs[:, None] + NA_KC)
    dc = np.clip(cols[None, :] - cols[:, None], -(NA_KC - 1), NA_KC - 1) + NA_KC - 1
    onehot = (dc[None] == np.arange(2 * NA_KC - 1)[:, None, None]).astype(np.float32)
    rpb_c = jnp.einsum("hdj,jqk->hdqk", rpb, jnp.asarray(onehot), precision=lax.Precision.HIGHEST)
    rpb_c = jnp.where(jnp.asarray(col_mask)[None, None], rpb_c * LOG2E, MASK_NEG)
    tabs = []
    for c in range(kr):
        lo = NA_KR_MAX - 1 - c
        b = rpb_c[:, lo:lo + kr].transpose(0, 2, 1, 3)
        tabs.append(b.reshape(NA_HEADS, GRID_W, kr * GRID_W))
    return jnp.stack(tabs)


def _pad_heads(w, width, pad):
    k = w.shape[0]
    w = w.reshape(k, -1, width)
    return jnp.pad(w, ((0, 0), (0, 0), (0, pad - width))).reshape(k, -1)


def _layer_weights(l, p, rows):
    f32 = jnp.float32
    w_in = p["w_in"][l]
    zeros = jnp.zeros((D_MODEL, LANES - MLA_ROPE), f32)
    w_in_p = jnp.concatenate([w_in[:, :Q_LORA + KV_LORA + MLA_ROPE], zeros, w_in[:, Q_LORA + KV_LORA + MLA_ROPE:]], axis=1)
    w_kv = p["w_kv_b"][l].reshape(KV_LORA, MLA_HEADS, MLA_NOPE + MLA_V)
    w_kb = jnp.pad(w_kv[:, :, :MLA_NOPE], ((0, 0), (0, 0), (0, HEAD_PAD - MLA_NOPE))).reshape(KV_LORA, MLA_PAD_WIDTH)
    w_v = w_kv[:, :, MLA_NOPE:].reshape(KV_LORA, MLA_WIDTH)
    scale = MLA_QK ** -0.5
    g_q = jnp.tile(jnp.pad(p["g_mla_q"][l] * (scale * LOG2E), (0, HEAD_PAD - MLA_QK)), MLA_HEADS)[None]
    g_k = jnp.tile(jnp.pad(p["g_mla_k"][l][:MLA_NOPE], (0, HEAD_PAD - MLA_NOPE)), MLA_HEADS)[None]
    g_kpe = jnp.pad(p["g_mla_k"][l][MLA_NOPE:], (0, LANES - MLA_ROPE))[None]
    w_r = jnp.concatenate([p["w_router_group"][l], p["w_router_expert"][l]], axis=1)
    w_r = jnp.pad(w_r, ((0, 0), (0, LANES - w_r.shape[1])))
    w_r_hi = w_r.astype(jnp.bfloat16)
    w_r_lo = (w_r - w_r_hi.astype(f32)).astype(jnp.bfloat16)
    b_r = jnp.pad(jnp.concatenate([p["b_router_group"][l], p["b_router_expert"][l]]), (0, LANES - N_GROUPS * (1 + EXPERTS_PER_GROUP)))[None]
    return dict(
        g_mix=p["g_mix_norm"][l][None], w_in=w_in_p.astype(jnp.bfloat16),
        g_q_a=p["g_q_a"][l][None], w_qb=_pad_heads(p["w_q_b"][l], MLA_QK, HEAD_PAD).astype(jnp.bfloat16),
        g_kv_a=p["g_kv_a"][l][None], w_kb=w_kb.astype(jnp.bfloat16), w_v=w_v.astype(jnp.bfloat16),
        g_q=g_q, g_k=g_k, g_kpe=g_kpe,
        g_naq=jnp.tile(p["g_na_q"][l] * (NA_DH ** -0.5 * LOG2E), NA_HEADS)[None],
        g_nak=jnp.tile(p["g_na_k"][l], NA_HEADS)[None],
        na_bias=_na_bias_table(p["na_rpb"][l], rows),
        g_mla_out=p["g_mla_out"][l][None], g_na_out=p["g_na_out"][l][None],
        w_out=p["w_out"][l].astype(jnp.bfloat16), g_ffn=p["g_ffn_norm"][l][None],
        w_r_hi=w_r_hi, w_r_lo=w_r_lo, b_r=b_r,
        w_g=p["w_gate"][l].astype(jnp.bfloat16), w_u=p["w_up"][l].astype(jnp.bfloat16),
        w_d=p["w_down"][l].astype(jnp.bfloat16))


def kernel(x, g_mix_norm, w_in, g_q_a, w_q_b, g_kv_a, w_kv_b, g_mla_q, g_mla_k, g_na_q, g_na_k, na_rpb, g_mla_out, g_na_out, w_out, g_ffn_norm, w_router_group, b_router_group, w_router_expert, b_router_expert, w_gate, w_up, w_down):
    batch, seq, d = x.shape
    assert d == D_MODEL and seq % TM_PROJ == 0 and seq % TQ_MLA == 0 and (batch * seq) % TM_MOE == 0
    rows = seq // GRID_W
    assert rows % NA_ROWS_PER_STEP == 0 and rows >= NA_KR_MAX
    p = dict(g_mix_norm=g_mix_norm, w_in=w_in, g_q_a=g_q_a, w_q_b=w_q_b, g_kv_a=g_kv_a, w_kv_b=w_kv_b,
             g_mla_q=g_mla_q, g_mla_k=g_mla_k, g_na_q=g_na_q, g_na_k=g_na_k, na_rpb=na_rpb,
             g_mla_out=g_mla_out, g_na_out=g_na_out, w_out=w_out, g_ffn_norm=g_ffn_norm,
             w_router_group=w_router_group, b_router_group=b_router_group,
             w_router_expert=w_router_expert, b_router_expert=b_router_expert,
             w_gate=w_gate, w_up=w_up, w_down=w_down)
    consts = _constants(seq)
    xf = x.reshape(batch * seq, d)
    for l in range(w_in.shape[0]):
        lw = _layer_weights(l, p, rows)
        q, k, v, naq, nak, nav = _proj_call(xf, lw, consts, seq)
        o_a = _mla_call(q, k, v, batch, seq)
        o_b = _na_call(naq, nak, nav, lw["na_bias"], batch, seq)
        x1, h2 = _mix_call(xf, o_a, o_b, lw)
        xf = _moe_call(x1, h2, lw)
    return xf.reshape(batch, seq, d)
```

```python
import functools
import math

import numpy as np
import jax
import jax.numpy as jnp
from jax import lax
from jax.experimental import pallas as pl
from jax.experimental.pallas import tpu as pltpu

D_MODEL = 1024
GRID_W = 64
MLA_HEADS = 8
MLA_NOPE = 64
MLA_ROPE = 32
MLA_V = 64
MLA_QK = MLA_NOPE + MLA_ROPE
Q_LORA = 256
KV_LORA = 128
MLA_WIDTH = MLA_HEADS * MLA_V
ROPE_THETA = 10000.0
NA_HEADS = 8
NA_DH = 64
NA_WIDTH = NA_HEADS * NA_DH
NA_KR_MAX = 8
NA_KC = 16
N_GROUPS = 4
EXPERTS_PER_GROUP = 4
D_EXPERT = 256
EPS = 1e-6

LANES = 128
MXU_TILE = 256
BF16_ROWS = 16
HEAD_PAD = LANES
MLA_PAD_WIDTH = MLA_HEADS * HEAD_PAD
PROJ_COLS = 2048
LOG2E = 1.4426950408889634
MASK_NEG = -1e30
SHIFT_LANE = MLA_QK
MAX_SHIFT_GAP = 100.0
BOUND_SLACK = 1.02
VMEM_LIMIT = 56 * 1024 * 1024

C_CQ = 0
C_CKV = C_CQ + Q_LORA
C_KPE = C_CKV + KV_LORA
C_NAQ = C_KPE + LANES
C_NAK = C_NAQ + NA_WIDTH
C_NAV = C_NAK + NA_WIDTH

TM_PROJ = 512
PROJ_SUBBLOCKS = 2
MIX_SUBBLOCKS = 2
TQ_MLA = 512
MLA_HEADS_PER_STEP = 4
NA_ROWS_PER_STEP = 8
TM_MOE = 1024
MOE_CHUNK = 128
MOE_ROWS = TM_MOE + N_GROUPS * BF16_ROWS + MOE_CHUNK + 64
ROUTE_GID = 0
ROUTE_W0 = 1
ROUTE_LO = 8
H2_COLS = D_MODEL + LANES


def _cparams(sem):
    return pltpu.CompilerParams(dimension_semantics=sem, vmem_limit_bytes=VMEM_LIMIT)


def _full(shape):
    nd = len(shape)
    return pl.BlockSpec(shape, lambda *_: (0,) * nd)


def _bf16(x):
    return x.astype(jnp.bfloat16)


def _dot(a, b):
    return jnp.dot(a, b, preferred_element_type=jnp.float32)


def _dot_nt(a, b):
    return lax.dot_general(a, b, (((1,), (1,)), ((), ())), preferred_element_type=jnp.float32)


def _rms_rows(x, gain):
    ms = jnp.mean(x * x, axis=-1, keepdims=True)
    return x * lax.rsqrt(ms + EPS) * gain


def _segment_rms_scale(x, seg_mat):
    n = x.shape[-1]
    parts = []
    for j in range(0, n, seg_mat.shape[0]):
        w = min(seg_mat.shape[0], n - j)
        xs = x[:, j:j + w]
        parts.append(_dot(_bf16(xs * xs), seg_mat[:w, :w]))
    ms = parts[0] if len(parts) == 1 else jnp.concatenate(parts, axis=-1)
    return lax.rsqrt(ms + EPS)


def _rope_block(x, cos, sin_a, sin_b):
    return x * cos + pltpu.roll(x, LANES - 8, 1) * sin_a + pltpu.roll(x, 8, 1) * sin_b


def _proj_rows(rows, x_ref, gmix_ref, win_ref, gqa_ref, wqb_ref, gkva_ref, wkb_ref, wv_ref,
               gq_ref, gk_ref, gkpe_ref, gnaq_ref, gnak_ref,
               mq_ref, mk_ref, mkpe_ref, mna_ref, place_ref, qshift_ref, kone_ref,
               cq_ref, sqa_ref, sqb_ref, ck_ref, ska_ref, skb_ref,
               q_out, k_out, v_out, naq_out, nak_out, nav_out):
    h = _bf16(_rms_rows(x_ref[rows, :], gmix_ref[...]))
    proj = _dot(h, win_ref[...])

    cqn = _bf16(_rms_rows(proj[:, C_CQ:C_CKV], gqa_ref[...]))
    q = _dot(cqn, wqb_ref[...])
    q = q * _segment_rms_scale(q, mq_ref[...]) * gq_ref[...]
    cosq, sqa, sqb = cq_ref[rows, :], sqa_ref[rows, :], sqb_ref[rows, :]
    for hd in range(MLA_HEADS):
        sl = slice(hd * HEAD_PAD, (hd + 1) * HEAD_PAD)
        q_out[rows, sl] = _bf16(_rope_block(q[:, sl], cosq, sqa, sqb) + qshift_ref[:, sl])

    ckvn = _bf16(_rms_rows(proj[:, C_CKV:C_KPE], gkva_ref[...]))
    kn = _dot(ckvn, wkb_ref[...])
    kn = kn * _segment_rms_scale(kn, mk_ref[...]) * gk_ref[...]
    v_out[rows, :] = _bf16(_dot(ckvn, wv_ref[...]))
    kp = proj[:, C_KPE:C_NAQ]
    kp = kp * lax.rsqrt(_dot(_bf16(kp * kp), mkpe_ref[...]) + EPS) * gkpe_ref[...]
    kp = _rope_block(kp, ck_ref[rows, :], ska_ref[rows, :], skb_ref[rows, :])
    k_out[rows, :] = _bf16(kn + _dot(_bf16(kp), place_ref[...]) + kone_ref[...])

    naq = proj[:, C_NAQ:C_NAK]
    naq_out[rows, :] = _bf16(naq * _segment_rms_scale(naq, mna_ref[...]) * gnaq_ref[...])
    nak = proj[:, C_NAK:C_NAV]
    nak_out[rows, :] = _bf16(nak * _segment_rms_scale(nak, mna_ref[...]) * gnak_ref[...])
    nav_out[rows, :] = _bf16(proj[:, C_NAV:PROJ_COLS])


def _proj_kernel(*refs):
    sub = TM_PROJ // PROJ_SUBBLOCKS
    for sb in range(PROJ_SUBBLOCKS):
        _proj_rows(slice(sb * sub, (sb + 1) * sub), *refs)


def _proj_call(x2d, lw, consts, seq):
    t = x2d.shape[0]
    tm = TM_PROJ
    n_seq_tiles = seq // tm
    row = lambda i: (i, 0)
    tab = lambda i: (i % n_seq_tiles, 0)
    weights = [lw["g_mix"], lw["w_in"], lw["g_q_a"], lw["w_qb"], lw["g_kv_a"], lw["w_kb"], lw["w_v"],
               lw["g_q"], lw["g_k"], lw["g_kpe"], lw["g_naq"], lw["g_nak"],
               consts["m_q"], consts["m_k"], consts["m_kpe"], consts["m_na"], consts["place"],
               lw["q_shift"], consts["k_one"]]
    tables = [consts["cos_q"], consts["sin_qa"], consts["sin_qb"], consts["cos_k"], consts["sin_ka"], consts["sin_kb"]]
    in_specs = ([pl.BlockSpec((tm, D_MODEL), row)] + [_full(w.shape) for w in weights]
                + [pl.BlockSpec((tm, LANES), tab) for _ in tables])
    out_shapes = [jax.ShapeDtypeStruct((t, MLA_PAD_WIDTH), jnp.bfloat16),
                  jax.ShapeDtypeStruct((t, MLA_PAD_WIDTH), jnp.bfloat16),
                  jax.ShapeDtypeStruct((t, MLA_WIDTH), jnp.bfloat16),
                  jax.ShapeDtypeStruct((t, NA_WIDTH), jnp.bfloat16),
                  jax.ShapeDtypeStruct((t, NA_WIDTH), jnp.bfloat16),
                  jax.ShapeDtypeStruct((t, NA_WIDTH), jnp.bfloat16)]
    out_specs = [pl.BlockSpec((tm, s.shape[1]), row) for s in out_shapes]
    return pl.pallas_call(
        _proj_kernel, grid=(t // tm,), in_specs=in_specs, out_specs=out_specs, out_shape=out_shapes,
        compiler_params=_cparams(("parallel",)), name="proj",
    )(x2d, *weights, *tables)


def _mla_kernel(q_ref, k_ref, v_ref, o_ref, *, use_max):
    scores = []
    for hh in range(MLA_HEADS_PER_STEP):
        sl = slice(hh * HEAD_PAD, (hh + 1) * HEAD_PAD)
        scores.append(_dot_nt(q_ref[:, sl], k_ref[:, sl]))
    outs = []
    for hh, s in enumerate(scores):
        p = jnp.exp2(s - jnp.max(s, axis=-1, keepdims=True)) if use_max else jnp.exp2(s)
        l = jnp.sum(p, axis=-1, keepdims=True)
        pair = slice((hh // 2) * LANES, (hh // 2 + 1) * LANES)
        outs.append(_dot(_bf16(p), v_ref[:, pair]) * (1.0 / l))
    lane = lax.broadcasted_iota(jnp.int32, outs[0].shape, 1)
    for pp in range(MLA_HEADS_PER_STEP // 2):
        o_ref[:, pp * LANES:(pp + 1) * LANES] = jnp.where(lane < MLA_V, outs[2 * pp], outs[2 * pp + 1])


def _mla_call(q, k, v, *, batch, seq, use_max):
    t = q.shape[0]
    tq = TQ_MLA
    nq = seq // tq
    hps = MLA_HEADS_PER_STEP
    return pl.pallas_call(
        functools.partial(_mla_kernel, use_max=use_max), grid=(batch, MLA_HEADS // hps, nq),
        in_specs=[pl.BlockSpec((tq, hps * HEAD_PAD), lambda b, p, i: (b * nq + i, p)),
                  pl.BlockSpec((seq, hps * HEAD_PAD), lambda b, p, i: (b, p)),
                  pl.BlockSpec((seq, hps * MLA_V), lambda b, p, i: (b, p))],
        out_specs=pl.BlockSpec((tq, hps * MLA_V), lambda b, p, i: (b * nq + i, p)),
        out_shape=jax.ShapeDtypeStruct((t, MLA_WIDTH), jnp.float32),
        compiler_params=_cparams(("parallel", "parallel", "arbitrary")), name="mla_attn",
    )(q, k, v)


def _na_kernel(q_ref, k_ref, v_ref, bias_ref, o_ref, *, rows, kr, use_max):
    step = pl.program_id(1)
    band = kr * GRID_W
    lane = lax.broadcasted_iota(jnp.int32, (GRID_W, LANES), 1)
    units = []
    for rr in range(NA_ROWS_PER_STEP):
        r = step * NA_ROWS_PER_STEP + rr
        rs = jnp.clip(r - kr // 2, 0, rows - kr)
        cls = r - rs
        start = pl.multiple_of(rs * GRID_W, GRID_W)
        qrow = q_ref[rr * GRID_W:(rr + 1) * GRID_W, :]
        for pr in range(NA_HEADS // 2):
            sl = slice(pr * LANES, (pr + 1) * LANES)
            q2 = qrow[:, sl]
            zero = jnp.zeros_like(q2)
            qq = jnp.concatenate([jnp.where(lane < NA_DH, q2, zero), jnp.where(lane >= NA_DH, q2, zero)], axis=0)
            s = _dot_nt(qq, k_ref[pl.ds(start, band), sl])
            units.append((rr, pr, start, cls, s))
    probs = []
    for rr, pr, start, cls, s in units:
        s = s + bias_ref[cls, 2 * pr:2 * pr + 2].reshape(2 * GRID_W, band)
        p = jnp.exp2(s - jnp.max(s, axis=-1, keepdims=True)) if use_max else jnp.exp2(s)
        l = jnp.sum(p, axis=-1, keepdims=True)
        probs.append((_bf16(p), 1.0 / l))
    for (rr, pr, start, cls, s), (p, rl) in zip(units, probs):
        sl = slice(pr * LANES, (pr + 1) * LANES)
        pv = _dot(p, v_ref[pl.ds(start, band), sl]) * rl
        o_ref[rr * GRID_W:(rr + 1) * GRID_W, sl] = jnp.where(lane < NA_DH, pv[:GRID_W], pv[GRID_W:])


def _na_call(q, k, v, bias, *, batch, seq, use_max):
    t = q.shape[0]
    rows = seq // GRID_W
    kr = min(NA_KR_MAX, rows)
    steps = rows // NA_ROWS_PER_STEP
    blk = NA_ROWS_PER_STEP * GRID_W
    return pl.pallas_call(
        functools.partial(_na_kernel, rows=rows, kr=kr, use_max=use_max), grid=(batch, steps),
        in_specs=[pl.BlockSpec((blk, NA_WIDTH), lambda b, i: (b * steps + i, 0)),
                  pl.BlockSpec((seq, NA_WIDTH), lambda b, i: (b, 0)),
                  pl.BlockSpec((seq, NA_WIDTH), lambda b, i: (b, 0)),
                  _full(bias.shape)],
        out_specs=pl.BlockSpec((blk, NA_WIDTH), lambda b, i: (b * steps + i, 0)),
        out_shape=jax.ShapeDtypeStruct((t, NA_WIDTH), jnp.float32),
        compiler_params=_cparams(("parallel", "arbitrary")), name="na_attn",
    )(q, k, v, bias)


def _mix_rows(rows, x_ref, oa_ref, ob_ref, ga_ref, gb_ref, wout_ref, gffn_ref, wr_ref, br_ref, x1_out, h2_out):
    mixed = jnp.concatenate([_rms_rows(oa_ref[rows, :], ga_ref[...]), _rms_rows(ob_ref[rows, :], gb_ref[...])], axis=-1)
    x1 = x_ref[rows, :] + _dot(_bf16(mixed), wout_ref[...])
    x1_out[rows, :] = x1
    hi = _bf16(_rms_rows(x1, gffn_ref[...]))
    h2_out[rows, :D_MODEL] = hi
    logits = _dot(hi, wr_ref[...]) + br_ref[...]

    lane = lax.broadcasted_iota(jnp.int32, logits.shape, 1).astype(jnp.float32)
    neg = jnp.float32(-jnp.inf)
    big = jnp.float32(LANES)
    is_grp = lane < N_GROUPS
    lg = jnp.where(is_grp, logits, neg)
    gmax = jnp.max(lg, axis=-1, keepdims=True)
    gid = jnp.min(jnp.where(lg == gmax, lane, big), axis=-1, keepdims=True)
    pg_top = 1.0 / jnp.sum(jnp.where(is_grp, jnp.exp(logits - gmax), 0.0), axis=-1, keepdims=True)
    base = N_GROUPS + EXPERTS_PER_GROUP * gid
    in_sel = (lane >= base) & (lane < base + EXPERTS_PER_GROUP)
    le = jnp.where(in_sel, logits, neg)
    m1 = jnp.max(le, axis=-1, keepdims=True)
    i1 = jnp.min(jnp.where(le == m1, lane, big), axis=-1, keepdims=True)
    le2 = jnp.where(lane == i1, neg, le)
    m2 = jnp.max(le2, axis=-1, keepdims=True)
    i2 = jnp.min(jnp.where(le2 == m2, lane, big), axis=-1, keepdims=True)
    e2 = jnp.exp(m2 - m1)
    w1 = pg_top / (1.0 + e2)
    w2 = pg_top * e2 / (1.0 + e2)
    j1 = i1 - base + ROUTE_W0
    j2 = i2 - base + ROUTE_W0
    w1h = _bf16(w1).astype(jnp.float32)
    w2h = _bf16(w2).astype(jnp.float32)
    rec = jnp.where(lane == ROUTE_GID, gid,
                    jnp.where(lane == j1, w1h, jnp.where(lane == j2, w2h,
                    jnp.where(lane == j1 + ROUTE_LO, w1 - w1h, jnp.where(lane == j2 + ROUTE_LO, w2 - w2h, 0.0)))))
    h2_out[rows, D_MODEL:] = _bf16(rec)


def _mix_kernel(*refs):
    sub = TM_PROJ // MIX_SUBBLOCKS
    for sb in range(MIX_SUBBLOCKS):
        _mix_rows(slice(sb * sub, (sb + 1) * sub), *refs)


def _mix_call(x2d, oa, ob, lw):
    t = x2d.shape[0]
    tm = TM_PROJ
    row = lambda i: (i, 0)
    weights = [lw["g_mla_out"], lw["g_na_out"], lw["w_out"], lw["g_ffn"], lw["w_r"], lw["b_r"]]
    return pl.pallas_call(
        _mix_kernel, grid=(t // tm,),
        in_specs=[pl.BlockSpec((tm, D_MODEL), row), pl.BlockSpec((tm, MLA_WIDTH), row),
                  pl.BlockSpec((tm, NA_WIDTH), row)] + [_full(w.shape) for w in weights],
        out_specs=[pl.BlockSpec((tm, D_MODEL), row), pl.BlockSpec((tm, H2_COLS), row)],
        out_shape=[jax.ShapeDtypeStruct((t, D_MODEL), jnp.float32),
                   jax.ShapeDtypeStruct((t, H2_COLS), jnp.bfloat16)],
        compiler_params=_cparams(("parallel",)), name="mix_router",
    )(x2d, oa, ob, *weights)


def _moe_kernel(x1_ref, h2_ref, wg_ref, wu_ref, wd_ref, o_ref,
                xs_ref, ys_ref, pt_ref, cnt_ref):
    g = pl.program_id(1)
    tm = TM_MOE

    @pl.when(g == 0)
    def _partition():
        route = h2_ref[:, D_MODEL:].astype(jnp.float32)
        lane = lax.broadcasted_iota(jnp.int32, route.shape, 1)
        gid = jnp.sum(jnp.where(lane == ROUTE_GID, route, 0.0), axis=-1, keepdims=True)
        onehot = jnp.where((lane.astype(jnp.float32) == gid) & (lane < N_GROUPS), 1.0, 0.0)
        r_i = lax.broadcasted_iota(jnp.int32, (tm, tm), 0)
        c_i = lax.broadcasted_iota(jnp.int32, (tm, tm), 1)
        tri = jnp.where(c_i < r_i, 1.0, 0.0).astype(jnp.bfloat16)
        rank = _dot(tri, _bf16(onehot))
        starts = []
        acc = jnp.int32(0)
        for gg in range(N_GROUPS):
            n = jnp.sum(jnp.where(lane == gg, onehot, 0.0)).astype(jnp.int32)
            cnt_ref[gg] = acc
            cnt_ref[N_GROUPS + gg] = lax.div(n + (MOE_CHUNK - 1), jnp.int32(MOE_CHUNK))
            starts.append(acc)
            acc = acc + lax.div(n + (BF16_ROWS - 1), jnp.int32(BF16_ROWS)) * BF16_ROWS
        start_vec = jnp.zeros_like(route)
        for gg in range(N_GROUPS):
            start_vec = jnp.where(lane == gg, starts[gg].astype(jnp.float32), start_vec)
        dest = jnp.sum(onehot * (start_vec + rank), axis=-1, keepdims=True)
        dest_i = dest.astype(jnp.int32)
        col = lax.broadcasted_iota(jnp.int32, (tm, MOE_ROWS), 1)
        pt_ref[...] = jnp.where(col == dest_i, 1.0, 0.0).astype(jnp.bfloat16)
        dest_row = jnp.transpose(jnp.broadcast_to(dest, (tm, LANES)))[0:1, :]
        rowi = lax.broadcasted_iota(jnp.int32, (MOE_ROWS, tm), 0)
        perm = jnp.where(rowi == dest_row.astype(jnp.int32), 1.0, 0.0).astype(jnp.bfloat16)
        xs_ref[...] = _bf16(_dot(perm, h2_ref[...]))
        ys_ref[...] = jnp.zeros_like(ys_ref)

    start_g = cnt_ref[g]
    n_chunks = cnt_ref[N_GROUPS + g]

    def _chunk(c, carry):
        off = pl.multiple_of(start_g + c * MOE_CHUNK, BF16_ROWS)
        xc = xs_ref[pl.ds(off, MOE_CHUNK), :D_MODEL]
        cw = xs_ref[pl.ds(off, MOE_CHUNK), D_MODEL:].astype(jnp.float32)
        hid = []
        for j in range(EXPERTS_PER_GROUP):
            a = _dot(xc, wg_ref[j])
            u = _dot(xc, wu_ref[j])
            c = cw[:, ROUTE_W0 + j:ROUTE_W0 + j + 1] + cw[:, ROUTE_W0 + ROUTE_LO + j:ROUTE_W0 + ROUTE_LO + j + 1]
            hid.append(a * (1.0 / (1.0 + jnp.exp(-a))) * u * c)
        hid = _bf16(jnp.concatenate(hid, axis=-1))
        w_down = wd_ref[...].reshape(EXPERTS_PER_GROUP * D_EXPERT, D_MODEL)
        ys_ref[pl.ds(off, MOE_CHUNK), :] = _bf16(_dot(hid, w_down))
        return carry

    lax.fori_loop(0, n_chunks, _chunk, 0)

    @pl.when(g == N_GROUPS - 1)
    def _unpermute():
        o_ref[...] = x1_ref[...] + _dot(pt_ref[...], ys_ref[...])


def _moe_call(x1, h2, lw):
    t = x1.shape[0]
    tm = TM_MOE
    row = lambda i, g: (i, 0)
    return pl.pallas_call(
        _moe_kernel, grid=(t // tm, N_GROUPS),
        in_specs=[pl.BlockSpec((tm, D_MODEL), row), pl.BlockSpec((tm, H2_COLS), row),
                  pl.BlockSpec((EXPERTS_PER_GROUP, D_MODEL, D_EXPERT), lambda i, g: (g, 0, 0)),
                  pl.BlockSpec((EXPERTS_PER_GROUP, D_MODEL, D_EXPERT), lambda i, g: (g, 0, 0)),
                  pl.BlockSpec((EXPERTS_PER_GROUP, D_EXPERT, D_MODEL), lambda i, g: (g, 0, 0))],
        out_specs=pl.BlockSpec((tm, D_MODEL), row),
        out_shape=jax.ShapeDtypeStruct((t, D_MODEL), jnp.float32),
        scratch_shapes=[pltpu.VMEM((MOE_ROWS, H2_COLS), jnp.bfloat16),
                        pltpu.VMEM((MOE_ROWS, D_MODEL), jnp.bfloat16),
                        pltpu.VMEM((tm, MOE_ROWS), jnp.bfloat16),
                        pltpu.SMEM((2 * N_GROUPS,), jnp.int32)],
        compiler_params=_cparams(("parallel", "arbitrary")), name="moe",
    )(x1, h2, lw["w_g"], lw["w_u"], lw["w_d"])


def _segment_matrix(width, segments):
    m = np.zeros((width, width), np.float32)
    for lo, hi in segments:
        m[lo:hi, lo:hi] = 1.0 / (hi - lo)
    return jnp.asarray(m, jnp.bfloat16)


def _constants(seq):
    t = np.arange(seq)
    row = (t // GRID_W).astype(np.float32)
    col = (t % GRID_W).astype(np.float32)
    n_freq = MLA_ROPE // 4
    inv = (np.float32(ROPE_THETA) ** (-np.arange(n_freq, dtype=np.float32) / n_freq)).astype(np.float32)
    ang_r = row[:, None] * inv[None, :]
    ang_c = col[:, None] * inv[None, :]

    def tables(base):
        cos = np.zeros((seq, LANES), np.float32)
        cos[:, :base] = 1.0
        sa = np.zeros((seq, LANES), np.float32)
        sb = np.zeros((seq, LANES), np.float32)
        for k, ang in enumerate((ang_r, ang_c)):
            o = base + 2 * n_freq * k
            cos[:, o:o + n_freq] = np.cos(ang)
            cos[:, o + n_freq:o + 2 * n_freq] = np.cos(ang)
            sa[:, o:o + n_freq] = -np.sin(ang)
            sb[:, o + n_freq:o + 2 * n_freq] = np.sin(ang)
        return jnp.asarray(cos), jnp.asarray(sa), jnp.asarray(sb)

    cos_q, sin_qa, sin_qb = tables(MLA_NOPE)
    cos_k, sin_ka, sin_kb = tables(0)
    seg_q, seg_k = [], []
    for hb in range(0, MXU_TILE, HEAD_PAD):
        seg_q += [(hb, hb + MLA_NOPE), (hb + MLA_NOPE, hb + MLA_QK)]
        seg_k += [(hb, hb + MLA_NOPE)]
    place = np.zeros((LANES, MLA_PAD_WIDTH), np.float32)
    for hd in range(MLA_HEADS):
        for j in range(MLA_ROPE):
            place[j, hd * HEAD_PAD + MLA_NOPE + j] = 1.0
    shift_lanes = np.zeros((1, MLA_PAD_WIDTH), np.float32)
    shift_lanes[0, SHIFT_LANE::HEAD_PAD] = 1.0
    return dict(cos_q=cos_q, sin_qa=sin_qa, sin_qb=sin_qb, cos_k=cos_k, sin_ka=sin_ka, sin_kb=sin_kb,
                m_q=_segment_matrix(MXU_TILE, seg_q), m_k=_segment_matrix(MXU_TILE, seg_k),
                m_kpe=_segment_matrix(LANES, [(0, MLA_ROPE)]),
                m_na=_segment_matrix(MXU_TILE, [(o, o + NA_DH) for o in range(0, MXU_TILE, NA_DH)]),
                place=jnp.asarray(place, jnp.bfloat16), k_one=jnp.asarray(shift_lanes))


def _na_bias_table(rpb, rows, shift):
    kr = min(NA_KR_MAX, rows)
    cols = np.arange(GRID_W)
    cs = np.clip(cols - NA_KC // 2, 0, GRID_W - NA_KC)
    col_mask = (cols[None, :] >= cs[:, None]) & (cols[None, :] < cs[:, None] + NA_KC)
    dc = np.clip(cols[None, :] - cols[:, None], -(NA_KC - 1), NA_KC - 1) + NA_KC - 1
    onehot = (dc[None] == np.arange(2 * NA_KC - 1)[:, None, None]).astype(np.float32)
    rpb_c = jnp.einsum("hdj,jqk->hdqk", rpb, jnp.asarray(onehot), precision=lax.Precision.HIGHEST)
    rpb_c = jnp.where(jnp.asarray(col_mask)[None, None], rpb_c * LOG2E - shift, MASK_NEG)
    tabs = []
    for c in range(kr):
        lo = NA_KR_MAX - 1 - c
        b = rpb_c[:, lo:lo + kr].transpose(0, 2, 1, 3)
        tabs.append(b.reshape(NA_HEADS, GRID_W, kr * GRID_W))
    return jnp.stack(tabs)


def _softmax_shifts(p, l):
    amax = lambda v: jnp.max(jnp.abs(v))
    gq, gk = p["g_mla_q"][l], p["g_mla_k"][l]
    qn = (MLA_QK ** -0.5 * LOG2E) * jnp.sqrt(MLA_NOPE * amax(gq[:MLA_NOPE]) ** 2 + MLA_ROPE * amax(gq[MLA_NOPE:]) ** 2)
    kn = jnp.sqrt(MLA_NOPE * amax(gk[:MLA_NOPE]) ** 2 + MLA_ROPE * amax(gk[MLA_NOPE:]) ** 2)
    b_mla = BOUND_SLACK * qn * kn
    mla_fast = 2.0 * b_mla <= MAX_SHIFT_GAP
    b_na = BOUND_SLACK * (NA_DH ** -0.5 * LOG2E) * NA_DH * amax(p["g_na_q"][l]) * amax(p["g_na_k"][l])
    bias_hi = jnp.max(p["na_rpb"][l]) * LOG2E
    bias_lo = jnp.min(p["na_rpb"][l]) * LOG2E
    na_fast = 2.0 * b_na + (bias_hi - bias_lo) <= MAX_SHIFT_GAP
    return (mla_fast, jnp.where(mla_fast, b_mla, 0.0)), (na_fast, jnp.where(na_fast, b_na + bias_hi, 0.0))


def _pad_heads(w, width, pad):
    k = w.shape[0]
    w = w.reshape(k, -1, width)
    return jnp.pad(w, ((0, 0), (0, 0), (0, pad - width))).reshape(k, -1)


def _layer_weights(l, p, rows, consts):
    f32 = jnp.float32
    (mla_fast, mla_shift), (na_fast, na_shift) = _softmax_shifts(p, l)
    w_in = p["w_in"][l]
    zeros = jnp.zeros((D_MODEL, LANES - MLA_ROPE), f32)
    w_in_p = jnp.concatenate([w_in[:, :Q_LORA + KV_LORA + MLA_ROPE], zeros, w_in[:, Q_LORA + KV_LORA + MLA_ROPE:]], axis=1)
    w_kv = p["w_kv_b"][l].reshape(KV_LORA, MLA_HEADS, MLA_NOPE + MLA_V)
    w_kb = jnp.pad(w_kv[:, :, :MLA_NOPE], ((0, 0), (0, 0), (0, HEAD_PAD - MLA_NOPE))).reshape(KV_LORA, MLA_PAD_WIDTH)
    w_v = w_kv[:, :, MLA_NOPE:].reshape(KV_LORA, MLA_WIDTH)
    scale = MLA_QK ** -0.5
    g_q = jnp.tile(jnp.pad(p["g_mla_q"][l] * (scale * LOG2E), (0, HEAD_PAD - MLA_QK)), MLA_HEADS)[None]
    g_k = jnp.tile(jnp.pad(p["g_mla_k"][l][:MLA_NOPE], (0, HEAD_PAD - MLA_NOPE)), MLA_HEADS)[None]
    g_kpe = jnp.pad(p["g_mla_k"][l][MLA_NOPE:], (0, LANES - MLA_ROPE))[None]
    w_r = jnp.concatenate([p["w_router_group"][l], p["w_router_expert"][l]], axis=1)
    w_r = jnp.pad(w_r, ((0, 0), (0, LANES - w_r.shape[1])))
    b_r = jnp.pad(jnp.concatenate([p["b_router_group"][l], p["b_router_expert"][l]]), (0, LANES - N_GROUPS * (1 + EXPERTS_PER_GROUP)))[None]
    return dict(
        g_mix=p["g_mix_norm"][l][None], w_in=w_in_p.astype(jnp.bfloat16),
        g_q_a=p["g_q_a"][l][None], w_qb=_pad_heads(p["w_q_b"][l], MLA_QK, HEAD_PAD).astype(jnp.bfloat16),
        g_kv_a=p["g_kv_a"][l][None], w_kb=w_kb.astype(jnp.bfloat16), w_v=w_v.astype(jnp.bfloat16),
        g_q=g_q, g_k=g_k, g_kpe=g_kpe,
        g_naq=jnp.tile(p["g_na_q"][l] * (NA_DH ** -0.5 * LOG2E), NA_HEADS)[None],
        g_nak=jnp.tile(p["g_na_k"][l], NA_HEADS)[None],
        na_bias=_na_bias_table(p["na_rpb"][l], rows, na_shift), na_fast=na_fast,
        q_shift=-mla_shift * consts["k_one"], mla_fast=mla_fast,
        g_mla_out=p["g_mla_out"][l][None], g_na_out=p["g_na_out"][l][None],
        w_out=p["w_out"][l].astype(jnp.bfloat16), g_ffn=p["g_ffn_norm"][l][None],
        w_r=w_r.astype(jnp.bfloat16), b_r=b_r,
        w_g=p["w_gate"][l].astype(jnp.bfloat16), w_u=p["w_up"][l].astype(jnp.bfloat16),
        w_d=p["w_down"][l].astype(jnp.bfloat16))


def kernel(x, g_mix_norm, w_in, g_q_a, w_q_b, g_kv_a, w_kv_b, g_mla_q, g_mla_k, g_na_q, g_na_k, na_rpb, g_mla_out, g_na_out, w_out, g_ffn_norm, w_router_group, b_router_group, w_router_expert, b_router_expert, w_gate, w_up, w_down):
    batch, seq, d = x.shape
    assert d == D_MODEL and seq % TM_PROJ == 0 and seq % TQ_MLA == 0 and (batch * seq) % TM_MOE == 0
    rows = seq // GRID_W
    assert rows % NA_ROWS_PER_STEP == 0 and rows >= NA_KR_MAX
    p = dict(g_mix_norm=g_mix_norm, w_in=w_in, g_q_a=g_q_a, w_q_b=w_q_b, g_kv_a=g_kv_a, w_kv_b=w_kv_b,
             g_mla_q=g_mla_q, g_mla_k=g_mla_k, g_na_q=g_na_q, g_na_k=g_na_k, na_rpb=na_rpb,
             g_mla_out=g_mla_out, g_na_out=g_na_out, w_out=w_out, g_ffn_norm=g_ffn_norm,
             w_router_group=w_router_group, b_router_group=b_router_group,
             w_router_expert=w_router_expert, b_router_expert=b_router_expert,
             w_gate=w_gate, w_up=w_up, w_down=w_down)
    consts = _constants(seq)
    xf = x.reshape(batch * seq, d)
    for l in range(w_in.shape[0]):
        lw = _layer_weights(l, p, rows, consts)
        q, k, v, naq, nak, nav = _proj_call(xf, lw, consts, seq)
        o_a = lax.cond(lw["mla_fast"],
                       functools.partial(_mla_call, batch=batch, seq=seq, use_max=False),
                       functools.partial(_mla_call, batch=batch, seq=seq, use_max=True), q, k, v)
        o_b = lax.cond(lw["na_fast"],
                       functools.partial(_na_call, batch=batch, seq=seq, use_max=False),
                       functools.partial(_na_call, batch=batch, seq=seq, use_max=True), naq, nak, nav, lw["na_bias"])
        x1, h2 = _mix_call(xf, o_a, o_b, lw)
        xf = _moe_call(x1, h2, lw)
    return xf.reshape(batch, seq, d)
```

```python
import functools
import math

import numpy as np
import jax
import jax.numpy as jnp
from jax import lax
from jax.experimental import pallas as pl
from jax.experimental.pallas import tpu as pltpu

D_MODEL = 1024
GRID_W = 64
MLA_HEADS = 8
MLA_NOPE = 64
MLA_ROPE = 32
MLA_V = 64
MLA_QK = MLA_NOPE + MLA_ROPE
Q_LORA = 256
KV_LORA = 128
MLA_WIDTH = MLA_HEADS * MLA_V
ROPE_THETA = 10000.0
NA_HEADS = 8
NA_DH = 64
NA_WIDTH = NA_HEADS * NA_DH
NA_KR_MAX = 8
NA_KC = 16
N_GROUPS = 4
EXPERTS_PER_GROUP = 4
D_EXPERT = 256
EPS = 1e-6

LANES = 128
MXU_TILE = 256
BF16_ROWS = 16
HEAD_PAD = LANES
MLA_PAD_WIDTH = MLA_HEADS * HEAD_PAD
PROJ_COLS = 2048
LOG2E = 1.4426950408889634
MASK_NEG = -1e30
SHIFT_LANE = MLA_QK
MAX_SHIFT_GAP = 100.0
BOUND_SLACK = 1.02
VMEM_LIMIT = 56 * 1024 * 1024

C_CQ = 0
C_CKV = C_CQ + Q_LORA
C_KPE = C_CKV + KV_LORA
C_NAQ = C_KPE + LANES
C_NAK = C_NAQ + NA_WIDTH
C_NAV = C_NAK + NA_WIDTH

TM_PROJ = 512
PROJ_SUBBLOCKS = 2
MIX_SUBBLOCKS = 2
TQ_MLA = 512
MLA_HEADS_PER_STEP = 4
NA_ROWS_PER_STEP = 8
TM_MOE = 1024
MOE_CHUNK = 160
MOE_ROWS = TM_MOE + MOE_CHUNK
RANK_BLOCK = 256
ROUTE_GID = 0
ROUTE_W0 = 1
ROUTE_LO = 8
H2_COLS = D_MODEL + LANES


def _cparams(sem):
    return pltpu.CompilerParams(dimension_semantics=sem, vmem_limit_bytes=VMEM_LIMIT)


def _full(shape):
    nd = len(shape)
    return pl.BlockSpec(shape, lambda *_: (0,) * nd)


def _bf16(x):
    return x.astype(jnp.bfloat16)


def _dot(a, b):
    return jnp.dot(a, b, preferred_element_type=jnp.float32)


def _dot_nt(a, b):
    return lax.dot_general(a, b, (((1,), (1,)), ((), ())), preferred_element_type=jnp.float32)


def _rms_rows(x, gain):
    ms = jnp.mean(x * x, axis=-1, keepdims=True)
    return x * lax.rsqrt(ms + EPS) * gain


def _segment_rms_scale(x, seg_mat):
    n = x.shape[-1]
    parts = []
    for j in range(0, n, seg_mat.shape[0]):
        w = min(seg_mat.shape[0], n - j)
        xs = x[:, j:j + w]
        parts.append(_dot(_bf16(xs * xs), seg_mat[:w, :w]))
    ms = parts[0] if len(parts) == 1 else jnp.concatenate(parts, axis=-1)
    return lax.rsqrt(ms + EPS)


def _rope_block(x, cos, sin_a, sin_b):
    return x * cos + pltpu.roll(x, LANES - 8, 1) * sin_a + pltpu.roll(x, 8, 1) * sin_b


def _proj_rows(rows, x_ref, gmix_ref, win_ref, gqa_ref, wqb_ref, gkva_ref, wkb_ref, wvt_ref,
               gq_ref, gk_ref, gkpe_ref, gnaq_ref, gnak_ref,
               mq_ref, mk_ref, mkpe_ref, mna_ref, place_ref, qshift_ref, kone_ref,
               cq_ref, sqa_ref, sqb_ref, ck_ref, ska_ref, skb_ref,
               q_out, k_out, vt_out, naq_out, nak_out, nav_out):
    h = _bf16(_rms_rows(x_ref[rows, :], gmix_ref[...]))
    proj = _dot(h, win_ref[...])

    cqn = _bf16(_rms_rows(proj[:, C_CQ:C_CKV], gqa_ref[...]))
    q = _dot(cqn, wqb_ref[...])
    q = q * _segment_rms_scale(q, mq_ref[...]) * gq_ref[...]
    cosq, sqa, sqb = cq_ref[rows, :], sqa_ref[rows, :], sqb_ref[rows, :]
    for hd in range(MLA_HEADS):
        sl = slice(hd * HEAD_PAD, (hd + 1) * HEAD_PAD)
        q_out[rows, sl] = _bf16(_rope_block(q[:, sl], cosq, sqa, sqb) + qshift_ref[:, sl])

    ckvn = _bf16(_rms_rows(proj[:, C_CKV:C_KPE], gkva_ref[...]))
    kn = _dot(ckvn, wkb_ref[...])
    kn = kn * _segment_rms_scale(kn, mk_ref[...]) * gk_ref[...]
    vt_out[:, rows] = _bf16(_dot_nt(wvt_ref[...], ckvn))
    kp = proj[:, C_KPE:C_NAQ]
    kp = kp * lax.rsqrt(_dot(_bf16(kp * kp), mkpe_ref[...]) + EPS) * gkpe_ref[...]
    kp = _rope_block(kp, ck_ref[rows, :], ska_ref[rows, :], skb_ref[rows, :])
    k_out[rows, :] = _bf16(kn + _dot(_bf16(kp), place_ref[...]) + kone_ref[...])

    naq = proj[:, C_NAQ:C_NAK]
    naq_out[rows, :] = _bf16(naq * _segment_rms_scale(naq, mna_ref[...]) * gnaq_ref[...])
    nak = proj[:, C_NAK:C_NAV]
    nak_out[rows, :] = _bf16(nak * _segment_rms_scale(nak, mna_ref[...]) * gnak_ref[...])
    nav_out[rows, :] = _bf16(proj[:, C_NAV:PROJ_COLS])


def _proj_kernel(*refs):
    sub = TM_PROJ // PROJ_SUBBLOCKS
    for sb in range(PROJ_SUBBLOCKS):
        _proj_rows(slice(sb * sub, (sb + 1) * sub), *refs)


def _proj_call(x2d, lw, consts, seq):
    t = x2d.shape[0]
    tm = TM_PROJ
    n_seq_tiles = seq // tm
    row = lambda i: (i, 0)
    tab = lambda i: (i % n_seq_tiles, 0)
    weights = [lw["g_mix"], lw["w_in"], lw["g_q_a"], lw["w_qb"], lw["g_kv_a"], lw["w_kb"], lw["w_vt"],
               lw["g_q"], lw["g_k"], lw["g_kpe"], lw["g_naq"], lw["g_nak"],
               consts["m_q"], consts["m_k"], consts["m_kpe"], consts["m_na"], consts["place"],
               lw["q_shift"], consts["k_one"]]
    tables = [consts["cos_q"], consts["sin_qa"], consts["sin_qb"], consts["cos_k"], consts["sin_ka"], consts["sin_kb"]]
    in_specs = ([pl.BlockSpec((tm, D_MODEL), row)] + [_full(w.shape) for w in weights]
                + [pl.BlockSpec((tm, LANES), tab) for _ in tables])
    out_shapes = [jax.ShapeDtypeStruct((t, MLA_PAD_WIDTH), jnp.bfloat16),
                  jax.ShapeDtypeStruct((t, MLA_PAD_WIDTH), jnp.bfloat16),
                  jax.ShapeDtypeStruct((MLA_WIDTH, t), jnp.bfloat16),
                  jax.ShapeDtypeStruct((t, NA_WIDTH), jnp.bfloat16),
                  jax.ShapeDtypeStruct((t, NA_WIDTH), jnp.bfloat16),
                  jax.ShapeDtypeStruct((t, NA_WIDTH), jnp.bfloat16)]
    out_specs = [pl.BlockSpec((MLA_WIDTH, tm), lambda i: (0, i)) if s.shape[0] == MLA_WIDTH
                 else pl.BlockSpec((tm, s.shape[1]), row) for s in out_shapes]
    return pl.pallas_call(
        _proj_kernel, grid=(t // tm,), in_specs=in_specs, out_specs=out_specs, out_shape=out_shapes,
        compiler_params=_cparams(("parallel",)), name="proj",
    )(x2d, *weights, *tables)


def _mla_kernel(q_ref, k_ref, vt_ref, o_ref, *, use_max):
    scores = []
    for hh in range(MLA_HEADS_PER_STEP):
        sl = slice(hh * HEAD_PAD, (hh + 1) * HEAD_PAD)
        scores.append(_dot_nt(k_ref[:, sl], q_ref[:, sl]))
    outs = []
    for hh, s in enumerate(scores):
        p = jnp.exp2(s - jnp.max(s, axis=0, keepdims=True)) if use_max else jnp.exp2(s)
        l = jnp.sum(p, axis=0, keepdims=True)
        pair = slice((hh // 2) * LANES, (hh // 2 + 1) * LANES)
        outs.append(_dot(vt_ref[pair, :], _bf16(p)) * (1.0 / l))
    row = lax.broadcasted_iota(jnp.int32, outs[0].shape, 0)
    for pp in range(MLA_HEADS_PER_STEP // 2):
        o_t = jnp.where(row < MLA_V, outs[2 * pp], outs[2 * pp + 1])
        o_ref[:, pp * LANES:(pp + 1) * LANES] = o_t.T


def _mla_call(q, k, vt, *, batch, seq, use_max):
    t = q.shape[0]
    tq = TQ_MLA
    nq = seq // tq
    hps = MLA_HEADS_PER_STEP
    return pl.pallas_call(
        functools.partial(_mla_kernel, use_max=use_max), grid=(batch, MLA_HEADS // hps, nq),
        in_specs=[pl.BlockSpec((tq, hps * HEAD_PAD), lambda b, p, i: (b * nq + i, p)),
                  pl.BlockSpec((seq, hps * HEAD_PAD), lambda b, p, i: (b, p)),
                  pl.BlockSpec((hps * MLA_V, seq), lambda b, p, i: (p, b))],
        out_specs=pl.BlockSpec((tq, hps * MLA_V), lambda b, p, i: (b * nq + i, p)),
        out_shape=jax.ShapeDtypeStruct((t, MLA_WIDTH), jnp.float32),
        compiler_params=_cparams(("parallel", "parallel", "arbitrary")), name="mla_attn",
    )(q, k, vt)


def _na_kernel(q_ref, k_ref, v_ref, bias_ref, o_ref, *, rows, kr, use_max):
    step = pl.program_id(1)
    band = kr * GRID_W
    lane = lax.broadcasted_iota(jnp.int32, (GRID_W, LANES), 1)
    units = []
    for rr in range(NA_ROWS_PER_STEP):
        r = step * NA_ROWS_PER_STEP + rr
        rs = jnp.clip(r - kr // 2, 0, rows - kr)
        cls = r - rs
        start = pl.multiple_of(rs * GRID_W, GRID_W)
        qrow = q_ref[rr * GRID_W:(rr + 1) * GRID_W, :]
        for pr in range(NA_HEADS // 2):
            sl = slice(pr * LANES, (pr + 1) * LANES)
            q2 = qrow[:, sl]
            zero = jnp.zeros_like(q2)
            qq = jnp.concatenate([jnp.where(lane < NA_DH, q2, zero), jnp.where(lane >= NA_DH, q2, zero)], axis=0)
            s = _dot_nt(qq, k_ref[pl.ds(start, band), sl])
            units.append((rr, pr, start, cls, s))
    probs = []
    for rr, pr, start, cls, s in units:
        s = s + bias_ref[cls, 2 * pr:2 * pr + 2].reshape(2 * GRID_W, band)
        p = jnp.exp2(s - jnp.max(s, axis=-1, keepdims=True)) if use_max else jnp.exp2(s)
        l = jnp.sum(p, axis=-1, keepdims=True)
        probs.append((_bf16(p), 1.0 / l))
    for (rr, pr, start, cls, s), (p, rl) in zip(units, probs):
        sl = slice(pr * LANES, (pr + 1) * LANES)
        pv = _dot(p, v_ref[pl.ds(start, band), sl]) * rl
        o_ref[rr * GRID_W:(rr + 1) * GRID_W, sl] = jnp.where(lane < NA_DH, pv[:GRID_W], pv[GRID_W:])


def _na_call(q, k, v, bias, *, batch, seq, use_max):
    t = q.shape[0]
    rows = seq // GRID_W
    kr = min(NA_KR_MAX, rows)
    steps = rows // NA_ROWS_PER_STEP
    blk = NA_ROWS_PER_STEP * GRID_W
    return pl.pallas_call(
        functools.partial(_na_kernel, rows=rows, kr=kr, use_max=use_max), grid=(batch, steps),
        in_specs=[pl.BlockSpec((blk, NA_WIDTH), lambda b, i: (b * steps + i, 0)),
                  pl.BlockSpec((seq, NA_WIDTH), lambda b, i: (b, 0)),
                  pl.BlockSpec((seq, NA_WIDTH), lambda b, i: (b, 0)),
                  _full(bias.shape)],
        out_specs=pl.BlockSpec((blk, NA_WIDTH), lambda b, i: (b * steps + i, 0)),
        out_shape=jax.ShapeDtypeStruct((t, NA_WIDTH), jnp.float32),
        compiler_params=_cparams(("parallel", "arbitrary")), name="na_attn",
    )(q, k, v, bias)


def _mix_rows(rows, x_ref, oa_ref, ob_ref, ga_ref, gb_ref, wout_ref, gffn_ref, wr_ref, br_ref, x1_out, h2_out):
    mixed = jnp.concatenate([_rms_rows(oa_ref[rows, :], ga_ref[...]), _rms_rows(ob_ref[rows, :], gb_ref[...])], axis=-1)
    x1 = x_ref[rows, :] + _dot(_bf16(mixed), wout_ref[...])
    x1_out[rows, :] = x1
    hi = _bf16(_rms_rows(x1, gffn_ref[...]))
    h2_out[rows, :D_MODEL] = hi
    logits = _dot(hi, wr_ref[...]) + br_ref[...]

    lane = lax.broadcasted_iota(jnp.int32, logits.shape, 1).astype(jnp.float32)
    neg = jnp.float32(-jnp.inf)
    big = jnp.float32(LANES)
    is_grp = lane < N_GROUPS
    lg = jnp.where(is_grp, logits, neg)
    gmax = jnp.max(lg, axis=-1, keepdims=True)
    gid = jnp.min(jnp.where(lg == gmax, lane, big), axis=-1, keepdims=True)
    pg_top = 1.0 / jnp.sum(jnp.where(is_grp, jnp.exp(logits - gmax), 0.0), axis=-1, keepdims=True)
    base = N_GROUPS + EXPERTS_PER_GROUP * gid
    in_sel = (lane >= base) & (lane < base + EXPERTS_PER_GROUP)
    le = jnp.where(in_sel, logits, neg)
    m1 = jnp.max(le, axis=-1, keepdims=True)
    i1 = jnp.min(jnp.where(le == m1, lane, big), axis=-1, keepdims=True)
    le2 = jnp.where(lane == i1, neg, le)
    m2 = jnp.max(le2, axis=-1, keepdims=True)
    i2 = jnp.min(jnp.where(le2 == m2, lane, big), axis=-1, keepdims=True)
    e2 = jnp.exp(m2 - m1)
    w1 = pg_top / (1.0 + e2)
    w2 = pg_top * e2 / (1.0 + e2)
    j1 = i1 - base + ROUTE_W0
    j2 = i2 - base + ROUTE_W0
    w1h = _bf16(w1).astype(jnp.float32)
    w2h = _bf16(w2).astype(jnp.float32)
    rec = jnp.where(lane == ROUTE_GID, gid,
                    jnp.where(lane == j1, w1h, jnp.where(lane == j2, w2h,
                    jnp.where(lane == j1 + ROUTE_LO, w1 - w1h, jnp.where(lane == j2 + ROUTE_LO, w2 - w2h, 0.0)))))
    h2_out[rows, D_MODEL:] = _bf16(rec)


def _mix_kernel(*refs):
    sub = TM_PROJ // MIX_SUBBLOCKS
    for sb in range(MIX_SUBBLOCKS):
        _mix_rows(slice(sb * sub, (sb + 1) * sub), *refs)


def _mix_call(x2d, oa, ob, lw):
    t = x2d.shape[0]
    tm = TM_PROJ
    row = lambda i: (i, 0)
    weights = [lw["g_mla_out"], lw["g_na_out"], lw["w_out"], lw["g_ffn"], lw["w_r"], lw["b_r"]]
    return pl.pallas_call(
        _mix_kernel, grid=(t // tm,),
        in_specs=[pl.BlockSpec((tm, D_MODEL), row), pl.BlockSpec((tm, MLA_WIDTH), row),
                  pl.BlockSpec((tm, NA_WIDTH), row)] + [_full(w.shape) for w in weights],
        out_specs=[pl.BlockSpec((tm, D_MODEL), row), pl.BlockSpec((tm, H2_COLS), row)],
        out_shape=[jax.ShapeDtypeStruct((t, D_MODEL), jnp.float32),
                   jax.ShapeDtypeStruct((t, H2_COLS), jnp.bfloat16)],
        compiler_params=_cparams(("parallel",)), name="mix_router",
    )(x2d, oa, ob, *weights)


def _moe_kernel(x1_ref, h2_ref, wg_ref, wu_ref, wd_ref, o_ref,
                xs_ref, ys_ref, pt_ref, cnt_ref):
    g = pl.program_id(1)
    tm = TM_MOE

    @pl.when(g == 0)
    def _partition():
        route = h2_ref[:, D_MODEL:].astype(jnp.float32)
        lane = lax.broadcasted_iota(jnp.int32, route.shape, 1)
        gid = jnp.sum(jnp.where(lane == ROUTE_GID, route, 0.0), axis=-1, keepdims=True)
        onehot = jnp.where((lane.astype(jnp.float32) == gid) & (lane < N_GROUPS), 1.0, 0.0)
        r_i = lax.broadcasted_iota(jnp.int32, (RANK_BLOCK, RANK_BLOCK), 0)
        c_i = lax.broadcasted_iota(jnp.int32, (RANK_BLOCK, RANK_BLOCK), 1)
        tri = jnp.where(c_i < r_i, 1.0, 0.0).astype(jnp.bfloat16)
        ranks = []
        sizes = jnp.zeros((1, LANES), jnp.float32)
        for blk in range(tm // RANK_BLOCK):
            oh = onehot[blk * RANK_BLOCK:(blk + 1) * RANK_BLOCK]
            ranks.append(_dot(tri, _bf16(oh)) + sizes)
            sizes = sizes + jnp.sum(oh, axis=0, keepdims=True)
        rank = jnp.concatenate(ranks, axis=0)
        lane1 = lane[0:1]
        start_vec = jnp.zeros((1, LANES), jnp.float32)
        acc = jnp.int32(0)
        for gg in range(N_GROUPS):
            n = jnp.sum(jnp.where(lane1 == gg, sizes, 0.0)).astype(jnp.int32)
            cnt_ref[gg] = acc
            cnt_ref[N_GROUPS + gg] = acc + n
            start_vec = jnp.where(lane1 == gg, acc.astype(jnp.float32), start_vec)
            acc = acc + n
        dest = jnp.sum(onehot * (start_vec + rank), axis=-1, keepdims=True)
        col = lax.broadcasted_iota(jnp.int32, (tm, tm), 1)
        pt_ref[...] = jnp.where(col == dest.astype(jnp.int32), 1.0, 0.0).astype(jnp.bfloat16)
        dest_row = jnp.transpose(jnp.broadcast_to(dest, (tm, LANES)))[0:1, :]
        rowi = lax.broadcasted_iota(jnp.int32, (tm, tm), 0)
        perm = jnp.where(rowi == dest_row.astype(jnp.int32), 1.0, 0.0).astype(jnp.bfloat16)
        xs_ref[:tm, :] = _bf16(_dot(perm, h2_ref[...]))
        xs_ref[tm:, :] = jnp.zeros((MOE_ROWS - tm, H2_COLS), jnp.bfloat16)
        ys_ref[...] = jnp.zeros_like(ys_ref)

    start_g = cnt_ref[g]
    end_g = cnt_ref[N_GROUPS + g]
    first = lax.shift_left(lax.shift_right_logical(start_g, 4), 4)
    n_chunks = lax.div(end_g - first + (MOE_CHUNK - 1), jnp.int32(MOE_CHUNK))

    def _chunk(c, carry):
        off = pl.multiple_of(first + c * MOE_CHUNK, BF16_ROWS)
        xc = xs_ref[pl.ds(off, MOE_CHUNK), :D_MODEL]
        cw = xs_ref[pl.ds(off, MOE_CHUNK), D_MODEL:].astype(jnp.float32)
        hid = []
        for j in range(EXPERTS_PER_GROUP):
            a = _dot(xc, wg_ref[j])
            u = _dot(xc, wu_ref[j])
            c = cw[:, ROUTE_W0 + j:ROUTE_W0 + j + 1] + cw[:, ROUTE_W0 + ROUTE_LO + j:ROUTE_W0 + ROUTE_LO + j + 1]
            hid.append(a * (1.0 / (1.0 + jnp.exp(-a))) * u * c)
        hid = _bf16(jnp.concatenate(hid, axis=-1))
        w_down = wd_ref[...].reshape(EXPERTS_PER_GROUP * D_EXPERT, D_MODEL)
        rows = off + lax.broadcasted_iota(jnp.int32, (MOE_CHUNK, 1), 0)
        mine = (rows >= start_g) & (rows < end_g)
        old = ys_ref[pl.ds(off, MOE_CHUNK), :].astype(jnp.float32)
        ys_ref[pl.ds(off, MOE_CHUNK), :] = _bf16(jnp.where(mine, _dot(hid, w_down), old))
        return carry

    lax.fori_loop(0, n_chunks, _chunk, 0)

    @pl.when(g == N_GROUPS - 1)
    def _unpermute():
        o_ref[...] = x1_ref[...] + _dot(pt_ref[...], ys_ref[:tm, :])


def _moe_call(x1, h2, lw):
    t = x1.shape[0]
    tm = TM_MOE
    row = lambda i, g: (i, 0)
    return pl.pallas_call(
        _moe_kernel, grid=(t // tm, N_GROUPS),
        in_specs=[pl.BlockSpec((tm, D_MODEL), row), pl.BlockSpec((tm, H2_COLS), row),
                  pl.BlockSpec((EXPERTS_PER_GROUP, D_MODEL, D_EXPERT), lambda i, g: (g, 0, 0)),
                  pl.BlockSpec((EXPERTS_PER_GROUP, D_MODEL, D_EXPERT), lambda i, g: (g, 0, 0)),
                  pl.BlockSpec((EXPERTS_PER_GROUP, D_EXPERT, D_MODEL), lambda i, g: (g, 0, 0))],
        out_specs=pl.BlockSpec((tm, D_MODEL), row),
        out_shape=jax.ShapeDtypeStruct((t, D_MODEL), jnp.float32),
        scratch_shapes=[pltpu.VMEM((MOE_ROWS, H2_COLS), jnp.bfloat16),
                        pltpu.VMEM((MOE_ROWS, D_MODEL), jnp.bfloat16),
                        pltpu.VMEM((tm, tm), jnp.bfloat16),
                        pltpu.SMEM((2 * N_GROUPS,), jnp.int32)],
        compiler_params=_cparams(("parallel", "arbitrary")), name="moe",
    )(x1, h2, lw["w_g"], lw["w_u"], lw["w_d"])


def _segment_matrix(width, segments):
    m = np.zeros((width, width), np.float32)
    for lo, hi in segments:
        m[lo:hi, lo:hi] = 1.0 / (hi - lo)
    return jnp.asarray(m, jnp.bfloat16)


def _constants(seq):
    t = np.arange(seq)
    row = (t // GRID_W).astype(np.float32)
    col = (t % GRID_W).astype(np.float32)
    n_freq = MLA_ROPE // 4
    inv = (np.float32(ROPE_THETA) ** (-np.arange(n_freq, dtype=np.float32) / n_freq)).astype(np.float32)
    ang_r = row[:, None] * inv[None, :]
    ang_c = col[:, None] * inv[None, :]

    def tables(base):
        cos = np.zeros((seq, LANES), np.float32)
        cos[:, :base] = 1.0
        sa = np.zeros((seq, LANES), np.float32)
        sb = np.zeros((seq, LANES), np.float32)
        for k, ang in enumerate((ang_r, ang_c)):
            o = base + 2 * n_freq * k
            cos[:, o:o + n_freq] = np.cos(ang)
            cos[:, o + n_freq:o + 2 * n_freq] = np.cos(ang)
            sa[:, o:o + n_freq] = -np.sin(ang)
            sb[:, o + n_freq:o + 2 * n_freq] = np.sin(ang)
        return jnp.asarray(cos), jnp.asarray(sa), jnp.asarray(sb)

    cos_q, sin_qa, sin_qb = tables(MLA_NOPE)
    cos_k, sin_ka, sin_kb = tables(0)
    seg_q, seg_k = [], []
    for hb in range(0, MXU_TILE, HEAD_PAD):
        seg_q += [(hb, hb + MLA_NOPE), (hb + MLA_NOPE, hb + MLA_QK)]
        seg_k += [(hb, hb + MLA_NOPE)]
    place = np.zeros((LANES, MLA_PAD_WIDTH), np.float32)
    for hd in range(MLA_HEADS):
        for j in range(MLA_ROPE):
            place[j, hd * HEAD_PAD + MLA_NOPE + j] = 1.0
    shift_lanes = np.zeros((1, MLA_PAD_WIDTH), np.float32)
    shift_lanes[0, SHIFT_LANE::HEAD_PAD] = 1.0
    return dict(cos_q=cos_q, sin_qa=sin_qa, sin_qb=sin_qb, cos_k=cos_k, sin_ka=sin_ka, sin_kb=sin_kb,
                m_q=_segment_matrix(MXU_TILE, seg_q), m_k=_segment_matrix(MXU_TILE, seg_k),
                m_kpe=_segment_matrix(LANES, [(0, MLA_ROPE)]),
                m_na=_segment_matrix(MXU_TILE, [(o, o + NA_DH) for o in range(0, MXU_TILE, NA_DH)]),
                place=jnp.asarray(place, jnp.bfloat16), k_one=jnp.asarray(shift_lanes))


def _na_bias_table(rpb, rows, shift):
    kr = min(NA_KR_MAX, rows)
    cols = np.arange(GRID_W)
    cs = np.clip(cols - NA_KC // 2, 0, GRID_W - NA_KC)
    col_mask = (cols[None, :] >= cs[:, None]) & (cols[None, :] < cs[:, None] + NA_KC)
    dc = np.clip(cols[None, :] - cols[:, None], -(NA_KC - 1), NA_KC - 1) + NA_KC - 1
    onehot = (dc[None] == np.arange(2 * NA_KC - 1)[:, None, None]).astype(np.float32)
    rpb_c = jnp.einsum("hdj,jqk->hdqk", rpb, jnp.asarray(onehot), precision=lax.Precision.HIGHEST)
    rpb_c = jnp.where(jnp.asarray(col_mask)[None, None], rpb_c * LOG2E - shift, MASK_NEG)
    tabs = []
    for c in range(kr):
        lo = NA_KR_MAX - 1 - c
        b = rpb_c[:, lo:lo + kr].transpose(0, 2, 1, 3)
        tabs.append(b.reshape(NA_HEADS, GRID_W, kr * GRID_W))
    return jnp.stack(tabs)


def _softmax_shifts(p, l):
    amax = lambda v: jnp.max(jnp.abs(v))
    gq, gk = p["g_mla_q"][l], p["g_mla_k"][l]
    qn = (MLA_QK ** -0.5 * LOG2E) * jnp.sqrt(MLA_NOPE * amax(gq[:MLA_NOPE]) ** 2 + MLA_ROPE * amax(gq[MLA_NOPE:]) ** 2)
    kn = jnp.sqrt(MLA_NOPE * amax(gk[:MLA_NOPE]) ** 2 + MLA_ROPE * amax(gk[MLA_NOPE:]) ** 2)
    b_mla = BOUND_SLACK * qn * kn
    mla_fast = 2.0 * b_mla <= MAX_SHIFT_GAP
    b_na = BOUND_SLACK * (NA_DH ** -0.5 * LOG2E) * NA_DH * amax(p["g_na_q"][l]) * amax(p["g_na_k"][l])
    bias_hi = jnp.max(p["na_rpb"][l]) * LOG2E
    bias_lo = jnp.min(p["na_rpb"][l]) * LOG2E
    na_fast = 2.0 * b_na + (bias_hi - bias_lo) <= MAX_SHIFT_GAP
    return (mla_fast, jnp.where(mla_fast, b_mla, 0.0)), (na_fast, jnp.where(na_fast, b_na + bias_hi, 0.0))


def _pad_heads(w, width, pad):
    k = w.shape[0]
    w = w.reshape(k, -1, width)
    return jnp.pad(w, ((0, 0), (0, 0), (0, pad - width))).reshape(k, -1)


def _layer_weights(l, p, rows, consts):
    f32 = jnp.float32
    (mla_fast, mla_shift), (na_fast, na_shift) = _softmax_shifts(p, l)
    w_in = p["w_in"][l]
    zeros = jnp.zeros((D_MODEL, LANES - MLA_ROPE), f32)
    w_in_p = jnp.concatenate([w_in[:, :Q_LORA + KV_LORA + MLA_ROPE], zeros, w_in[:, Q_LORA + KV_LORA + MLA_ROPE:]], axis=1)
    w_kv = p["w_kv_b"][l].reshape(KV_LORA, MLA_HEADS, MLA_NOPE + MLA_V)
    w_kb = jnp.pad(w_kv[:, :, :MLA_NOPE], ((0, 0), (0, 0), (0, HEAD_PAD - MLA_NOPE))).reshape(KV_LORA, MLA_PAD_WIDTH)
    w_v = w_kv[:, :, MLA_NOPE:].reshape(KV_LORA, MLA_WIDTH)
    scale = MLA_QK ** -0.5
    g_q = jnp.tile(jnp.pad(p["g_mla_q"][l] * (scale * LOG2E), (0, HEAD_PAD - MLA_QK)), MLA_HEADS)[None]
    g_k = jnp.tile(jnp.pad(p["g_mla_k"][l][:MLA_NOPE], (0, HEAD_PAD - MLA_NOPE)), MLA_HEADS)[None]
    g_kpe = jnp.pad(p["g_mla_k"][l][MLA_NOPE:], (0, LANES - MLA_ROPE))[None]
    w_r = jnp.concatenate([p["w_router_group"][l], p["w_router_expert"][l]], axis=1)
    w_r = jnp.pad(w_r, ((0, 0), (0, LANES - w_r.shape[1])))
    b_r = jnp.pad(jnp.concatenate([p["b_router_group"][l], p["b_router_expert"][l]]), (0, LANES - N_GROUPS * (1 + EXPERTS_PER_GROUP)))[None]
    return dict(
        g_mix=p["g_mix_norm"][l][None], w_in=w_in_p.astype(jnp.bfloat16),
        g_q_a=p["g_q_a"][l][None], w_qb=_pad_heads(p["w_q_b"][l], MLA_QK, HEAD_PAD).astype(jnp.bfloat16),
        g_kv_a=p["g_kv_a"][l][None], w_kb=w_kb.astype(jnp.bfloat16), w_vt=w_v.T.astype(jnp.bfloat16),
        g_q=g_q, g_k=g_k, g_kpe=g_kpe,
        g_naq=jnp.tile(p["g_na_q"][l] * (NA_DH ** -0.5 * LOG2E), NA_HEADS)[None],
        g_nak=jnp.tile(p["g_na_k"][l], NA_HEADS)[None],
        na_bias=_na_bias_table(p["na_rpb"][l], rows, na_shift), na_fast=na_fast,
        q_shift=-mla_shift * consts["k_one"], mla_fast=mla_fast,
        g_mla_out=p["g_mla_out"][l][None], g_na_out=p["g_na_out"][l][None],
        w_out=p["w_out"][l].astype(jnp.bfloat16), g_ffn=p["g_ffn_norm"][l][None],
        w_r=w_r.astype(jnp.bfloat16), b_r=b_r,
        w_g=p["w_gate"][l].astype(jnp.bfloat16), w_u=p["w_up"][l].astype(jnp.bfloat16),
        w_d=p["w_down"][l].astype(jnp.bfloat16))


def kernel(x, g_mix_norm, w_in, g_q_a, w_q_b, g_kv_a, w_kv_b, g_mla_q, g_mla_k, g_na_q, g_na_k, na_rpb, g_mla_out, g_na_out, w_out, g_ffn_norm, w_router_group, b_router_group, w_router_expert, b_router_expert, w_gate, w_up, w_down):
    batch, seq, d = x.shape
    assert d == D_MODEL and seq % TM_PROJ == 0 and seq % TQ_MLA == 0 and (batch * seq) % TM_MOE == 0
    rows = seq // GRID_W
    assert rows % NA_ROWS_PER_STEP == 0 and rows >= NA_KR_MAX
    p = dict(g_mix_norm=g_mix_norm, w_in=w_in, g_q_a=g_q_a, w_q_b=w_q_b, g_kv_a=g_kv_a, w_kv_b=w_kv_b,
             g_mla_q=g_mla_q, g_mla_k=g_mla_k, g_na_q=g_na_q, g_na_k=g_na_k, na_rpb=na_rpb,
             g_mla_out=g_mla_out, g_na_out=g_na_out, w_out=w_out, g_ffn_norm=g_ffn_norm,
             w_router_group=w_router_group, b_router_group=b_router_group,
             w_router_expert=w_router_expert, b_router_expert=b_router_expert,
             w_gate=w_gate, w_up=w_up, w_down=w_down)
    consts = _constants(seq)
    xf = x.reshape(batch * seq, d)
    for l in range(w_in.shape[0]):
        lw = _layer_weights(l, p, rows, consts)
        q, k, v, naq, nak, nav = _proj_call(xf, lw, consts, seq)
        o_a = lax.cond(lw["mla_fast"],
                       functools.partial(_mla_call, batch=batch, seq=seq, use_max=False),
                       functools.partial(_mla_call, batch=batch, seq=seq, use_max=True), q, k, v)
        o_b = lax.cond(lw["na_fast"],
                       functools.partial(_na_call, batch=batch, seq=seq, use_max=False),
                       functools.partial(_na_call, batch=batch, seq=seq, use_max=True), naq, nak, nav, lw["na_bias"])
        x1, h2 = _mix_call(xf, o_a, o_b, lw)
        xf = _moe_call(x1, h2, lw)
    return xf.reshape(batch, seq, d)
```

```python
import functools
import math

import numpy as np
import jax
import jax.numpy as jnp
from jax import lax
from jax.experimental import pallas as pl
from jax.experimental.pallas import tpu as pltpu

D_MODEL = 1024
GRID_W = 64
MLA_HEADS = 8
MLA_NOPE = 64
MLA_ROPE = 32
MLA_V = 64
MLA_QK = MLA_NOPE + MLA_ROPE
Q_LORA = 256
KV_LORA = 128
MLA_WIDTH = MLA_HEADS * MLA_V
ROPE_THETA = 10000.0
NA_HEADS = 8
NA_DH = 64
NA_WIDTH = NA_HEADS * NA_DH
NA_KR_MAX = 8
NA_KC = 16
N_GROUPS = 4
EXPERTS_PER_GROUP = 4
D_EXPERT = 256
EPS = 1e-6

LANES = 128
MXU_TILE = 256
BF16_ROWS = 16
HEAD_PAD = LANES
MLA_PAD_WIDTH = MLA_HEADS * HEAD_PAD
PROJ_COLS = 2048
LOG2E = 1.4426950408889634
MASK_NEG = -1e30
SHIFT_LANE = MLA_QK
MAX_SHIFT_GAP = 100.0
BOUND_SLACK = 1.02
VMEM_LIMIT = 56 * 1024 * 1024

C_CQ = 0
C_CKV = C_CQ + Q_LORA
C_KPE = C_CKV + KV_LORA
C_NAQ = C_KPE + LANES
C_NAK = C_NAQ + NA_WIDTH
C_NAV = C_NAK + NA_WIDTH

TM_PROJ = 512
PROJ_SUBBLOCKS = 2
MIX_SUBBLOCKS = 2
TQ_MLA = 512
MLA_HEADS_PER_STEP = 4
NA_ROWS_PER_STEP = 8
TM_MOE = 1024
MOE_CHUNK = 160
MOE_ROWS = TM_MOE + MOE_CHUNK
RANK_BLOCK = 256
ROUTE_GID = 0
ROUTE_W0 = 1
ROUTE_LO = 8
H2_COLS = D_MODEL + LANES


def _cparams(sem):
    return pltpu.CompilerParams(dimension_semantics=sem, vmem_limit_bytes=VMEM_LIMIT)


def _full(shape):
    nd = len(shape)
    return pl.BlockSpec(shape, lambda *_: (0,) * nd)


def _bf16(x):
    return x.astype(jnp.bfloat16)


def _dot(a, b):
    return jnp.dot(a, b, preferred_element_type=jnp.float32)


def _dot_nt(a, b):
    return lax.dot_general(a, b, (((1,), (1,)), ((), ())), preferred_element_type=jnp.float32)


def _rms_rows(x, gain):
    ms = jnp.mean(x * x, axis=-1, keepdims=True)
    return x * lax.rsqrt(ms + EPS) * gain


def _segment_rms_scale(x, seg_mat):
    n = x.shape[-1]
    parts = []
    for j in range(0, n, seg_mat.shape[0]):
        w = min(seg_mat.shape[0], n - j)
        xs = x[:, j:j + w]
        parts.append(_dot(_bf16(xs * xs), seg_mat[:w, :w]))
    ms = parts[0] if len(parts) == 1 else jnp.concatenate(parts, axis=-1)
    return lax.rsqrt(ms + EPS)


def _rope_block(x, cos, sin_a, sin_b):
    return x * cos + pltpu.roll(x, LANES - 8, 1) * sin_a + pltpu.roll(x, 8, 1) * sin_b


def _proj_rows(rows, x_ref, gmix_ref, win_ref, gqa_ref, wqb_ref, gkva_ref, wkc_ref, wvt_ref,
               gq_ref, gkc_ref, gkpe_ref, gnaq_ref, gnak_ref,
               mq_ref, mkpe_ref, mna_ref, qshift_ref, kone_ref,
               cq_ref, sqa_ref, sqb_ref, ck_ref, ska_ref, skb_ref,
               q_out, k_out, vt_out, naq_out, nak_out, nav_out):
    h = _bf16(_rms_rows(x_ref[rows, :], gmix_ref[...]))
    proj = _dot(h, win_ref[...])

    cqn = _bf16(_rms_rows(proj[:, C_CQ:C_CKV], gqa_ref[...]))
    q = _dot(cqn, wqb_ref[...])
    q = q * _segment_rms_scale(q, mq_ref[...]) * gq_ref[...]
    cosq, sqa, sqb = cq_ref[rows, :], sqa_ref[rows, :], sqb_ref[rows, :]
    for hd in range(MLA_HEADS):
        sl = slice(hd * HEAD_PAD, (hd + 1) * HEAD_PAD)
        q_out[rows, sl] = _bf16(_rope_block(q[:, sl], cosq, sqa, sqb) + qshift_ref[:, sl])

    ckvn = _bf16(_rms_rows(proj[:, C_CKV:C_KPE], gkva_ref[...]))
    kc = _dot(ckvn, wkc_ref[...])
    kc = kc * _segment_rms_scale(kc, mna_ref[...]) * gkc_ref[...]
    vt_out[:, rows] = _bf16(_dot_nt(wvt_ref[...], ckvn))
    kp = proj[:, C_KPE:C_NAQ]
    kp = kp * lax.rsqrt(_dot(_bf16(kp * kp), mkpe_ref[...]) + EPS) * gkpe_ref[...]
    kp = _rope_block(kp, ck_ref[rows, :], ska_ref[rows, :], skb_ref[rows, :])
    tail = pltpu.roll(kp, MLA_NOPE, 1) + kone_ref[...]
    lane = lax.broadcasted_iota(jnp.int32, tail.shape, 1)
    for pr in range(MLA_HEADS // 2):
        two = kc[:, pr * LANES:(pr + 1) * LANES]
        k_out[rows, (2 * pr) * HEAD_PAD:(2 * pr + 1) * HEAD_PAD] = _bf16(jnp.where(lane < MLA_NOPE, two, tail))
        k_out[rows, (2 * pr + 1) * HEAD_PAD:(2 * pr + 2) * HEAD_PAD] = _bf16(
            jnp.where(lane < MLA_NOPE, pltpu.roll(two, MLA_NOPE, 1), tail))

    naq = proj[:, C_NAQ:C_NAK]
    naq_out[rows, :] = _bf16(naq * _segment_rms_scale(naq, mna_ref[...]) * gnaq_ref[...])
    nak = proj[:, C_NAK:C_NAV]
    nak_out[rows, :] = _bf16(nak * _segment_rms_scale(nak, mna_ref[...]) * gnak_ref[...])
    nav_out[rows, :] = _bf16(proj[:, C_NAV:PROJ_COLS])


def _proj_kernel(*refs):
    sub = TM_PROJ // PROJ_SUBBLOCKS
    for sb in range(PROJ_SUBBLOCKS):
        _proj_rows(slice(sb * sub, (sb + 1) * sub), *refs)


def _proj_call(x2d, lw, consts, seq):
    t = x2d.shape[0]
    tm = TM_PROJ
    n_seq_tiles = seq // tm
    row = lambda i: (i, 0)
    tab = lambda i: (i % n_seq_tiles, 0)
    weights = [lw["g_mix"], lw["w_in"], lw["g_q_a"], lw["w_qb"], lw["g_kv_a"], lw["w_kc"], lw["w_vt"],
               lw["g_q"], lw["g_kc"], lw["g_kpe"], lw["g_naq"], lw["g_nak"],
               consts["m_q"], consts["m_kpe"], consts["m_na"], lw["q_shift"], consts["k_one"]]
    tables = [consts["cos_q"], consts["sin_qa"], consts["sin_qb"], consts["cos_k"], consts["sin_ka"], consts["sin_kb"]]
    in_specs = ([pl.BlockSpec((tm, D_MODEL), row)] + [_full(w.shape) for w in weights]
                + [pl.BlockSpec((tm, LANES), tab) for _ in tables])
    out_shapes = [jax.ShapeDtypeStruct((t, MLA_PAD_WIDTH), jnp.bfloat16),
                  jax.ShapeDtypeStruct((t, MLA_PAD_WIDTH), jnp.bfloat16),
                  jax.ShapeDtypeStruct((MLA_WIDTH, t), jnp.bfloat16),
                  jax.ShapeDtypeStruct((t, NA_WIDTH), jnp.bfloat16),
                  jax.ShapeDtypeStruct((t, NA_WIDTH), jnp.bfloat16),
                  jax.ShapeDtypeStruct((t, NA_WIDTH), jnp.bfloat16)]
    out_specs = [pl.BlockSpec((MLA_WIDTH, tm), lambda i: (0, i)) if s.shape[0] == MLA_WIDTH
                 else pl.BlockSpec((tm, s.shape[1]), row) for s in out_shapes]
    return pl.pallas_call(
        _proj_kernel, grid=(t // tm,), in_specs=in_specs, out_specs=out_specs, out_shape=out_shapes,
        compiler_params=_cparams(("parallel",)), name="proj",
    )(x2d, *weights, *tables)


def _mla_kernel(q_ref, k_ref, vt_ref, o_ref, *, use_max):
    scores = []
    for hh in range(MLA_HEADS_PER_STEP):
        sl = slice(hh * HEAD_PAD, (hh + 1) * HEAD_PAD)
        scores.append(_dot_nt(k_ref[:, sl], q_ref[:, sl]))
    outs = []
    for hh, s in enumerate(scores):
        p = jnp.exp2(s - jnp.max(s, axis=0, keepdims=True)) if use_max else jnp.exp2(s)
        l = jnp.sum(p, axis=0, keepdims=True)
        pair = slice((hh // 2) * LANES, (hh // 2 + 1) * LANES)
        outs.append(_dot(vt_ref[pair, :], _bf16(p)) * (1.0 / l))
    row = lax.broadcasted_iota(jnp.int32, outs[0].shape, 0)
    for pp in range(MLA_HEADS_PER_STEP // 2):
        o_t = jnp.where(row < MLA_V, outs[2 * pp], outs[2 * pp + 1])
        o_ref[:, pp * LANES:(pp + 1) * LANES] = _bf16(o_t.T)


def _mla_call(q, k, vt, *, batch, seq, use_max):
    t = q.shape[0]
    tq = TQ_MLA
    nq = seq // tq
    hps = MLA_HEADS_PER_STEP
    return pl.pallas_call(
        functools.partial(_mla_kernel, use_max=use_max), grid=(batch, MLA_HEADS // hps, nq),
        in_specs=[pl.BlockSpec((tq, hps * HEAD_PAD), lambda b, p, i: (b * nq + i, p)),
                  pl.BlockSpec((seq, hps * HEAD_PAD), lambda b, p, i: (b, p)),
                  pl.BlockSpec((hps * MLA_V, seq), lambda b, p, i: (p, b))],
        out_specs=pl.BlockSpec((tq, hps * MLA_V), lambda b, p, i: (b * nq + i, p)),
        out_shape=jax.ShapeDtypeStruct((t, MLA_WIDTH), jnp.bfloat16),
        compiler_params=_cparams(("parallel", "parallel", "arbitrary")), name="mla_attn",
    )(q, k, vt)


def _na_kernel(q_ref, k_ref, v_ref, bias_ref, o_ref, *, rows, kr, use_max):
    step = pl.program_id(1)
    band = kr * GRID_W
    lane = lax.broadcasted_iota(jnp.int32, (GRID_W, LANES), 1)
    units = []
    for rr in range(NA_ROWS_PER_STEP):
        r = step * NA_ROWS_PER_STEP + rr
        rs = jnp.clip(r - kr // 2, 0, rows - kr)
        cls = r - rs
        start = pl.multiple_of(rs * GRID_W, GRID_W)
        qrow = q_ref[rr * GRID_W:(rr + 1) * GRID_W, :]
        for pr in range(NA_HEADS // 2):
            sl = slice(pr * LANES, (pr + 1) * LANES)
            q2 = qrow[:, sl]
            zero = jnp.zeros_like(q2)
            qq = jnp.concatenate([jnp.where(lane < NA_DH, q2, zero), jnp.where(lane >= NA_DH, q2, zero)], axis=0)
            s = _dot_nt(qq, k_ref[pl.ds(start, band), sl])
            units.append((rr, pr, start, cls, s))
    probs = []
    for rr, pr, start, cls, s in units:
        s = s + bias_ref[cls, 2 * pr:2 * pr + 2].reshape(2 * GRID_W, band)
        p = jnp.exp2(s - jnp.max(s, axis=-1, keepdims=True)) if use_max else jnp.exp2(s)
        l = jnp.sum(p, axis=-1, keepdims=True)
        probs.append((_bf16(p), 1.0 / l))
    for (rr, pr, start, cls, s), (p, rl) in zip(units, probs):
        sl = slice(pr * LANES, (pr + 1) * LANES)
        pv = _dot(p, v_ref[pl.ds(start, band), sl]) * rl
        o_ref[rr * GRID_W:(rr + 1) * GRID_W, sl] = _bf16(jnp.where(lane < NA_DH, pv[:GRID_W], pv[GRID_W:]))


def _na_call(q, k, v, bias, *, batch, seq, use_max):
    t = q.shape[0]
    rows = seq // GRID_W
    kr = min(NA_KR_MAX, rows)
    steps = rows // NA_ROWS_PER_STEP
    blk = NA_ROWS_PER_STEP * GRID_W
    return pl.pallas_call(
        functools.partial(_na_kernel, rows=rows, kr=kr, use_max=use_max), grid=(batch, steps),
        in_specs=[pl.BlockSpec((blk, NA_WIDTH), lambda b, i: (b * steps + i, 0)),
                  pl.BlockSpec((seq, NA_WIDTH), lambda b, i: (b, 0)),
                  pl.BlockSpec((seq, NA_WIDTH), lambda b, i: (b, 0)),
                  _full(bias.shape)],
        out_specs=pl.BlockSpec((blk, NA_WIDTH), lambda b, i: (b * steps + i, 0)),
        out_shape=jax.ShapeDtypeStruct((t, NA_WIDTH), jnp.bfloat16),
        compiler_params=_cparams(("parallel", "arbitrary")), name="na_attn",
    )(q, k, v, bias)


def _mix_rows(rows, x_ref, oa_ref, ob_ref, ga_ref, gb_ref, wout_ref, gffn_ref, wr_ref, br_ref, x1_out, h2_out):
    mixed = jnp.concatenate([_rms_rows(oa_ref[rows, :].astype(jnp.float32), ga_ref[...]),
                             _rms_rows(ob_ref[rows, :].astype(jnp.float32), gb_ref[...])], axis=-1)
    x1 = x_ref[rows, :] + _dot(_bf16(mixed), wout_ref[...])
    x1_out[rows, :] = x1
    hi = _bf16(_rms_rows(x1, gffn_ref[...]))
    h2_out[rows, :D_MODEL] = hi
    logits = _dot(hi, wr_ref[...]) + br_ref[...]

    lane = lax.broadcasted_iota(jnp.int32, logits.shape, 1).astype(jnp.float32)
    neg = jnp.float32(-jnp.inf)
    big = jnp.float32(LANES)
    is_grp = lane < N_GROUPS
    lg = jnp.where(is_grp, logits, neg)
    gmax = jnp.max(lg, axis=-1, keepdims=True)
    gid = jnp.min(jnp.where(lg == gmax, lane, big), axis=-1, keepdims=True)
    pg_top = 1.0 / jnp.sum(jnp.where(is_grp, jnp.exp(logits - gmax), 0.0), axis=-1, keepdims=True)
    base = N_GROUPS + EXPERTS_PER_GROUP * gid
    in_sel = (lane >= base) & (lane < base + EXPERTS_PER_GROUP)
    le = jnp.where(in_sel, logits, neg)
    m1 = jnp.max(le, axis=-1, keepdims=True)
    i1 = jnp.min(jnp.where(le == m1, lane, big), axis=-1, keepdims=True)
    le2 = jnp.where(lane == i1, neg, le)
    m2 = jnp.max(le2, axis=-1, keepdims=True)
    i2 = jnp.min(jnp.where(le2 == m2, lane, big), axis=-1, keepdims=True)
    e2 = jnp.exp(m2 - m1)
    w1 = pg_top / (1.0 + e2)
    w2 = pg_top * e2 / (1.0 + e2)
    j1 = i1 - base + ROUTE_W0
    j2 = i2 - base + ROUTE_W0
    w1h = _bf16(w1).astype(jnp.float32)
    w2h = _bf16(w2).astype(jnp.float32)
    rec = jnp.where(lane == ROUTE_GID, gid,
                    jnp.where(lane == j1, w1h, jnp.where(lane == j2, w2h,
                    jnp.where(lane == j1 + ROUTE_LO, w1 - w1h, jnp.where(lane == j2 + ROUTE_LO, w2 - w2h, 0.0)))))
    h2_out[rows, D_MODEL:] = _bf16(rec)


def _mix_kernel(*refs):
    sub = TM_PROJ // MIX_SUBBLOCKS
    for sb in range(MIX_SUBBLOCKS):
        _mix_rows(slice(sb * sub, (sb + 1) * sub), *refs)


def _mix_call(x2d, oa, ob, lw):
    t = x2d.shape[0]
    tm = TM_PROJ
    row = lambda i: (i, 0)
    weights = [lw["g_mla_out"], lw["g_na_out"], lw["w_out"], lw["g_ffn"], lw["w_r"], lw["b_r"]]
    return pl.pallas_call(
        _mix_kernel, grid=(t // tm,),
        in_specs=[pl.BlockSpec((tm, D_MODEL), row), pl.BlockSpec((tm, MLA_WIDTH), row),
                  pl.BlockSpec((tm, NA_WIDTH), row)] + [_full(w.shape) for w in weights],
        out_specs=[pl.BlockSpec((tm, D_MODEL), row), pl.BlockSpec((tm, H2_COLS), row)],
        out_shape=[jax.ShapeDtypeStruct((t, D_MODEL), jnp.float32),
                   jax.ShapeDtypeStruct((t, H2_COLS), jnp.bfloat16)],
        compiler_params=_cparams(("parallel",)), name="mix_router",
    )(x2d, oa, ob, *weights)


def _moe_kernel(x1_ref, h2_ref, wg_ref, wu_ref, wd_ref, o_ref,
                xs_ref, ys_ref, pt_ref, cnt_ref):
    g = pl.program_id(1)
    tm = TM_MOE

    @pl.when(g == 0)
    def _partition():
        route = h2_ref[:, D_MODEL:].astype(jnp.float32)
        lane = lax.broadcasted_iota(jnp.int32, route.shape, 1)
        gid = jnp.sum(jnp.where(lane == ROUTE_GID, route, 0.0), axis=-1, keepdims=True)
        onehot = jnp.where((lane.astype(jnp.float32) == gid) & (lane < N_GROUPS), 1.0, 0.0)
        r_i = lax.broadcasted_iota(jnp.int32, (RANK_BLOCK, RANK_BLOCK), 0)
        c_i = lax.broadcasted_iota(jnp.int32, (RANK_BLOCK, RANK_BLOCK), 1)
        tri = jnp.where(c_i < r_i, 1.0, 0.0).astype(jnp.bfloat16)
        ranks = []
        sizes = jnp.zeros((1, LANES), jnp.float32)
        for blk in range(tm // RANK_BLOCK):
            oh = onehot[blk * RANK_BLOCK:(blk + 1) * RANK_BLOCK]
            ranks.append(_dot(tri, _bf16(oh)) + sizes)
            sizes = sizes + jnp.sum(oh, axis=0, keepdims=True)
        rank = jnp.concatenate(ranks, axis=0)
        lane1 = lane[0:1]
        start_vec = jnp.zeros((1, LANES), jnp.float32)
        acc = jnp.int32(0)
        for gg in range(N_GROUPS):
            n = jnp.sum(jnp.where(lane1 == gg, sizes, 0.0)).astype(jnp.int32)
            cnt_ref[gg] = acc
            cnt_ref[N_GROUPS + gg] = acc + n
            start_vec = jnp.where(lane1 == gg, acc.astype(jnp.float32), start_vec)
            acc = acc + n
        dest = jnp.sum(onehot * (start_vec + rank), axis=-1, keepdims=True)
        col = lax.broadcasted_iota(jnp.int32, (tm, tm), 1)
        pt_ref[...] = jnp.where(col == dest.astype(jnp.int32), 1.0, 0.0).astype(jnp.bfloat16)
        dest_row = jnp.transpose(jnp.broadcast_to(dest, (tm, LANES)))[0:1, :]
        rowi = lax.broadcasted_iota(jnp.int32, (tm, tm), 0)
        perm = jnp.where(rowi == dest_row.astype(jnp.int32), 1.0, 0.0).astype(jnp.bfloat16)
        xs_ref[:tm, :] = _bf16(_dot(perm, h2_ref[...]))
        xs_ref[tm:, :] = jnp.zeros((MOE_ROWS - tm, H2_COLS), jnp.bfloat16)
        ys_ref[...] = jnp.zeros_like(ys_ref)

    start_g = cnt_ref[g]
    end_g = cnt_ref[N_GROUPS + g]
    first = lax.shift_left(lax.shift_right_logical(start_g, 4), 4)
    n_chunks = lax.div(end_g - first + (MOE_CHUNK - 1), jnp.int32(MOE_CHUNK))

    def _chunk(c, carry):
        off = pl.multiple_of(first + c * MOE_CHUNK, BF16_ROWS)
        xc = xs_ref[pl.ds(off, MOE_CHUNK), :D_MODEL]
        cw = xs_ref[pl.ds(off, MOE_CHUNK), D_MODEL:].astype(jnp.float32)
        hid = []
        for j in range(EXPERTS_PER_GROUP):
            a = _dot(xc, wg_ref[g * EXPERTS_PER_GROUP + j])
            u = _dot(xc, wu_ref[g * EXPERTS_PER_GROUP + j])
            c = cw[:, ROUTE_W0 + j:ROUTE_W0 + j + 1] + cw[:, ROUTE_W0 + ROUTE_LO + j:ROUTE_W0 + ROUTE_LO + j + 1]
            hid.append(a * (1.0 / (1.0 + jnp.exp(-a))) * u * c)
        hid = _bf16(jnp.concatenate(hid, axis=-1))
        w_down = wd_ref[pl.ds(g * EXPERTS_PER_GROUP, EXPERTS_PER_GROUP)].reshape(EXPERTS_PER_GROUP * D_EXPERT, D_MODEL)
        rows = off + lax.broadcasted_iota(jnp.int32, (MOE_CHUNK, 1), 0)
        mine = (rows >= start_g) & (rows < end_g)
        old = ys_ref[pl.ds(off, MOE_CHUNK), :].astype(jnp.float32)
        ys_ref[pl.ds(off, MOE_CHUNK), :] = _bf16(jnp.where(mine, _dot(hid, w_down), old))
        return carry

    lax.fori_loop(0, n_chunks, _chunk, 0)

    @pl.when(g == N_GROUPS - 1)
    def _unpermute():
        o_ref[...] = x1_ref[...] + _dot(pt_ref[...], ys_ref[:tm, :])


def _moe_call(x1, h2, lw):
    t = x1.shape[0]
    tm = TM_MOE
    row = lambda i, g: (i, 0)
    return pl.pallas_call(
        _moe_kernel, grid=(t // tm, N_GROUPS),
        in_specs=[pl.BlockSpec((tm, D_MODEL), row), pl.BlockSpec((tm, H2_COLS), row),
                  pl.BlockSpec(lw["w_g"].shape, lambda i, g: (0, 0, 0), pipeline_mode=pl.Buffered(1)),
                  pl.BlockSpec(lw["w_u"].shape, lambda i, g: (0, 0, 0), pipeline_mode=pl.Buffered(1)),
                  pl.BlockSpec(lw["w_d"].shape, lambda i, g: (0, 0, 0), pipeline_mode=pl.Buffered(1))],
        out_specs=pl.BlockSpec((tm, D_MODEL), row),
        out_shape=jax.ShapeDtypeStruct((t, D_MODEL), jnp.float32),
        scratch_shapes=[pltpu.VMEM((MOE_ROWS, H2_COLS), jnp.bfloat16),
                        pltpu.VMEM((MOE_ROWS, D_MODEL), jnp.bfloat16),
                        pltpu.VMEM((tm, tm), jnp.bfloat16),
                        pltpu.SMEM((2 * N_GROUPS,), jnp.int32)],
        compiler_params=_cparams(("parallel", "arbitrary")), name="moe",
    )(x1, h2, lw["w_g"], lw["w_u"], lw["w_d"])


def _segment_matrix(width, segments):
    m = np.zeros((width, width), np.float32)
    for lo, hi in segments:
        m[lo:hi, lo:hi] = 1.0 / (hi - lo)
    return jnp.asarray(m, jnp.bfloat16)


def _constants(seq):
    t = np.arange(seq)
    row = (t // GRID_W).astype(np.float32)
    col = (t % GRID_W).astype(np.float32)
    n_freq = MLA_ROPE // 4
    inv = (np.float32(ROPE_THETA) ** (-np.arange(n_freq, dtype=np.float32) / n_freq)).astype(np.float32)
    ang_r = row[:, None] * inv[None, :]
    ang_c = col[:, None] * inv[None, :]

    def tables(base):
        cos = np.zeros((seq, LANES), np.float32)
        cos[:, :base] = 1.0
        sa = np.zeros((seq, LANES), np.float32)
        sb = np.zeros((seq, LANES), np.float32)
        for k, ang in enumerate((ang_r, ang_c)):
            o = base + 2 * n_freq * k
            cos[:, o:o + n_freq] = np.cos(ang)
            cos[:, o + n_freq:o + 2 * n_freq] = np.cos(ang)
            sa[:, o:o + n_freq] = -np.sin(ang)
            sb[:, o + n_freq:o + 2 * n_freq] = np.sin(ang)
        return jnp.asarray(cos), jnp.asarray(sa), jnp.asarray(sb)

    cos_q, sin_qa, sin_qb = tables(MLA_NOPE)
    cos_k, sin_ka, sin_kb = tables(0)
    seg_q = []
    for hb in range(0, MXU_TILE, HEAD_PAD):
        seg_q += [(hb, hb + MLA_NOPE), (hb + MLA_NOPE, hb + MLA_QK)]
    shift_lanes = np.zeros((1, MLA_PAD_WIDTH), np.float32)
    shift_lanes[0, SHIFT_LANE::HEAD_PAD] = 1.0
    return dict(cos_q=cos_q, sin_qa=sin_qa, sin_qb=sin_qb, cos_k=cos_k, sin_ka=sin_ka, sin_kb=sin_kb,
                m_q=_segment_matrix(MXU_TILE, seg_q),
                m_kpe=_segment_matrix(LANES, [(0, MLA_ROPE)]),
                m_na=_segment_matrix(MXU_TILE, [(o, o + NA_DH) for o in range(0, MXU_TILE, NA_DH)]),
                shift_lanes=jnp.asarray(shift_lanes), k_one=jnp.asarray(shift_lanes[:, :HEAD_PAD]))


def _na_bias_table(rpb, rows, shift):
    kr = min(NA_KR_MAX, rows)
    cols = np.arange(GRID_W)
    cs = np.clip(cols - NA_KC // 2, 0, GRID_W - NA_KC)
    col_mask = (cols[None, :] >= cs[:, None]) & (cols[None, :] < cs[:, None] + NA_KC)
    dc = np.clip(cols[None, :] - cols[:, None], -(NA_KC - 1), NA_KC - 1) + NA_KC - 1
    onehot = (dc[None] == np.arange(2 * NA_KC - 1)[:, None, None]).astype(np.float32)
    rpb_c = jnp.einsum("hdj,jqk->hdqk", rpb, jnp.asarray(onehot), precision=lax.Precision.HIGHEST)
    rpb_c = jnp.where(jnp.asarray(col_mask)[None, None], rpb_c * LOG2E - shift, MASK_NEG)
    tabs = []
    for c in range(kr):
        lo = NA_KR_MAX - 1 - c
        b = rpb_c[:, lo:lo + kr].transpose(0, 2, 1, 3)
        tabs.append(b.reshape(NA_HEADS, GRID_W, kr * GRID_W))
    return jnp.stack(tabs)


def _softmax_shifts(p, l):
    amax = lambda v: jnp.max(jnp.abs(v))
    gq, gk = p["g_mla_q"][l], p["g_mla_k"][l]
    qn = (MLA_QK ** -0.5 * LOG2E) * jnp.sqrt(MLA_NOPE * amax(gq[:MLA_NOPE]) ** 2 + MLA_ROPE * amax(gq[MLA_NOPE:]) ** 2)
    kn = jnp.sqrt(MLA_NOPE * amax(gk[:MLA_NOPE]) ** 2 + MLA_ROPE * amax(gk[MLA_NOPE:]) ** 2)
    b_mla = BOUND_SLACK * qn * kn
    mla_fast = 2.0 * b_mla <= MAX_SHIFT_GAP
    b_na = BOUND_SLACK * (NA_DH ** -0.5 * LOG2E) * NA_DH * amax(p["g_na_q"][l]) * amax(p["g_na_k"][l])
    bias_hi = jnp.max(p["na_rpb"][l]) * LOG2E
    bias_lo = jnp.min(p["na_rpb"][l]) * LOG2E
    na_fast = 2.0 * b_na + (bias_hi - bias_lo) <= MAX_SHIFT_GAP
    return (mla_fast, jnp.where(mla_fast, b_mla, 0.0)), (na_fast, jnp.where(na_fast, b_na + bias_hi, 0.0))


def _pad_heads(w, width, pad):
    k = w.shape[0]
    w = w.reshape(k, -1, width)
    return jnp.pad(w, ((0, 0), (0, 0), (0, pad - width))).reshape(k, -1)


def _layer_weights(l, p, rows, consts):
    f32 = jnp.float32
    (mla_fast, mla_shift), (na_fast, na_shift) = _softmax_shifts(p, l)
    w_in = p["w_in"][l]
    zeros = jnp.zeros((D_MODEL, LANES - MLA_ROPE), f32)
    w_in_p = jnp.concatenate([w_in[:, :Q_LORA + KV_LORA + MLA_ROPE], zeros, w_in[:, Q_LORA + KV_LORA + MLA_ROPE:]], axis=1)
    w_kv = p["w_kv_b"][l].reshape(KV_LORA, MLA_HEADS, MLA_NOPE + MLA_V)
    w_kc = w_kv[:, :, :MLA_NOPE].reshape(KV_LORA, MLA_HEADS * MLA_NOPE)
    w_v = w_kv[:, :, MLA_NOPE:].reshape(KV_LORA, MLA_WIDTH)
    scale = MLA_QK ** -0.5
    g_q = jnp.tile(jnp.pad(p["g_mla_q"][l] * (scale * LOG2E), (0, HEAD_PAD - MLA_QK)), MLA_HEADS)[None]
    g_kc = jnp.tile(p["g_mla_k"][l][:MLA_NOPE], MLA_HEADS)[None]
    g_kpe = jnp.pad(p["g_mla_k"][l][MLA_NOPE:], (0, LANES - MLA_ROPE))[None]
    w_r = jnp.concatenate([p["w_router_group"][l], p["w_router_expert"][l]], axis=1)
    w_r = jnp.pad(w_r, ((0, 0), (0, LANES - w_r.shape[1])))
    b_r = jnp.pad(jnp.concatenate([p["b_router_group"][l], p["b_router_expert"][l]]), (0, LANES - N_GROUPS * (1 + EXPERTS_PER_GROUP)))[None]
    return dict(
        g_mix=p["g_mix_norm"][l][None], w_in=w_in_p.astype(jnp.bfloat16),
        g_q_a=p["g_q_a"][l][None], w_qb=_pad_heads(p["w_q_b"][l], MLA_QK, HEAD_PAD).astype(jnp.bfloat16),
        g_kv_a=p["g_kv_a"][l][None], w_kc=w_kc.astype(jnp.bfloat16), w_vt=w_v.T.astype(jnp.bfloat16),
        g_q=g_q, g_kc=g_kc, g_kpe=g_kpe,
        g_naq=jnp.tile(p["g_na_q"][l] * (NA_DH ** -0.5 * LOG2E), NA_HEADS)[None],
        g_nak=jnp.tile(p["g_na_k"][l], NA_HEADS)[None],
        na_bias=_na_bias_table(p["na_rpb"][l], rows, na_shift), na_fast=na_fast,
        q_shift=-mla_shift * consts["shift_lanes"], mla_fast=mla_fast,
        g_mla_out=p["g_mla_out"][l][None], g_na_out=p["g_na_out"][l][None],
        w_out=p["w_out"][l].astype(jnp.bfloat16), g_ffn=p["g_ffn_norm"][l][None],
        w_r=w_r.astype(jnp.bfloat16), b_r=b_r,
        w_g=p["w_gate"][l].astype(jnp.bfloat16), w_u=p["w_up"][l].astype(jnp.bfloat16),
        w_d=p["w_down"][l].astype(jnp.bfloat16))


def kernel(x, g_mix_norm, w_in, g_q_a, w_q_b, g_kv_a, w_kv_b, g_mla_q, g_mla_k, g_na_q, g_na_k, na_rpb, g_mla_out, g_na_out, w_out, g_ffn_norm, w_router_group, b_router_group, w_router_expert, b_router_expert, w_gate, w_up, w_down):
    batch, seq, d = x.shape
    assert d == D_MODEL and seq % TM_PROJ == 0 and seq % TQ_MLA == 0 and (batch * seq) % TM_MOE == 0
    rows = seq // GRID_W
    assert rows % NA_ROWS_PER_STEP == 0 and rows >= NA_KR_MAX
    p = dict(g_mix_norm=g_mix_norm, w_in=w_in, g_q_a=g_q_a, w_q_b=w_q_b, g_kv_a=g_kv_a, w_kv_b=w_kv_b,
             g_mla_q=g_mla_q, g_mla_k=g_mla_k, g_na_q=g_na_q, g_na_k=g_na_k, na_rpb=na_rpb,
             g_mla_out=g_mla_out, g_na_out=g_na_out, w_out=w_out, g_ffn_norm=g_ffn_norm,
             w_router_group=w_router_group, b_router_group=b_router_group,
             w_router_expert=w_router_expert, b_router_expert=b_router_expert,
             w_gate=w_gate, w_up=w_up, w_down=w_down)
    consts = _constants(seq)
    xf = x.reshape(batch * seq, d)
    for l in range(w_in.shape[0]):
        lw = _layer_weights(l, p, rows, consts)
        q, k, v, naq, nak, nav = _proj_call(xf, lw, consts, seq)
        o_a = lax.cond(lw["mla_fast"],
                       functools.partial(_mla_call, batch=batch, seq=seq, use_max=False),
                       functools.partial(_mla_call, batch=batch, seq=seq, use_max=True), q, k, v)
        o_b = lax.cond(lw["na_fast"],
                       functools.partial(_na_call, batch=batch, seq=seq, use_max=False),
                       functools.partial(_na_call, batch=batch, seq=seq, use_max=True), naq, nak, nav, lw["na_bias"])
        x1, h2 = _mix_call(xf, o_a, o_b, lw)
        xf = _moe_call(x1, h2, lw)
    return xf.reshape(batch, seq, d)
```

```python
import functools

import numpy as np
import jax
import jax.numpy as jnp
from jax import lax
from jax.experimental import pallas as pl
from jax.experimental.pallas import tpu as pltpu

D_MODEL = 1024
GRID_W = 64
MLA_HEADS = 8
MLA_NOPE = 64
MLA_ROPE = 32
MLA_V = 64
MLA_QK = MLA_NOPE + MLA_ROPE
Q_LORA = 256
KV_LORA = 128
MLA_WIDTH = MLA_HEADS * MLA_V
ROPE_THETA = 10000.0
NA_HEADS = 8
NA_DH = 64
NA_WIDTH = NA_HEADS * NA_DH
NA_KR_MAX = 8
NA_KC = 16
N_GROUPS = 4
EXPERTS_PER_GROUP = 4
D_EXPERT = 256
EPS = 1e-6

LANES = 128
MXU_TILE = 256
BF16_ROWS = 16
HEAD_PAD = LANES
MLA_PAD_WIDTH = MLA_HEADS * HEAD_PAD
PROJ_COLS = 2048
LOG2E = 1.4426950408889634
MASK_NEG = -1e30
SHIFT_LANE = MLA_QK
MAX_SHIFT_GAP = 100.0
BOUND_SLACK = 1.02
VMEM_LIMIT = 56 * 1024 * 1024

C_CQ = 0
C_CKV = C_CQ + Q_LORA
C_KPE = C_CKV + KV_LORA
C_NAQ = C_KPE + LANES
C_NAK = C_NAQ + NA_WIDTH
C_NAV = C_NAK + NA_WIDTH

TM_PROJ = 512
PROJ_SUBBLOCKS = 2
MIX_SUBBLOCKS = 2
TQ_MLA = 512
MLA_HEADS_PER_STEP = 4
NA_ROWS_PER_STEP = 8
TM_MOE = 512
MOE_CHUNK = 160
MOE_ROWS = TM_MOE + MOE_CHUNK
RANK_BLOCK = 256
ROUTE_GID = 0
ROUTE_W0 = 1
ROUTE_LO = 8
H2_COLS = D_MODEL + LANES

assert MLA_NOPE == NA_DH


def _cparams(sem):
    return pltpu.CompilerParams(dimension_semantics=sem, vmem_limit_bytes=VMEM_LIMIT)


def _full(shape):
    nd = len(shape)
    return pl.BlockSpec(shape, lambda *_: (0,) * nd)


def _layer(w, l, **kw):
    nd = w.ndim
    return pl.BlockSpec((None,) + w.shape[1:], lambda *_: (l,) + (0,) * (nd - 1), **kw)


def _bf16(x):
    return x.astype(jnp.bfloat16)


def _dot(a, b):
    return jnp.dot(a, b, preferred_element_type=jnp.float32)


def _dot_nt(a, b):
    return lax.dot_general(a, b, (((1,), (1,)), ((), ())), preferred_element_type=jnp.float32)


def _rms_rows(x, gain):
    ms = jnp.mean(x * x, axis=-1, keepdims=True)
    return x * lax.rsqrt(ms + EPS) * gain


def _segment_rms_scale(x, seg_mat):
    n = x.shape[-1]
    parts = []
    for j in range(0, n, seg_mat.shape[0]):
        w = min(seg_mat.shape[0], n - j)
        xs = x[:, j:j + w]
        parts.append(_dot(_bf16(xs * xs), seg_mat[:w, :w]))
    ms = parts[0] if len(parts) == 1 else jnp.concatenate(parts, axis=-1)
    return lax.rsqrt(ms + EPS)


def _rope_block(x, cos, sin_a, sin_b):
    return x * cos + pltpu.roll(x, LANES - 8, 1) * sin_a + pltpu.roll(x, 8, 1) * sin_b


def _proj_rows(rows, x_ref, gmix_ref, win_ref, gqa_ref, wqb_ref, gkva_ref, wkc_ref, wvt_ref,
               gq_ref, gkc_ref, gkpe_ref, gnaq_ref, gnak_ref,
               mq_ref, mkpe_ref, mna_ref, qshift_ref, kone_ref,
               cq_ref, sqa_ref, sqb_ref, ck_ref, ska_ref, skb_ref,
               q_out, k_out, vt_out, naq_out, nak_out, nav_out):
    h = _bf16(_rms_rows(x_ref[rows, :], gmix_ref[...]))
    proj = _dot(h, win_ref[...])

    cqn = _bf16(_rms_rows(proj[:, C_CQ:C_CKV], gqa_ref[...]))
    q = _dot(cqn, wqb_ref[...])
    q = q * _segment_rms_scale(q, mq_ref[...]) * gq_ref[...]
    cosq, sqa, sqb = cq_ref[rows, :], sqa_ref[rows, :], sqb_ref[rows, :]
    for hd in range(MLA_HEADS):
        sl = slice(hd * HEAD_PAD, (hd + 1) * HEAD_PAD)
        q_out[rows, sl] = _bf16(_rope_block(q[:, sl], cosq, sqa, sqb) + qshift_ref[:, sl])

    ckvn = _bf16(_rms_rows(proj[:, C_CKV:C_KPE], gkva_ref[...]))
    kc = _dot(ckvn, wkc_ref[...])
    kc = kc * _segment_rms_scale(kc, mna_ref[...]) * gkc_ref[...]
    vt_out[:, rows] = _bf16(_dot_nt(wvt_ref[...], ckvn))
    kp = proj[:, C_KPE:C_NAQ]
    kp = kp * lax.rsqrt(_dot(_bf16(kp * kp), mkpe_ref[...]) + EPS) * gkpe_ref[...]
    kp = _rope_block(kp, ck_ref[rows, :], ska_ref[rows, :], skb_ref[rows, :])
    tail = pltpu.roll(kp, MLA_NOPE, 1) + kone_ref[...]
    lane = lax.broadcasted_iota(jnp.int32, tail.shape, 1)
    for pr in range(MLA_HEADS // 2):
        two = kc[:, pr * LANES:(pr + 1) * LANES]
        k_out[rows, (2 * pr) * HEAD_PAD:(2 * pr + 1) * HEAD_PAD] = _bf16(jnp.where(lane < MLA_NOPE, two, tail))
        k_out[rows, (2 * pr + 1) * HEAD_PAD:(2 * pr + 2) * HEAD_PAD] = _bf16(
            jnp.where(lane < MLA_NOPE, pltpu.roll(two, MLA_NOPE, 1), tail))

    naq = proj[:, C_NAQ:C_NAK]
    naq_out[rows, :] = _bf16(naq * _segment_rms_scale(naq, mna_ref[...]) * gnaq_ref[...])
    nak = proj[:, C_NAK:C_NAV]
    nak_out[rows, :] = _bf16(nak * _segment_rms_scale(nak, mna_ref[...]) * gnak_ref[...])
    nav_out[rows, :] = _bf16(proj[:, C_NAV:PROJ_COLS])


def _proj_kernel(*refs):
    sub = TM_PROJ // PROJ_SUBBLOCKS
    for sb in range(PROJ_SUBBLOCKS):
        _proj_rows(slice(sb * sub, (sb + 1) * sub), *refs)


def _proj_call(x2d, sw, l, consts, seq):
    t = x2d.shape[0]
    tm = TM_PROJ
    n_seq_tiles = seq // tm
    row = lambda i: (i, 0)
    tab = lambda i: (i % n_seq_tiles, 0)
    layered = ("g_mix", "w_in", "g_q_a", "w_qb", "g_kv_a", "w_kc", "w_vt", "g_q", "g_kc", "g_kpe", "g_naq", "g_nak")
    weights = [sw[n] for n in layered] + [consts["m_q"], consts["m_kpe"], consts["m_na"], sw["q_shift"], consts["k_one"]]
    specs = ([_layer(sw[n], l) for n in layered] + [_full(consts[n].shape) for n in ("m_q", "m_kpe", "m_na")]
             + [_layer(sw["q_shift"], l), _full(consts["k_one"].shape)])
    tables = [consts["cos_q"], consts["sin_qa"], consts["sin_qb"], consts["cos_k"], consts["sin_ka"], consts["sin_kb"]]
    in_specs = [pl.BlockSpec((tm, D_MODEL), row)] + specs + [pl.BlockSpec((tm, LANES), tab) for _ in tables]
    out_shapes = [jax.ShapeDtypeStruct((t, MLA_PAD_WIDTH), jnp.bfloat16),
                  jax.ShapeDtypeStruct((t, MLA_PAD_WIDTH), jnp.bfloat16),
                  jax.ShapeDtypeStruct((MLA_WIDTH, t), jnp.bfloat16),
                  jax.ShapeDtypeStruct((t, NA_WIDTH), jnp.bfloat16),
                  jax.ShapeDtypeStruct((t, NA_WIDTH), jnp.bfloat16),
                  jax.ShapeDtypeStruct((t, NA_WIDTH), jnp.bfloat16)]
    out_specs = [pl.BlockSpec((MLA_WIDTH, tm), lambda i: (0, i)) if s.shape[0] == MLA_WIDTH
                 else pl.BlockSpec((tm, s.shape[1]), row) for s in out_shapes]
    return pl.pallas_call(
        _proj_kernel, grid=(t // tm,), in_specs=in_specs, out_specs=out_specs, out_shape=out_shapes,
        compiler_params=_cparams(("parallel",)), name="proj",
    )(x2d, *weights, *tables)


def _mla_kernel(q_ref, k_ref, vt_ref, o_ref, *, use_max):
    scores = []
    for hh in range(MLA_HEADS_PER_STEP):
        sl = slice(hh * HEAD_PAD, (hh + 1) * HEAD_PAD)
        scores.append(_dot_nt(k_ref[:, sl], q_ref[:, sl]))
    outs = []
    for hh, s in enumerate(scores):
        p = jnp.exp2(s - jnp.max(s, axis=0, keepdims=True)) if use_max else jnp.exp2(s)
        l = jnp.sum(p, axis=0, keepdims=True)
        pair = slice((hh // 2) * LANES, (hh // 2 + 1) * LANES)
        outs.append(_dot(vt_ref[pair, :], _bf16(p)) * (1.0 / l))
    row = lax.broadcasted_iota(jnp.int32, outs[0].shape, 0)
    for pp in range(MLA_HEADS_PER_STEP // 2):
        o_t = jnp.where(row < MLA_V, outs[2 * pp], outs[2 * pp + 1])
        o_ref[:, pp * LANES:(pp + 1) * LANES] = _bf16(o_t.T)


def _mla_call(q, k, vt, *, batch, seq, use_max):
    t = q.shape[0]
    tq = TQ_MLA
    nq = seq // tq
    hps = MLA_HEADS_PER_STEP
    return pl.pallas_call(
        functools.partial(_mla_kernel, use_max=use_max), grid=(batch, MLA_HEADS // hps, nq),
        in_specs=[pl.BlockSpec((tq, hps * HEAD_PAD), lambda b, p, i: (b * nq + i, p)),
                  pl.BlockSpec((seq, hps * HEAD_PAD), lambda b, p, i: (b, p)),
                  pl.BlockSpec((hps * MLA_V, seq), lambda b, p, i: (p, b))],
        out_specs=pl.BlockSpec((tq, hps * MLA_V), lambda b, p, i: (b * nq + i, p)),
        out_shape=jax.ShapeDtypeStruct((t, MLA_WIDTH), jnp.bfloat16),
        compiler_params=_cparams(("parallel", "parallel", "arbitrary")), name="mla_attn",
    )(q, k, vt)


def _na_kernel(q_ref, k_ref, v_ref, bias_ref, o_ref, *, rows, kr, use_max):
    step = pl.program_id(1)
    band = kr * GRID_W
    lane = lax.broadcasted_iota(jnp.int32, (GRID_W, LANES), 1)
    units = []
    for rr in range(NA_ROWS_PER_STEP):
        r = step * NA_ROWS_PER_STEP + rr
        rs = jnp.clip(r - kr // 2, 0, rows - kr)
        cls = r - rs
        start = pl.multiple_of(rs * GRID_W, GRID_W)
        qrow = q_ref[rr * GRID_W:(rr + 1) * GRID_W, :]
        for pr in range(NA_HEADS // 2):
            sl = slice(pr * LANES, (pr + 1) * LANES)
            q2 = qrow[:, sl]
            zero = jnp.zeros_like(q2)
            qq = jnp.concatenate([jnp.where(lane < NA_DH, q2, zero), jnp.where(lane >= NA_DH, q2, zero)], axis=0)
            s = _dot_nt(qq, k_ref[pl.ds(start, band), sl])
            units.append((rr, pr, start, cls, s))
    probs = []
    for rr, pr, start, cls, s in units:
        s = s + bias_ref[cls, 2 * pr:2 * pr + 2].reshape(2 * GRID_W, band)
        p = jnp.exp2(s - jnp.max(s, axis=-1, keepdims=True)) if use_max else jnp.exp2(s)
        l = jnp.sum(p, axis=-1, keepdims=True)
        probs.append((_bf16(p), 1.0 / l))
    for (rr, pr, start, cls, s), (p, rl) in zip(units, probs):
        sl = slice(pr * LANES, (pr + 1) * LANES)
        pv = _dot(p, v_ref[pl.ds(start, band), sl]) * rl
        o_ref[rr * GRID_W:(rr + 1) * GRID_W, sl] = _bf16(jnp.where(lane < NA_DH, pv[:GRID_W], pv[GRID_W:]))


def _na_call(q, k, v, bias, *, l, batch, seq, use_max):
    t = q.shape[0]
    rows = seq // GRID_W
    kr = min(NA_KR_MAX, rows)
    steps = rows // NA_ROWS_PER_STEP
    blk = NA_ROWS_PER_STEP * GRID_W
    return pl.pallas_call(
        functools.partial(_na_kernel, rows=rows, kr=kr, use_max=use_max), grid=(batch, steps),
        in_specs=[pl.BlockSpec((blk, NA_WIDTH), lambda b, i: (b * steps + i, 0)),
                  pl.BlockSpec((seq, NA_WIDTH), lambda b, i: (b, 0)),
                  pl.BlockSpec((seq, NA_WIDTH), lambda b, i: (b, 0)),
                  _layer(bias, l)],
        out_specs=pl.BlockSpec((blk, NA_WIDTH), lambda b, i: (b * steps + i, 0)),
        out_shape=jax.ShapeDtypeStruct((t, NA_WIDTH), jnp.bfloat16),
        compiler_params=_cparams(("parallel", "arbitrary")), name="na_attn",
    )(q, k, v, bias)


def _mix_rows(rows, x_ref, oa_ref, ob_ref, ga_ref, gb_ref, wout_ref, gffn_ref, wr_ref, br_ref, x1_out, h2_out):
    mixed = jnp.concatenate([_rms_rows(oa_ref[rows, :].astype(jnp.float32), ga_ref[...]),
                             _rms_rows(ob_ref[rows, :].astype(jnp.float32), gb_ref[...])], axis=-1)
    x1 = x_ref[rows, :] + _dot(_bf16(mixed), wout_ref[...])
    x1_out[rows, :] = x1
    hi = _bf16(_rms_rows(x1, gffn_ref[...]))
    h2_out[rows, :D_MODEL] = hi
    logits = _dot(hi, wr_ref[...]) + br_ref[...]

    lane = lax.broadcasted_iota(jnp.int32, logits.shape, 1).astype(jnp.float32)
    neg = jnp.float32(-jnp.inf)
    big = jnp.float32(LANES)
    is_grp = lane < N_GROUPS
    lg = jnp.where(is_grp, logits, neg)
    gmax = jnp.max(lg, axis=-1, keepdims=True)
    gid = jnp.min(jnp.where(lg == gmax, lane, big), axis=-1, keepdims=True)
    pg_top = 1.0 / jnp.sum(jnp.where(is_grp, jnp.exp(logits - gmax), 0.0), axis=-1, keepdims=True)
    base = N_GROUPS + EXPERTS_PER_GROUP * gid
    in_sel = (lane >= base) & (lane < base + EXPERTS_PER_GROUP)
    le = jnp.where(in_sel, logits, neg)
    m1 = jnp.max(le, axis=-1, keepdims=True)
    i1 = jnp.min(jnp.where(le == m1, lane, big), axis=-1, keepdims=True)
    le2 = jnp.where(lane == i1, neg, le)
    m2 = jnp.max(le2, axis=-1, keepdims=True)
    i2 = jnp.min(jnp.where(le2 == m2, lane, big), axis=-1, keepdims=True)
    e2 = jnp.exp(m2 - m1)
    w1 = pg_top / (1.0 + e2)
    w2 = pg_top * e2 / (1.0 + e2)
    j1 = i1 - base + ROUTE_W0
    j2 = i2 - base + ROUTE_W0
    w1h = _bf16(w1).astype(jnp.float32)
    w2h = _bf16(w2).astype(jnp.float32)
    rec = jnp.where(lane == ROUTE_GID, gid,
                    jnp.where(lane == j1, w1h, jnp.where(lane == j2, w2h,
                    jnp.where(lane == j1 + ROUTE_LO, w1 - w1h, jnp.where(lane == j2 + ROUTE_LO, w2 - w2h, 0.0)))))
    h2_out[rows, D_MODEL:] = _bf16(rec)


def _mix_kernel(*refs):
    sub = TM_PROJ // MIX_SUBBLOCKS
    for sb in range(MIX_SUBBLOCKS):
        _mix_rows(slice(sb * sub, (sb + 1) * sub), *refs)


def _mix_call(x2d, oa, ob, sw, l):
    t = x2d.shape[0]
    tm = TM_PROJ
    row = lambda i: (i, 0)
    weights = [sw[n] for n in ("g_mla_out", "g_na_out", "w_out", "g_ffn", "w_r", "b_r")]
    return pl.pallas_call(
        _mix_kernel, grid=(t // tm,),
        in_specs=[pl.BlockSpec((tm, D_MODEL), row), pl.BlockSpec((tm, MLA_WIDTH), row),
                  pl.BlockSpec((tm, NA_WIDTH), row)] + [_layer(w, l) for w in weights],
        out_specs=[pl.BlockSpec((tm, D_MODEL), row), pl.BlockSpec((tm, H2_COLS), row)],
        out_shape=[jax.ShapeDtypeStruct((t, D_MODEL), jnp.float32),
                   jax.ShapeDtypeStruct((t, H2_COLS), jnp.bfloat16)],
        compiler_params=_cparams(("parallel",)), name="mix_router",
    )(x2d, oa, ob, *weights)


def _moe_kernel(x1_ref, h2_ref, wg_ref, wu_ref, wd_ref, o_ref, xs_ref, ys_ref):
    tm = TM_MOE
    route = h2_ref[:, D_MODEL:].astype(jnp.float32)
    lane = lax.broadcasted_iota(jnp.int32, route.shape, 1)
    gid = jnp.sum(jnp.where(lane == ROUTE_GID, route, 0.0), axis=-1, keepdims=True)
    onehot = jnp.where((lane.astype(jnp.float32) == gid) & (lane < N_GROUPS), 1.0, 0.0)
    r_i = lax.broadcasted_iota(jnp.int32, (RANK_BLOCK, RANK_BLOCK), 0)
    c_i = lax.broadcasted_iota(jnp.int32, (RANK_BLOCK, RANK_BLOCK), 1)
    tri = jnp.where(c_i < r_i, 1.0, 0.0).astype(jnp.bfloat16)
    ranks = []
    sizes = jnp.zeros((1, LANES), jnp.float32)
    for blk in range(tm // RANK_BLOCK):
        oh = onehot[blk * RANK_BLOCK:(blk + 1) * RANK_BLOCK]
        ranks.append(_dot(tri, _bf16(oh)) + sizes)
        sizes = sizes + jnp.sum(oh, axis=0, keepdims=True)
    rank = jnp.concatenate(ranks, axis=0)
    lane1 = lane[0:1]
    start_vec = jnp.zeros((1, LANES), jnp.float32)
    bounds = []
    acc = jnp.int32(0)
    for g in range(N_GROUPS):
        n = jnp.sum(jnp.where(lane1 == g, sizes, 0.0)).astype(jnp.int32)
        bounds.append((acc, acc + n))
        start_vec = jnp.where(lane1 == g, acc.astype(jnp.float32), start_vec)
        acc = acc + n
    dest = jnp.sum(onehot * (start_vec + rank), axis=-1, keepdims=True)
    dest_row = jnp.transpose(jnp.broadcast_to(dest, (tm, LANES)))[0:1, :]
    rowi = lax.broadcasted_iota(jnp.int32, (tm, tm), 0)
    perm = jnp.where(rowi == dest_row.astype(jnp.int32), 1.0, 0.0).astype(jnp.bfloat16)
    xs_ref[:tm, :] = _bf16(_dot(perm, h2_ref[...]))
    xs_ref[tm:, :] = jnp.zeros((MOE_ROWS - tm, H2_COLS), jnp.bfloat16)
    ys_ref[...] = jnp.zeros_like(ys_ref)

    for g, (start_g, end_g) in enumerate(bounds):
        first = lax.shift_left(lax.shift_right_logical(start_g, 4), 4)
        n_chunks = lax.div(end_g - first + (MOE_CHUNK - 1), jnp.int32(MOE_CHUNK))
        w_down = wd_ref[g * EXPERTS_PER_GROUP:(g + 1) * EXPERTS_PER_GROUP].reshape(EXPERTS_PER_GROUP * D_EXPERT, D_MODEL)

        def _chunk(c, carry, g=g, start_g=start_g, end_g=end_g, first=first, w_down=w_down):
            off = pl.multiple_of(first + c * MOE_CHUNK, BF16_ROWS)
            xc = xs_ref[pl.ds(off, MOE_CHUNK), :D_MODEL]
            cw = xs_ref[pl.ds(off, MOE_CHUNK), D_MODEL:].astype(jnp.float32)
            hid = []
            for j in range(EXPERTS_PER_GROUP):
                a = _dot(xc, wg_ref[g * EXPERTS_PER_GROUP + j])
                u = _dot(xc, wu_ref[g * EXPERTS_PER_GROUP + j])
                cj = cw[:, ROUTE_W0 + j:ROUTE_W0 + j + 1] + cw[:, ROUTE_W0 + ROUTE_LO + j:ROUTE_W0 + ROUTE_LO + j + 1]
                hid.append(a * (1.0 / (1.0 + jnp.exp(-a))) * u * cj)
            hid = _bf16(jnp.concatenate(hid, axis=-1))
            rows = off + lax.broadcasted_iota(jnp.int32, (MOE_CHUNK, 1), 0)
            mine = (rows >= start_g) & (rows < end_g)
            old = ys_ref[pl.ds(off, MOE_CHUNK), :].astype(jnp.float32)
            ys_ref[pl.ds(off, MOE_CHUNK), :] = _bf16(jnp.where(mine, _dot(hid, w_down), old))
            return carry

        lax.fori_loop(0, n_chunks, _chunk, 0)

    col = lax.broadcasted_iota(jnp.int32, (tm, tm), 1)
    unperm = jnp.where(col == dest.astype(jnp.int32), 1.0, 0.0).astype(jnp.bfloat16)
    o_ref[...] = x1_ref[...] + _dot(unperm, ys_ref[:tm, :])


def _moe_call(x1, h2, sw, l):
    t = x1.shape[0]
    tm = TM_MOE
    row = lambda i: (i, 0)
    resident = lambda w: _layer(w, l, pipeline_mode=pl.Buffered(1))
    return pl.pallas_call(
        _moe_kernel, grid=(t // tm,),
        in_specs=[pl.BlockSpec((tm, D_MODEL), row), pl.BlockSpec((tm, H2_COLS), row),
                  resident(sw["w_g"]), resident(sw["w_u"]), resident(sw["w_d"])],
        out_specs=pl.BlockSpec((tm, D_MODEL), row),
        out_shape=jax.ShapeDtypeStruct((t, D_MODEL), jnp.float32),
        scratch_shapes=[pltpu.VMEM((MOE_ROWS, H2_COLS), jnp.bfloat16),
                        pltpu.VMEM((MOE_ROWS, D_MODEL), jnp.bfloat16)],
        compiler_params=_cparams(("parallel",)), name="moe",
    )(x1, h2, sw["w_g"], sw["w_u"], sw["w_d"])


def _segment_matrix(width, segments):
    m = np.zeros((width, width), np.float32)
    for lo, hi in segments:
        m[lo:hi, lo:hi] = 1.0 / (hi - lo)
    return jnp.asarray(m, jnp.bfloat16)


def _constants(seq):
    t = np.arange(seq)
    row = (t // GRID_W).astype(np.float32)
    col = (t % GRID_W).astype(np.float32)
    n_freq = MLA_ROPE // 4
    inv = (np.float32(ROPE_THETA) ** (-np.arange(n_freq, dtype=np.float32) / n_freq)).astype(np.float32)
    ang_r = row[:, None] * inv[None, :]
    ang_c = col[:, None] * inv[None, :]

    def tables(base):
        cos = np.zeros((seq, LANES), np.float32)
        cos[:, :base] = 1.0
        sa = np.zeros((seq, LANES), np.float32)
        sb = np.zeros((seq, LANES), np.float32)
        for k, ang in enumerate((ang_r, ang_c)):
            o = base + 2 * n_freq * k
            cos[:, o:o + n_freq] = np.cos(ang)
            cos[:, o + n_freq:o + 2 * n_freq] = np.cos(ang)
            sa[:, o:o + n_freq] = -np.sin(ang)
            sb[:, o + n_freq:o + 2 * n_freq] = np.sin(ang)
        return jnp.asarray(cos), jnp.asarray(sa), jnp.asarray(sb)

    cos_q, sin_qa, sin_qb = tables(MLA_NOPE)
    cos_k, sin_ka, sin_kb = tables(0)
    seg_q = []
    for hb in range(0, MXU_TILE, HEAD_PAD):
        seg_q += [(hb, hb + MLA_NOPE), (hb + MLA_NOPE, hb + MLA_QK)]
    shift_lanes = np.zeros((1, MLA_PAD_WIDTH), np.float32)
    shift_lanes[0, SHIFT_LANE::HEAD_PAD] = 1.0
    return dict(cos_q=cos_q, sin_qa=sin_qa, sin_qb=sin_qb, cos_k=cos_k, sin_ka=sin_ka, sin_kb=sin_kb,
                m_q=_segment_matrix(MXU_TILE, seg_q),
                m_kpe=_segment_matrix(LANES, [(0, MLA_ROPE)]),
                m_na=_segment_matrix(MXU_TILE, [(o, o + NA_DH) for o in range(0, MXU_TILE, NA_DH)]),
                shift_lanes=jnp.asarray(shift_lanes), k_one=jnp.asarray(shift_lanes[:, :HEAD_PAD]))


def _na_bias_tables(rpb, rows, shift):
    kr = min(NA_KR_MAX, rows)
    cols = np.arange(GRID_W)
    cs = np.clip(cols - NA_KC // 2, 0, GRID_W - NA_KC)
    col_mask = (cols[None, :] >= cs[:, None]) & (cols[None, :] < cs[:, None] + NA_KC)
    dc = np.clip(cols[None, :] - cols[:, None], -(NA_KC - 1), NA_KC - 1) + NA_KC - 1
    onehot = (dc[None] == np.arange(2 * NA_KC - 1)[:, None, None]).astype(np.float32)
    rpb_c = jnp.einsum("lhdj,jqk->lhdqk", rpb, jnp.asarray(onehot), precision=lax.Precision.HIGHEST)
    rpb_c = jnp.where(jnp.asarray(col_mask), rpb_c * LOG2E - shift[:, None, None, None, None], MASK_NEG)
    tabs = []
    for c in range(kr):
        lo = NA_KR_MAX - 1 - c
        b = rpb_c[:, :, lo:lo + kr].transpose(0, 1, 3, 2, 4)
        tabs.append(b.reshape(rpb.shape[0], NA_HEADS, GRID_W, kr * GRID_W))
    return jnp.stack(tabs, axis=1)


def _softmax_shifts(p):
    amax = lambda v: jnp.max(jnp.abs(v), axis=-1)
    gq, gk = p["g_mla_q"], p["g_mla_k"]
    qn = (MLA_QK ** -0.5 * LOG2E) * jnp.sqrt(MLA_NOPE * amax(gq[:, :MLA_NOPE]) ** 2 + MLA_ROPE * amax(gq[:, MLA_NOPE:]) ** 2)
    kn = jnp.sqrt(MLA_NOPE * amax(gk[:, :MLA_NOPE]) ** 2 + MLA_ROPE * amax(gk[:, MLA_NOPE:]) ** 2)
    b_mla = BOUND_SLACK * qn * kn
    mla_fast = 2.0 * b_mla <= MAX_SHIFT_GAP
    b_na = BOUND_SLACK * (NA_DH ** -0.5 * LOG2E) * NA_DH * amax(p["g_na_q"]) * amax(p["g_na_k"])
    bias_hi = jnp.max(p["na_rpb"], axis=(1, 2, 3)) * LOG2E
    bias_lo = jnp.min(p["na_rpb"], axis=(1, 2, 3)) * LOG2E
    na_fast = 2.0 * b_na + (bias_hi - bias_lo) <= MAX_SHIFT_GAP
    return (mla_fast, jnp.where(mla_fast, b_mla, 0.0)), (na_fast, jnp.where(na_fast, b_na + bias_hi, 0.0))


def _prep_weights(p, rows, consts):
    bf = jnp.bfloat16
    nl = p["w_in"].shape[0]
    (mla_fast, mla_shift), (na_fast, na_shift) = _softmax_shifts(p)
    w_in = p["w_in"]
    split = Q_LORA + KV_LORA + MLA_ROPE
    zeros = jnp.zeros((nl, D_MODEL, LANES - MLA_ROPE), w_in.dtype)
    w_in_p = jnp.concatenate([w_in[:, :, :split], zeros, w_in[:, :, split:]], axis=2)
    w_qb = jnp.pad(p["w_q_b"].reshape(nl, Q_LORA, MLA_HEADS, MLA_QK), ((0, 0), (0, 0), (0, 0), (0, HEAD_PAD - MLA_QK)))
    w_kv = p["w_kv_b"].reshape(nl, KV_LORA, MLA_HEADS, MLA_NOPE + MLA_V)
    w_kc = w_kv[..., :MLA_NOPE].reshape(nl, KV_LORA, MLA_HEADS * MLA_NOPE)
    w_vt = w_kv[..., MLA_NOPE:].reshape(nl, KV_LORA, MLA_WIDTH).transpose(0, 2, 1)
    g_q = jnp.tile(jnp.pad(p["g_mla_q"] * (MLA_QK ** -0.5 * LOG2E), ((0, 0), (0, HEAD_PAD - MLA_QK))), (1, MLA_HEADS))
    w_r = jnp.concatenate([p["w_router_group"], p["w_router_expert"]], axis=2)
    w_r = jnp.pad(w_r, ((0, 0), (0, 0), (0, LANES - w_r.shape[2])))
    b_r = jnp.concatenate([p["b_router_group"], p["b_router_expert"]], axis=1)
    b_r = jnp.pad(b_r, ((0, 0), (0, LANES - b_r.shape[1])))
    row = lambda v: v[:, None, :]
    return dict(
        g_mix=row(p["g_mix_norm"]), w_in=w_in_p.astype(bf),
        g_q_a=row(p["g_q_a"]), w_qb=w_qb.reshape(nl, Q_LORA, MLA_PAD_WIDTH).astype(bf),
        g_kv_a=row(p["g_kv_a"]), w_kc=w_kc.astype(bf), w_vt=w_vt.astype(bf),
        g_q=row(g_q), g_kc=row(jnp.tile(p["g_mla_k"][:, :MLA_NOPE], (1, MLA_HEADS))),
        g_kpe=row(jnp.pad(p["g_mla_k"][:, MLA_NOPE:], ((0, 0), (0, LANES - MLA_ROPE)))),
        g_naq=row(jnp.tile(p["g_na_q"] * (NA_DH ** -0.5 * LOG2E), (1, NA_HEADS))),
        g_nak=row(jnp.tile(p["g_na_k"], (1, NA_HEADS))),
        na_bias=_na_bias_tables(p["na_rpb"], rows, na_shift), na_fast=na_fast,
        q_shift=-mla_shift[:, None, None] * consts["shift_lanes"][None], mla_fast=mla_fast,
        g_mla_out=row(p["g_mla_out"]), g_na_out=row(p["g_na_out"]),
        w_out=p["w_out"].astype(bf), g_ffn=row(p["g_ffn_norm"]),
        w_r=w_r.astype(bf), b_r=row(b_r),
        w_g=p["w_gate"].astype(bf), w_u=p["w_up"].astype(bf), w_d=p["w_down"].astype(bf))


def kernel(x, g_mix_norm, w_in, g_q_a, w_q_b, g_kv_a, w_kv_b, g_mla_q, g_mla_k, g_na_q, g_na_k, na_rpb, g_mla_out, g_na_out, w_out, g_ffn_norm, w_router_group, b_router_group, w_router_expert, b_router_expert, w_gate, w_up, w_down):
    batch, seq, d = x.shape
    assert d == D_MODEL and seq % TM_PROJ == 0 and seq % TQ_MLA == 0 and (batch * seq) % TM_MOE == 0
    rows = seq // GRID_W
    assert rows % NA_ROWS_PER_STEP == 0 and rows >= NA_KR_MAX
    p = dict(g_mix_norm=g_mix_norm, w_in=w_in, g_q_a=g_q_a, w_q_b=w_q_b, g_kv_a=g_kv_a, w_kv_b=w_kv_b,
             g_mla_q=g_mla_q, g_mla_k=g_mla_k, g_na_q=g_na_q, g_na_k=g_na_k, na_rpb=na_rpb,
             g_mla_out=g_mla_out, g_na_out=g_na_out, w_out=w_out, g_ffn_norm=g_ffn_norm,
             w_router_group=w_router_group, b_router_group=b_router_group,
             w_router_expert=w_router_expert, b_router_expert=b_router_expert,
             w_gate=w_gate, w_up=w_up, w_down=w_down)
    consts = _constants(seq)
    sw = _prep_weights(p, rows, consts)
    xf = x.reshape(batch * seq, d)
    for l in range(w_in.shape[0]):
        q, k, vt, naq, nak, nav = _proj_call(xf, sw, l, consts, seq)
        o_a = lax.cond(sw["mla_fast"][l],
                       functools.partial(_mla_call, batch=batch, seq=seq, use_max=False),
                       functools.partial(_mla_call, batch=batch, seq=seq, use_max=True), q, k, vt)
        o_b = lax.cond(sw["na_fast"][l],
                       functools.partial(_na_call, l=l, batch=batch, seq=seq, use_max=False),
                       functools.partial(_na_call, l=l, batch=batch, seq=seq, use_max=True), naq, nak, nav, sw["na_bias"])
        x1, h2 = _mix_call(xf, o_a, o_b, sw, l)
        xf = _moe_call(x1, h2, sw, l)
    return xf.reshape(batch, seq, d)
```

```python
import functools

import numpy as np
import jax
import jax.numpy as jnp
from jax import lax
from jax.experimental import pallas as pl
from jax.experimental.pallas import tpu as pltpu

D_MODEL = 1024
GRID_W = 64
MLA_HEADS = 8
MLA_NOPE = 64
MLA_ROPE = 32
MLA_V = 64
MLA_QK = MLA_NOPE + MLA_ROPE
Q_LORA = 256
KV_LORA = 128
MLA_WIDTH = MLA_HEADS * MLA_V
ROPE_THETA = 10000.0
NA_HEADS = 8
NA_DH = 64
NA_WIDTH = NA_HEADS * NA_DH
NA_KR_MAX = 8
NA_KC = 16
N_GROUPS = 4
EXPERTS_PER_GROUP = 4
D_EXPERT = 256
EPS = 1e-6

LANES = 128
MXU_TILE = 256
BF16_ROWS = 16
HEAD_PAD = LANES
MLA_PAD_WIDTH = MLA_HEADS * HEAD_PAD
PROJ_COLS = 2048
LOG2E = 1.4426950408889634
MASK_NEG = -1e30
SHIFT_LANE = MLA_QK
MAX_SHIFT_GAP = 100.0
BOUND_SLACK = 1.02
VMEM_LIMIT = 56 * 1024 * 1024

C_CQ = 0
C_CKV = C_CQ + Q_LORA
C_KPE = C_CKV + KV_LORA
C_NAQ = C_KPE + LANES
C_NAK = C_NAQ + NA_WIDTH
C_NAV = C_NAK + NA_WIDTH

TM_PROJ = 512
PROJ_SUBBLOCKS = 2
MIX_SUBBLOCKS = 2
TQ_MLA = 512
MLA_HEADS_PER_STEP = 4
NA_ROWS_PER_STEP = 8
TM_MOE = 512
MOE_SINGLE_CHUNKS = (128, 192)
MOE_LOOP_CHUNK = 256
MOE_ROWS = TM_MOE + MOE_LOOP_CHUNK
RANK_BLOCK = 256
ROUTE_GID = 0
ROUTE_W0 = 1
ROUTE_LO = 8
H2_COLS = D_MODEL + LANES

assert MLA_NOPE == NA_DH


def _cparams(sem):
    return pltpu.CompilerParams(dimension_semantics=sem, vmem_limit_bytes=VMEM_LIMIT)


def _full(shape):
    nd = len(shape)
    return pl.BlockSpec(shape, lambda *_: (0,) * nd)


def _layer(w, l, **kw):
    nd = w.ndim
    return pl.BlockSpec((None,) + w.shape[1:], lambda *_: (l,) + (0,) * (nd - 1), **kw)


def _bf16(x):
    return x.astype(jnp.bfloat16)


def _dot(a, b):
    return jnp.dot(a, b, preferred_element_type=jnp.float32)


def _dot_nt(a, b):
    return lax.dot_general(a, b, (((1,), (1,)), ((), ())), preferred_element_type=jnp.float32)


def _rms_rows(x, gain):
    ms = jnp.mean(x * x, axis=-1, keepdims=True)
    return x * lax.rsqrt(ms + EPS) * gain


def _segment_rms_scale(x, seg_mat):
    n = x.shape[-1]
    parts = []
    for j in range(0, n, seg_mat.shape[0]):
        w = min(seg_mat.shape[0], n - j)
        xs = x[:, j:j + w]
        parts.append(_dot(_bf16(xs * xs), seg_mat[:w, :w]))
    ms = parts[0] if len(parts) == 1 else jnp.concatenate(parts, axis=-1)
    return lax.rsqrt(ms + EPS)


def _rope_block(x, cos, sin_a, sin_b):
    return x * cos + pltpu.roll(x, LANES - 8, 1) * sin_a + pltpu.roll(x, 8, 1) * sin_b


def _proj_rows(rows, x_ref, gmix_ref, win_ref, gqa_ref, wqb_ref, gkva_ref, wkc_ref, wvt_ref,
               gq_ref, gkc_ref, gkpe_ref, gnaq_ref, gnak_ref,
               mq_ref, mkpe_ref, mna_ref, qshift_ref, kone_ref,
               cq_ref, sqa_ref, sqb_ref, ck_ref, ska_ref, skb_ref,
               q_out, k_out, vt_out, naq_out, nak_out, nav_out):
    h = _bf16(_rms_rows(x_ref[rows, :], gmix_ref[...]))
    proj = _dot(h, win_ref[...])

    cqn = _bf16(_rms_rows(proj[:, C_CQ:C_CKV], gqa_ref[...]))
    q = _dot(cqn, wqb_ref[...])
    q = q * _segment_rms_scale(q, mq_ref[...]) * gq_ref[...]
    cosq, sqa, sqb = cq_ref[rows, :], sqa_ref[rows, :], sqb_ref[rows, :]
    for hd in range(MLA_HEADS):
        sl = slice(hd * HEAD_PAD, (hd + 1) * HEAD_PAD)
        q_out[rows, sl] = _bf16(_rope_block(q[:, sl], cosq, sqa, sqb) + qshift_ref[:, sl])

    ckvn = _bf16(_rms_rows(proj[:, C_CKV:C_KPE], gkva_ref[...]))
    kc = _dot(ckvn, wkc_ref[...])
    kc = kc * _segment_rms_scale(kc, mna_ref[...]) * gkc_ref[...]
    vt_out[:, rows] = _bf16(_dot_nt(wvt_ref[...], ckvn))
    kp = proj[:, C_KPE:C_NAQ]
    kp = kp * lax.rsqrt(_dot(_bf16(kp * kp), mkpe_ref[...]) + EPS) * gkpe_ref[...]
    kp = _rope_block(kp, ck_ref[rows, :], ska_ref[rows, :], skb_ref[rows, :])
    tail = pltpu.roll(kp, MLA_NOPE, 1) + kone_ref[...]
    lane = lax.broadcasted_iota(jnp.int32, tail.shape, 1)
    for pr in range(MLA_HEADS // 2):
        two = kc[:, pr * LANES:(pr + 1) * LANES]
        k_out[rows, (2 * pr) * HEAD_PAD:(2 * pr + 1) * HEAD_PAD] = _bf16(jnp.where(lane < MLA_NOPE, two, tail))
        k_out[rows, (2 * pr + 1) * HEAD_PAD:(2 * pr + 2) * HEAD_PAD] = _bf16(
            jnp.where(lane < MLA_NOPE, pltpu.roll(two, MLA_NOPE, 1), tail))

    naq = proj[:, C_NAQ:C_NAK]
    naq_out[rows, :] = _bf16(naq * _segment_rms_scale(naq, mna_ref[...]) * gnaq_ref[...])
    nak = proj[:, C_NAK:C_NAV]
    nak_out[rows, :] = _bf16(nak * _segment_rms_scale(nak, mna_ref[...]) * gnak_ref[...])
    nav_out[rows, :] = _bf16(proj[:, C_NAV:PROJ_COLS])


def _proj_kernel(*refs):
    sub = TM_PROJ // PROJ_SUBBLOCKS
    for sb in range(PROJ_SUBBLOCKS):
        _proj_rows(slice(sb * sub, (sb + 1) * sub), *refs)


def _proj_call(x2d, sw, l, consts, seq):
    t = x2d.shape[0]
    tm = TM_PROJ
    n_seq_tiles = seq // tm
    row = lambda i: (i, 0)
    tab = lambda i: (i % n_seq_tiles, 0)
    layered = ("g_mix", "w_in", "g_q_a", "w_qb", "g_kv_a", "w_kc", "w_vt", "g_q", "g_kc", "g_kpe", "g_naq", "g_nak")
    weights = [sw[n] for n in layered] + [consts["m_q"], consts["m_kpe"], consts["m_na"], sw["q_shift"], consts["k_one"]]
    specs = ([_layer(sw[n], l) for n in layered] + [_full(consts[n].shape) for n in ("m_q", "m_kpe", "m_na")]
             + [_layer(sw["q_shift"], l), _full(consts["k_one"].shape)])
    tables = [consts["cos_q"], consts["sin_qa"], consts["sin_qb"], consts["cos_k"], consts["sin_ka"], consts["sin_kb"]]
    in_specs = [pl.BlockSpec((tm, D_MODEL), row)] + specs + [pl.BlockSpec((tm, LANES), tab) for _ in tables]
    out_shapes = [jax.ShapeDtypeStruct((t, MLA_PAD_WIDTH), jnp.bfloat16),
                  jax.ShapeDtypeStruct((t, MLA_PAD_WIDTH), jnp.bfloat16),
                  jax.ShapeDtypeStruct((MLA_WIDTH, t), jnp.bfloat16),
                  jax.ShapeDtypeStruct((t, NA_WIDTH), jnp.bfloat16),
                  jax.ShapeDtypeStruct((t, NA_WIDTH), jnp.bfloat16),
                  jax.ShapeDtypeStruct((t, NA_WIDTH), jnp.bfloat16)]
    out_specs = [pl.BlockSpec((MLA_WIDTH, tm), lambda i: (0, i)) if s.shape[0] == MLA_WIDTH
                 else pl.BlockSpec((tm, s.shape[1]), row) for s in out_shapes]
    return pl.pallas_call(
        _proj_kernel, grid=(t // tm,), in_specs=in_specs, out_specs=out_specs, out_shape=out_shapes,
        compiler_params=_cparams(("parallel",)), name="proj",
    )(x2d, *weights, *tables)


def _mla_kernel(q_ref, k_ref, vt_ref, o_ref, *, use_max):
    scores = []
    for hh in range(MLA_HEADS_PER_STEP):
        sl = slice(hh * HEAD_PAD, (hh + 1) * HEAD_PAD)
        scores.append(_dot_nt(k_ref[:, sl], q_ref[:, sl]))
    outs = []
    for hh, s in enumerate(scores):
        p = jnp.exp2(s - jnp.max(s, axis=0, keepdims=True)) if use_max else jnp.exp2(s)
        l = jnp.sum(p, axis=0, keepdims=True)
        pair = slice((hh // 2) * LANES, (hh // 2 + 1) * LANES)
        outs.append(_dot(vt_ref[pair, :], _bf16(p)) * (1.0 / l))
    row = lax.broadcasted_iota(jnp.int32, outs[0].shape, 0)
    for pp in range(MLA_HEADS_PER_STEP // 2):
        o_t = jnp.where(row < MLA_V, outs[2 * pp], outs[2 * pp + 1])
        o_ref[:, pp * LANES:(pp + 1) * LANES] = _bf16(o_t.T)


def _mla_call(q, k, vt, *, batch, seq, use_max):
    t = q.shape[0]
    tq = TQ_MLA
    nq = seq // tq
    hps = MLA_HEADS_PER_STEP
    return pl.pallas_call(
        functools.partial(_mla_kernel, use_max=use_max), grid=(batch, MLA_HEADS // hps, nq),
        in_specs=[pl.BlockSpec((tq, hps * HEAD_PAD), lambda b, p, i: (b * nq + i, p)),
                  pl.BlockSpec((seq, hps * HEAD_PAD), lambda b, p, i: (b, p)),
                  pl.BlockSpec((hps * MLA_V, seq), lambda b, p, i: (p, b))],
        out_specs=pl.BlockSpec((tq, hps * MLA_V), lambda b, p, i: (b * nq + i, p)),
        out_shape=jax.ShapeDtypeStruct((t, MLA_WIDTH), jnp.bfloat16),
        compiler_params=_cparams(("parallel", "parallel", "arbitrary")), name="mla_attn",
    )(q, k, vt)


def _na_kernel(q_ref, k_ref, v_ref, bias_ref, o_ref, *, rows, kr, use_max):
    step = pl.program_id(1)
    band = kr * GRID_W
    lane = lax.broadcasted_iota(jnp.int32, (GRID_W, LANES), 1)
    units = []
    for rr in range(NA_ROWS_PER_STEP):
        r = step * NA_ROWS_PER_STEP + rr
        rs = jnp.clip(r - kr // 2, 0, rows - kr)
        cls = r - rs
        start = pl.multiple_of(rs * GRID_W, GRID_W)
        qrow = q_ref[rr * GRID_W:(rr + 1) * GRID_W, :]
        for pr in range(NA_HEADS // 2):
            sl = slice(pr * LANES, (pr + 1) * LANES)
            q2 = qrow[:, sl]
            zero = jnp.zeros_like(q2)
            qq = jnp.concatenate([jnp.where(lane < NA_DH, q2, zero), jnp.where(lane >= NA_DH, q2, zero)], axis=0)
            s = _dot_nt(qq, k_ref[pl.ds(start, band), sl])
            units.append((rr, pr, start, cls, s))
    probs = []
    for rr, pr, start, cls, s in units:
        s = s + bias_ref[cls, 2 * pr:2 * pr + 2].reshape(2 * GRID_W, band)
        p = jnp.exp2(s - jnp.max(s, axis=-1, keepdims=True)) if use_max else jnp.exp2(s)
        l = jnp.sum(p, axis=-1, keepdims=True)
        probs.append((_bf16(p), 1.0 / l))
    for (rr, pr, start, cls, s), (p, rl) in zip(units, probs):
        sl = slice(pr * LANES, (pr + 1) * LANES)
        pv = _dot(p, v_ref[pl.ds(start, band), sl]) * rl
        o_ref[rr * GRID_W:(rr + 1) * GRID_W, sl] = _bf16(jnp.where(lane < NA_DH, pv[:GRID_W], pv[GRID_W:]))


def _na_call(q, k, v, bias, *, l, batch, seq, use_max):
    t = q.shape[0]
    rows = seq // GRID_W
    kr = min(NA_KR_MAX, rows)
    steps = rows // NA_ROWS_PER_STEP
    blk = NA_ROWS_PER_STEP * GRID_W
    return pl.pallas_call(
        functools.partial(_na_kernel, rows=rows, kr=kr, use_max=use_max), grid=(batch, steps),
        in_specs=[pl.BlockSpec((blk, NA_WIDTH), lambda b, i: (b * steps + i, 0)),
                  pl.BlockSpec((seq, NA_WIDTH), lambda b, i: (b, 0)),
                  pl.BlockSpec((seq, NA_WIDTH), lambda b, i: (b, 0)),
                  _layer(bias, l)],
        out_specs=pl.BlockSpec((blk, NA_WIDTH), lambda b, i: (b * steps + i, 0)),
        out_shape=jax.ShapeDtypeStruct((t, NA_WIDTH), jnp.bfloat16),
        compiler_params=_cparams(("parallel", "arbitrary")), name="na_attn",
    )(q, k, v, bias)


def _mix_rows(rows, x_ref, oa_ref, ob_ref, ga_ref, gb_ref, wout_ref, gffn_ref, wr_ref, br_ref, x1_out, h2_out):
    mixed = jnp.concatenate([_rms_rows(oa_ref[rows, :].astype(jnp.float32), ga_ref[...]),
                             _rms_rows(ob_ref[rows, :].astype(jnp.float32), gb_ref[...])], axis=-1)
    x1 = x_ref[rows, :] + _dot(_bf16(mixed), wout_ref[...])
    x1_out[rows, :] = x1
    hi = _bf16(_rms_rows(x1, gffn_ref[...]))
    h2_out[rows, :D_MODEL] = hi
    logits = _dot(hi, wr_ref[...]) + br_ref[...]

    lane = lax.broadcasted_iota(jnp.int32, logits.shape, 1).astype(jnp.float32)
    neg = jnp.float32(-jnp.inf)
    big = jnp.float32(LANES)
    is_grp = lane < N_GROUPS
    lg = jnp.where(is_grp, logits, neg)
    gmax = jnp.max(lg, axis=-1, keepdims=True)
    gid = jnp.min(jnp.where(lg == gmax, lane, big), axis=-1, keepdims=True)
    pg_top = 1.0 / jnp.sum(jnp.where(is_grp, jnp.exp(logits - gmax), 0.0), axis=-1, keepdims=True)
    base = N_GROUPS + EXPERTS_PER_GROUP * gid
    in_sel = (lane >= base) & (lane < base + EXPERTS_PER_GROUP)
    le = jnp.where(in_sel, logits, neg)
    m1 = jnp.max(le, axis=-1, keepdims=True)
    i1 = jnp.min(jnp.where(le == m1, lane, big), axis=-1, keepdims=True)
    le2 = jnp.where(lane == i1, neg, le)
    m2 = jnp.max(le2, axis=-1, keepdims=True)
    i2 = jnp.min(jnp.where(le2 == m2, lane, big), axis=-1, keepdims=True)
    e2 = jnp.exp(m2 - m1)
    w1 = pg_top / (1.0 + e2)
    w2 = pg_top * e2 / (1.0 + e2)
    j1 = i1 - base + ROUTE_W0
    j2 = i2 - base + ROUTE_W0
    w1h = _bf16(w1).astype(jnp.float32)
    w2h = _bf16(w2).astype(jnp.float32)
    rec = jnp.where(lane == ROUTE_GID, gid,
                    jnp.where(lane == j1, w1h, jnp.where(lane == j2, w2h,
                    jnp.where(lane == j1 + ROUTE_LO, w1 - w1h, jnp.where(lane == j2 + ROUTE_LO, w2 - w2h, 0.0)))))
    h2_out[rows, D_MODEL:] = _bf16(rec)


def _mix_kernel(*refs):
    sub = TM_PROJ // MIX_SUBBLOCKS
    for sb in range(MIX_SUBBLOCKS):
        _mix_rows(slice(sb * sub, (sb + 1) * sub), *refs)


def _mix_call(x2d, oa, ob, sw, l):
    t = x2d.shape[0]
    tm = TM_PROJ
    row = lambda i: (i, 0)
    weights = [sw[n] for n in ("g_mla_out", "g_na_out", "w_out", "g_ffn", "w_r", "b_r")]
    return pl.pallas_call(
        _mix_kernel, grid=(t // tm,),
        in_specs=[pl.BlockSpec((tm, D_MODEL), row), pl.BlockSpec((tm, MLA_WIDTH), row),
                  pl.BlockSpec((tm, NA_WIDTH), row)] + [_layer(w, l) for w in weights],
        out_specs=[pl.BlockSpec((tm, D_MODEL), row), pl.BlockSpec((tm, H2_COLS), row)],
        out_shape=[jax.ShapeDtypeStruct((t, D_MODEL), jnp.float32),
                   jax.ShapeDtypeStruct((t, H2_COLS), jnp.bfloat16)],
        compiler_params=_cparams(("parallel",)), name="mix_router",
    )(x2d, oa, ob, *weights)


def _moe_kernel(x1_ref, h2_ref, wg_ref, wu_ref, wd_ref, o_ref, xs_ref, ys_ref):
    tm = TM_MOE
    route = h2_ref[:, D_MODEL:].astype(jnp.float32)
    lane = lax.broadcasted_iota(jnp.int32, route.shape, 1)
    gid = jnp.sum(jnp.where(lane == ROUTE_GID, route, 0.0), axis=-1, keepdims=True)
    onehot = jnp.where((lane.astype(jnp.float32) == gid) & (lane < N_GROUPS), 1.0, 0.0)
    r_i = lax.broadcasted_iota(jnp.int32, (RANK_BLOCK, RANK_BLOCK), 0)
    c_i = lax.broadcasted_iota(jnp.int32, (RANK_BLOCK, RANK_BLOCK), 1)
    tri = jnp.where(c_i < r_i, 1.0, 0.0).astype(jnp.bfloat16)
    ranks = []
    sizes = jnp.zeros((1, LANES), jnp.float32)
    for blk in range(tm // RANK_BLOCK):
        oh = onehot[blk * RANK_BLOCK:(blk + 1) * RANK_BLOCK]
        ranks.append(_dot(tri, _bf16(oh)) + sizes)
        sizes = sizes + jnp.sum(oh, axis=0, keepdims=True)
    rank = jnp.concatenate(ranks, axis=0)
    lane1 = lane[0:1]
    start_vec = jnp.zeros((1, LANES), jnp.float32)
    bounds = []
    acc = jnp.int32(0)
    for g in range(N_GROUPS):
        n = jnp.sum(jnp.where(lane1 == g, sizes, 0.0)).astype(jnp.int32)
        bounds.append((acc, acc + n))
        start_vec = jnp.where(lane1 == g, acc.astype(jnp.float32), start_vec)
        acc = acc + n
    dest = jnp.sum(onehot * (start_vec + rank), axis=-1, keepdims=True)
    dest_row = jnp.transpose(jnp.broadcast_to(dest, (tm, LANES)))[0:1, :]
    rowi = lax.broadcasted_iota(jnp.int32, (tm, tm), 0)
    perm = jnp.where(rowi == dest_row.astype(jnp.int32), 1.0, 0.0).astype(jnp.bfloat16)
    xs_ref[:tm, :] = _bf16(_dot(perm, h2_ref[...]))
    xs_ref[tm:, :] = jnp.zeros((MOE_ROWS - tm, H2_COLS), jnp.bfloat16)
    ys_ref[...] = jnp.zeros_like(ys_ref)

    def _experts(g, off, size, start_g, end_g):
        off = pl.multiple_of(off, BF16_ROWS)
        xc = xs_ref[pl.ds(off, size), :D_MODEL]
        cw = xs_ref[pl.ds(off, size), D_MODEL:].astype(jnp.float32)
        hid = []
        for j in range(EXPERTS_PER_GROUP):
            a = _dot(xc, wg_ref[g * EXPERTS_PER_GROUP + j])
            u = _dot(xc, wu_ref[g * EXPERTS_PER_GROUP + j])
            cj = cw[:, ROUTE_W0 + j:ROUTE_W0 + j + 1] + cw[:, ROUTE_W0 + ROUTE_LO + j:ROUTE_W0 + ROUTE_LO + j + 1]
            hid.append(a * (1.0 / (1.0 + jnp.exp(-a))) * u * cj)
        hid = _bf16(jnp.concatenate(hid, axis=-1))
        w_down = wd_ref[g * EXPERTS_PER_GROUP:(g + 1) * EXPERTS_PER_GROUP].reshape(EXPERTS_PER_GROUP * D_EXPERT, D_MODEL)
        rows = off + lax.broadcasted_iota(jnp.int32, (size, 1), 0)
        mine = (rows >= start_g) & (rows < end_g)
        old = ys_ref[pl.ds(off, size), :].astype(jnp.float32)
        ys_ref[pl.ds(off, size), :] = _bf16(jnp.where(mine, _dot(hid, w_down), old))

    for g, (start_g, end_g) in enumerate(bounds):
        first = lax.shift_left(lax.shift_right_logical(start_g, 4), 4)
        span = end_g - first
        lo = 0
        for size in MOE_SINGLE_CHUNKS:
            @pl.when((span > lo) & (span <= size))
            def _(g=g, size=size, start_g=start_g, end_g=end_g, first=first):
                _experts(g, first, size, start_g, end_g)
            lo = size

        @pl.when(span > lo)
        def _(g=g, start_g=start_g, end_g=end_g, first=first, span=span):
            def _chunk(c, carry):
                _experts(g, first + c * MOE_LOOP_CHUNK, MOE_LOOP_CHUNK, start_g, end_g)
                return carry
            lax.fori_loop(0, lax.div(span + (MOE_LOOP_CHUNK - 1), jnp.int32(MOE_LOOP_CHUNK)), _chunk, 0)

    col = lax.broadcasted_iota(jnp.int32, (tm, tm), 1)
    unperm = jnp.where(col == dest.astype(jnp.int32), 1.0, 0.0).astype(jnp.bfloat16)
    o_ref[...] = x1_ref[...] + _dot(unperm, ys_ref[:tm, :])


def _moe_call(x1, h2, sw, l):
    t = x1.shape[0]
    tm = TM_MOE
    row = lambda i: (i, 0)
    resident = lambda w: _layer(w, l, pipeline_mode=pl.Buffered(1))
    return pl.pallas_call(
        _moe_kernel, grid=(t // tm,),
        in_specs=[pl.BlockSpec((tm, D_MODEL), row), pl.BlockSpec((tm, H2_COLS), row),
                  resident(sw["w_g"]), resident(sw["w_u"]), resident(sw["w_d"])],
        out_specs=pl.BlockSpec((tm, D_MODEL), row),
        out_shape=jax.ShapeDtypeStruct((t, D_MODEL), jnp.float32),
        scratch_shapes=[pltpu.VMEM((MOE_ROWS, H2_COLS), jnp.bfloat16),
                        pltpu.VMEM((MOE_ROWS, D_MODEL), jnp.bfloat16)],
        compiler_params=_cparams(("parallel",)), name="moe",
    )(x1, h2, sw["w_g"], sw["w_u"], sw["w_d"])


def _segment_matrix(width, segments):
    m = np.zeros((width, width), np.float32)
    for lo, hi in segments:
        m[lo:hi, lo:hi] = 1.0 / (hi - lo)
    return jnp.asarray(m, jnp.bfloat16)


def _constants(seq):
    t = np.arange(seq)
    row = (t // GRID_W).astype(np.float32)
    col = (t % GRID_W).astype(np.float32)
    n_freq = MLA_ROPE // 4
    inv = (np.float32(ROPE_THETA) ** (-np.arange(n_freq, dtype=np.float32) / n_freq)).astype(np.float32)
    ang_r = row[:, None] * inv[None, :]
    ang_c = col[:, None] * inv[None, :]

    def tables(base):
        cos = np.zeros((seq, LANES), np.float32)
        cos[:, :base] = 1.0
        sa = np.zeros((seq, LANES), np.float32)
        sb = np.zeros((seq, LANES), np.float32)
        for k, ang in enumerate((ang_r, ang_c)):
            o = base + 2 * n_freq * k
            cos[:, o:o + n_freq] = np.cos(ang)
            cos[:, o + n_freq:o + 2 * n_freq] = np.cos(ang)
            sa[:, o:o + n_freq] = -np.sin(ang)
            sb[:, o + n_freq:o + 2 * n_freq] = np.sin(ang)
        return jnp.asarray(cos), jnp.asarray(sa), jnp.asarray(sb)

    cos_q, sin_qa, sin_qb = tables(MLA_NOPE)
    cos_k, sin_ka, sin_kb = tables(0)
    seg_q = []
    for hb in range(0, MXU_TILE, HEAD_PAD):
        seg_q += [(hb, hb + MLA_NOPE), (hb + MLA_NOPE, hb + MLA_QK)]
    shift_lanes = np.zeros((1, MLA_PAD_WIDTH), np.float32)
    shift_lanes[0, SHIFT_LANE::HEAD_PAD] = 1.0
    return dict(cos_q=cos_q, sin_qa=sin_qa, sin_qb=sin_qb, cos_k=cos_k, sin_ka=sin_ka, sin_kb=sin_kb,
                m_q=_segment_matrix(MXU_TILE, seg_q),
                m_kpe=_segment_matrix(LANES, [(0, MLA_ROPE)]),
                m_na=_segment_matrix(MXU_TILE, [(o, o + NA_DH) for o in range(0, MXU_TILE, NA_DH)]),
                shift_lanes=jnp.asarray(shift_lanes), k_one=jnp.asarray(shift_lanes[:, :HEAD_PAD]))


def _na_bias_tables(rpb, rows, shift):
    kr = min(NA_KR_MAX, rows)
    cols = np.arange(GRID_W)
    cs = np.clip(cols - NA_KC // 2, 0, GRID_W - NA_KC)
    col_mask = (cols[None, :] >= cs[:, None]) & (cols[None, :] < cs[:, None] + NA_KC)
    dc = np.clip(cols[None, :] - cols[:, None], -(NA_KC - 1), NA_KC - 1) + NA_KC - 1
    onehot = (dc[None] == np.arange(2 * NA_KC - 1)[:, None, None]).astype(np.float32)
    rpb_c = jnp.einsum("lhdj,jqk->lhdqk", rpb, jnp.asarray(onehot), precision=lax.Precision.HIGHEST)
    rpb_c = jnp.where(jnp.asarray(col_mask), rpb_c * LOG2E - shift[:, None, None, None, None], MASK_NEG)
    tabs = []
    for c in range(kr):
        lo = NA_KR_MAX - 1 - c
        b = rpb_c[:, :, lo:lo + kr].transpose(0, 1, 3, 2, 4)
        tabs.append(b.reshape(rpb.shape[0], NA_HEADS, GRID_W, kr * GRID_W))
    return jnp.stack(tabs, axis=1)


def _softmax_shifts(p):
    amax = lambda v: jnp.max(jnp.abs(v), axis=-1)
    gq, gk = p["g_mla_q"], p["g_mla_k"]
    qn = (MLA_QK ** -0.5 * LOG2E) * jnp.sqrt(MLA_NOPE * amax(gq[:, :MLA_NOPE]) ** 2 + MLA_ROPE * amax(gq[:, MLA_NOPE:]) ** 2)
    kn = jnp.sqrt(MLA_NOPE * amax(gk[:, :MLA_NOPE]) ** 2 + MLA_ROPE * amax(gk[:, MLA_NOPE:]) ** 2)
    b_mla = BOUND_SLACK * qn * kn
    mla_fast = 2.0 * b_mla <= MAX_SHIFT_GAP
    b_na = BOUND_SLACK * (NA_DH ** -0.5 * LOG2E) * NA_DH * amax(p["g_na_q"]) * amax(p["g_na_k"])
    bias_hi = jnp.max(p["na_rpb"], axis=(1, 2, 3)) * LOG2E
    bias_lo = jnp.min(p["na_rpb"], axis=(1, 2, 3)) * LOG2E
    na_fast = 2.0 * b_na + (bias_hi - bias_lo) <= MAX_SHIFT_GAP
    return (mla_fast, jnp.where(mla_fast, b_mla, 0.0)), (na_fast, jnp.where(na_fast, b_na + bias_hi, 0.0))


def _prep_weights(p, rows, consts):
    bf = jnp.bfloat16
    nl = p["w_in"].shape[0]
    (mla_fast, mla_shift), (na_fast, na_shift) = _softmax_shifts(p)
    w_in = p["w_in"]
    split = Q_LORA + KV_LORA + MLA_ROPE
    zeros = jnp.zeros((nl, D_MODEL, LANES - MLA_ROPE), w_in.dtype)
    w_in_p = jnp.concatenate([w_in[:, :, :split], zeros, w_in[:, :, split:]], axis=2)
    w_qb = jnp.pad(p["w_q_b"].reshape(nl, Q_LORA, MLA_HEADS, MLA_QK), ((0, 0), (0, 0), (0, 0), (0, HEAD_PAD - MLA_QK)))
    w_kv = p["w_kv_b"].reshape(nl, KV_LORA, MLA_HEADS, MLA_NOPE + MLA_V)
    w_kc = w_kv[..., :MLA_NOPE].reshape(nl, KV_LORA, MLA_HEADS * MLA_NOPE)
    w_vt = w_kv[..., MLA_NOPE:].reshape(nl, KV_LORA, MLA_WIDTH).transpose(0, 2, 1)
    g_q = jnp.tile(jnp.pad(p["g_mla_q"] * (MLA_QK ** -0.5 * LOG2E), ((0, 0), (0, HEAD_PAD - MLA_QK))), (1, MLA_HEADS))
    w_r = jnp.concatenate([p["w_router_group"], p["w_router_expert"]], axis=2)
    w_r = jnp.pad(w_r, ((0, 0), (0, 0), (0, LANES - w_r.shape[2])))
    b_r = jnp.concatenate([p["b_router_group"], p["b_router_expert"]], axis=1)
    b_r = jnp.pad(b_r, ((0, 0), (0, LANES - b_r.shape[1])))
    row = lambda v: v[:, None, :]
    return dict(
        g_mix=row(p["g_mix_norm"]), w_in=w_in_p.astype(bf),
        g_q_a=row(p["g_q_a"]), w_qb=w_qb.reshape(nl, Q_LORA, MLA_PAD_WIDTH).astype(bf),
        g_kv_a=row(p["g_kv_a"]), w_kc=w_kc.astype(bf), w_vt=w_vt.astype(bf),
        g_q=row(g_q), g_kc=row(jnp.tile(p["g_mla_k"][:, :MLA_NOPE], (1, MLA_HEADS))),
        g_kpe=row(jnp.pad(p["g_mla_k"][:, MLA_NOPE:], ((0, 0), (0, LANES - MLA_ROPE)))),
        g_naq=row(jnp.tile(p["g_na_q"] * (NA_DH ** -0.5 * LOG2E), (1, NA_HEADS))),
        g_nak=row(jnp.tile(p["g_na_k"], (1, NA_HEADS))),
        na_bias=_na_bias_tables(p["na_rpb"], rows, na_shift), na_fast=na_fast,
        q_shift=-mla_shift[:, None, None] * consts["shift_lanes"][None], mla_fast=mla_fast,
        g_mla_out=row(p["g_mla_out"]), g_na_out=row(p["g_na_out"]),
        w_out=p["w_out"].astype(bf), g_ffn=row(p["g_ffn_norm"]),
        w_r=w_r.astype(bf), b_r=row(b_r),
        w_g=p["w_gate"].astype(bf), w_u=p["w_up"].astype(bf), w_d=p["w_down"].astype(bf))


def kernel(x, g_mix_norm, w_in, g_q_a, w_q_b, g_kv_a, w_kv_b, g_mla_q, g_mla_k, g_na_q, g_na_k, na_rpb, g_mla_out, g_na_out, w_out, g_ffn_norm, w_router_group, b_router_group, w_router_expert, b_router_expert, w_gate, w_up, w_down):
    batch, seq, d = x.shape
    assert d == D_MODEL and seq % TM_PROJ == 0 and seq % TQ_MLA == 0 and (batch * seq) % TM_MOE == 0
    rows = seq // GRID_W
    assert rows % NA_ROWS_PER_STEP == 0 and rows >= NA_KR_MAX
    p = dict(g_mix_norm=g_mix_norm, w_in=w_in, g_q_a=g_q_a, w_q_b=w_q_b, g_kv_a=g_kv_a, w_kv_b=w_kv_b,
             g_mla_q=g_mla_q, g_mla_k=g_mla_k, g_na_q=g_na_q, g_na_k=g_na_k, na_rpb=na_rpb,
             g_mla_out=g_mla_out, g_na_out=g_na_out, w_out=w_out, g_ffn_norm=g_ffn_norm,
             w_router_group=w_router_group, b_router_group=b_router_group,
             w_router_expert=w_router_expert, b_router_expert=b_router_expert,
             w_gate=w_gate, w_up=w_up, w_down=w_down)
    consts = _constants(seq)
    sw = _prep_weights(p, rows, consts)
    xf = x.reshape(batch * seq, d)
    for l in range(w_in.shape[0]):
        q, k, vt, naq, nak, nav = _proj_call(xf, sw, l, consts, seq)
        o_a = lax.cond(sw["mla_fast"][l],
                       functools.partial(_mla_call, batch=batch, seq=seq, use_max=False),
                       functools.partial(_mla_call, batch=batch, seq=seq, use_max=True), q, k, vt)
        o_b = lax.cond(sw["na_fast"][l],
                       functools.partial(_na_call, l=l, batch=batch, seq=seq, use_max=False),
                       functools.partial(_na_call, l=l, batch=batch, seq=seq, use_max=True), naq, nak, nav, sw["na_bias"])
        x1, h2 = _mix_call(xf, o_a, o_b, sw, l)
        xf = _moe_call(x1, h2, sw, l)
    return xf.reshape(batch, seq, d)
```

```python
import functools

import numpy as np
import jax
import jax.numpy as jnp
from jax import lax
from jax.experimental import pallas as pl
from jax.experimental.pallas import tpu as pltpu

D_MODEL = 1024
GRID_W = 64
MLA_HEADS = 8
MLA_NOPE = 64
MLA_ROPE = 32
MLA_V = 64
MLA_QK = MLA_NOPE + MLA_ROPE
Q_LORA = 256
KV_LORA = 128
MLA_WIDTH = MLA_HEADS * MLA_V
ROPE_THETA = 10000.0
NA_HEADS = 8
NA_DH = 64
NA_WIDTH = NA_HEADS * NA_DH
NA_KR_MAX = 8
NA_KC = 16
N_GROUPS = 4
EXPERTS_PER_GROUP = 4
D_EXPERT = 256
EPS = 1e-6

LANES = 128
MXU_TILE = 256
BF16_ROWS = 16
HEAD_PAD = LANES
MLA_PAD_WIDTH = MLA_HEADS * HEAD_PAD
PROJ_COLS = 2048
LOG2E = 1.4426950408889634
MASK_NEG = -1e30
SHIFT_LANE = MLA_QK
MAX_SHIFT_GAP = 100.0
BOUND_SLACK = 1.02
VMEM_LIMIT = 56 * 1024 * 1024

C_CQ = 0
C_CKV = C_CQ + Q_LORA
C_KPE = C_CKV + KV_LORA
C_NAQ = C_KPE + LANES
C_NAK = C_NAQ + NA_WIDTH
C_NAV = C_NAK + NA_WIDTH

TM_PROJ = 1024
PROJ_SUBBLOCKS = 4
MIX_SUBBLOCKS = 4
TQ_MLA = 512
MLA_HEADS_PER_STEP = 4
NA_ROWS_PER_STEP = 16
TM_MOE = 512
MOE_SINGLE_CHUNKS = (128, 192)
MOE_LOOP_CHUNK = 256
MOE_ROWS = TM_MOE + MOE_LOOP_CHUNK
RANK_BLOCK = 256
ROUTE_GID = 0
ROUTE_W0 = 1
ROUTE_LO = 8
H2_COLS = D_MODEL + LANES

assert MLA_NOPE == NA_DH


def _cparams(sem):
    return pltpu.CompilerParams(dimension_semantics=sem, vmem_limit_bytes=VMEM_LIMIT)


def _full(shape):
    nd = len(shape)
    return pl.BlockSpec(shape, lambda *_: (0,) * nd)


def _layer(w, l, **kw):
    nd = w.ndim
    return pl.BlockSpec((None,) + w.shape[1:], lambda *_: (l,) + (0,) * (nd - 1), **kw)


def _bf16(x):
    return x.astype(jnp.bfloat16)


def _dot(a, b):
    return jnp.dot(a, b, preferred_element_type=jnp.float32)


def _dot_nt(a, b):
    return lax.dot_general(a, b, (((1,), (1,)), ((), ())), preferred_element_type=jnp.float32)


def _rms_rows(x, gain):
    ms = jnp.mean(x * x, axis=-1, keepdims=True)
    return x * lax.rsqrt(ms + EPS) * gain


def _segment_rms_scale(x, seg_mat):
    n = x.shape[-1]
    parts = []
    for j in range(0, n, seg_mat.shape[0]):
        w = min(seg_mat.shape[0], n - j)
        xs = x[:, j:j + w]
        parts.append(_dot(_bf16(xs * xs), seg_mat[:w, :w]))
    ms = parts[0] if len(parts) == 1 else jnp.concatenate(parts, axis=-1)
    return lax.rsqrt(ms + EPS)


def _rope_block(x, cos, sin_a, sin_b):
    return x * cos + pltpu.roll(x, LANES - 8, 1) * sin_a + pltpu.roll(x, 8, 1) * sin_b


def _proj_rows(rows, x_ref, gmix_ref, win_ref, gqa_ref, wqb_ref, gkva_ref, wkc_ref, wvt_ref,
               gq_ref, gkc_ref, gkpe_ref, gnaq_ref, gnak_ref,
               mq_ref, mkpe_ref, mna_ref, qshift_ref, kone_ref,
               cq_ref, sqa_ref, sqb_ref, ck_ref, ska_ref, skb_ref,
               q_out, k_out, vt_out, naq_out, nak_out, nav_out):
    h = _bf16(_rms_rows(x_ref[rows, :], gmix_ref[...]))
    proj = _dot(h, win_ref[...])

    cqn = _bf16(_rms_rows(proj[:, C_CQ:C_CKV], gqa_ref[...]))
    q = _dot(cqn, wqb_ref[...])
    q = q * _segment_rms_scale(q, mq_ref[...]) * gq_ref[...]
    cosq, sqa, sqb = cq_ref[rows, :], sqa_ref[rows, :], sqb_ref[rows, :]
    for hd in range(MLA_HEADS):
        sl = slice(hd * HEAD_PAD, (hd + 1) * HEAD_PAD)
        q_out[rows, sl] = _bf16(_rope_block(q[:, sl], cosq, sqa, sqb) + qshift_ref[:, sl])

    ckvn = _bf16(_rms_rows(proj[:, C_CKV:C_KPE], gkva_ref[...]))
    kc = _dot(ckvn, wkc_ref[...])
    kc = kc * _segment_rms_scale(kc, mna_ref[...]) * gkc_ref[...]
    vt_out[:, rows] = _bf16(_dot_nt(wvt_ref[...], ckvn))
    kp = proj[:, C_KPE:C_NAQ]
    kp = kp * lax.rsqrt(_dot(_bf16(kp * kp), mkpe_ref[...]) + EPS) * gkpe_ref[...]
    kp = _rope_block(kp, ck_ref[rows, :], ska_ref[rows, :], skb_ref[rows, :])
    tail = pltpu.roll(kp, MLA_NOPE, 1) + kone_ref[...]
    lane = lax.broadcasted_iota(jnp.int32, tail.shape, 1)
    for pr in range(MLA_HEADS // 2):
        two = kc[:, pr * LANES:(pr + 1) * LANES]
        k_out[rows, (2 * pr) * HEAD_PAD:(2 * pr + 1) * HEAD_PAD] = _bf16(jnp.where(lane < MLA_NOPE, two, tail))
        k_out[rows, (2 * pr + 1) * HEAD_PAD:(2 * pr + 2) * HEAD_PAD] = _bf16(
            jnp.where(lane < MLA_NOPE, pltpu.roll(two, MLA_NOPE, 1), tail))

    naq = proj[:, C_NAQ:C_NAK]
    naq_out[rows, :] = _bf16(naq * _segment_rms_scale(naq, mna_ref[...]) * gnaq_ref[...])
    nak = proj[:, C_NAK:C_NAV]
    nak_out[rows, :] = _bf16(nak * _segment_rms_scale(nak, mna_ref[...]) * gnak_ref[...])
    nav_out[rows, :] = _bf16(proj[:, C_NAV:PROJ_COLS])


def _proj_kernel(*refs):
    sub = TM_PROJ // PROJ_SUBBLOCKS
    for sb in range(PROJ_SUBBLOCKS):
        _proj_rows(slice(sb * sub, (sb + 1) * sub), *refs)


def _proj_call(x2d, sw, l, consts, seq):
    t = x2d.shape[0]
    tm = TM_PROJ
    n_seq_tiles = seq // tm
    row = lambda i: (i, 0)
    tab = lambda i: (i % n_seq_tiles, 0)
    layered = ("g_mix", "w_in", "g_q_a", "w_qb", "g_kv_a", "w_kc", "w_vt", "g_q", "g_kc", "g_kpe", "g_naq", "g_nak")
    weights = [sw[n] for n in layered] + [consts["m_q"], consts["m_kpe"], consts["m_na"], sw["q_shift"], consts["k_one"]]
    specs = ([_layer(sw[n], l) for n in layered] + [_full(consts[n].shape) for n in ("m_q", "m_kpe", "m_na")]
             + [_layer(sw["q_shift"], l), _full(consts["k_one"].shape)])
    tables = [consts["cos_q"], consts["sin_qa"], consts["sin_qb"], consts["cos_k"], consts["sin_ka"], consts["sin_kb"]]
    in_specs = [pl.BlockSpec((tm, D_MODEL), row)] + specs + [pl.BlockSpec((tm, LANES), tab) for _ in tables]
    out_shapes = [jax.ShapeDtypeStruct((t, MLA_PAD_WIDTH), jnp.bfloat16),
                  jax.ShapeDtypeStruct((t, MLA_PAD_WIDTH), jnp.bfloat16),
                  jax.ShapeDtypeStruct((MLA_WIDTH, t), jnp.bfloat16),
                  jax.ShapeDtypeStruct((t, NA_WIDTH), jnp.bfloat16),
                  jax.ShapeDtypeStruct((t, NA_WIDTH), jnp.bfloat16),
                  jax.ShapeDtypeStruct((t, NA_WIDTH), jnp.bfloat16)]
    out_specs = [pl.BlockSpec((MLA_WIDTH, tm), lambda i: (0, i)) if s.shape[0] == MLA_WIDTH
                 else pl.BlockSpec((tm, s.shape[1]), row) for s in out_shapes]
    return pl.pallas_call(
        _proj_kernel, grid=(t // tm,), in_specs=in_specs, out_specs=out_specs, out_shape=out_shapes,
        compiler_params=_cparams(("parallel",)), name="proj",
    )(x2d, *weights, *tables)


def _mla_kernel(q_ref, k_ref, vt_ref, o_ref, *, use_max):
    scores = []
    for hh in range(MLA_HEADS_PER_STEP):
        sl = slice(hh * HEAD_PAD, (hh + 1) * HEAD_PAD)
        scores.append(_dot_nt(k_ref[:, sl], q_ref[:, sl]))
    outs = []
    for hh, s in enumerate(scores):
        p = jnp.exp2(s - jnp.max(s, axis=0, keepdims=True)) if use_max else jnp.exp2(s)
        l = jnp.sum(p, axis=0, keepdims=True)
        pair = slice((hh // 2) * LANES, (hh // 2 + 1) * LANES)
        outs.append(_dot(vt_ref[pair, :], _bf16(p)) * (1.0 / l))
    row = lax.broadcasted_iota(jnp.int32, outs[0].shape, 0)
    for pp in range(MLA_HEADS_PER_STEP // 2):
        o_t = jnp.where(row < MLA_V, outs[2 * pp], outs[2 * pp + 1])
        o_ref[:, pp * LANES:(pp + 1) * LANES] = _bf16(o_t.T)


def _mla_call(q, k, vt, *, batch, seq, use_max):
    t = q.shape[0]
    tq = TQ_MLA
    nq = seq // tq
    hps = MLA_HEADS_PER_STEP
    return pl.pallas_call(
        functools.partial(_mla_kernel, use_max=use_max), grid=(batch, MLA_HEADS // hps, nq),
        in_specs=[pl.BlockSpec((tq, hps * HEAD_PAD), lambda b, p, i: (b * nq + i, p)),
                  pl.BlockSpec((seq, hps * HEAD_PAD), lambda b, p, i: (b, p)),
                  pl.BlockSpec((hps * MLA_V, seq), lambda b, p, i: (p, b))],
        out_specs=pl.BlockSpec((tq, hps * MLA_V), lambda b, p, i: (b * nq + i, p)),
        out_shape=jax.ShapeDtypeStruct((t, MLA_WIDTH), jnp.bfloat16),
        compiler_params=_cparams(("parallel", "parallel", "arbitrary")), name="mla_attn",
    )(q, k, vt)


def _na_kernel(q_ref, k_ref, v_ref, bias_ref, o_ref, *, rows, kr, use_max):
    step = pl.program_id(1)
    band = kr * GRID_W
    lane = lax.broadcasted_iota(jnp.int32, (GRID_W, LANES), 1)
    units = []
    for rr in range(NA_ROWS_PER_STEP):
        r = step * NA_ROWS_PER_STEP + rr
        rs = jnp.clip(r - kr // 2, 0, rows - kr)
        cls = r - rs
        start = pl.multiple_of(rs * GRID_W, GRID_W)
        qrow = q_ref[rr * GRID_W:(rr + 1) * GRID_W, :]
        for pr in range(NA_HEADS // 2):
            sl = slice(pr * LANES, (pr + 1) * LANES)
            q2 = qrow[:, sl]
            zero = jnp.zeros_like(q2)
            qq = jnp.concatenate([jnp.where(lane < NA_DH, q2, zero), jnp.where(lane >= NA_DH, q2, zero)], axis=0)
            s = _dot_nt(qq, k_ref[pl.ds(start, band), sl])
            units.append((rr, pr, start, cls, s))
    probs = []
    for rr, pr, start, cls, s in units:
        s = s + bias_ref[cls, 2 * pr:2 * pr + 2].reshape(2 * GRID_W, band)
        p = jnp.exp2(s - jnp.max(s, axis=-1, keepdims=True)) if use_max else jnp.exp2(s)
        l = jnp.sum(p, axis=-1, keepdims=True)
        probs.append((_bf16(p), 1.0 / l))
    for (rr, pr, start, cls, s), (p, rl) in zip(units, probs):
        sl = slice(pr * LANES, (pr + 1) * LANES)
        pv = _dot(p, v_ref[pl.ds(start, band), sl]) * rl
        o_ref[rr * GRID_W:(rr + 1) * GRID_W, sl] = _bf16(jnp.where(lane < NA_DH, pv[:GRID_W], pv[GRID_W:]))


def _na_call(q, k, v, bias, *, l, batch, seq, use_max):
    t = q.shape[0]
    rows = seq // GRID_W
    kr = min(NA_KR_MAX, rows)
    steps = rows // NA_ROWS_PER_STEP
    blk = NA_ROWS_PER_STEP * GRID_W
    return pl.pallas_call(
        functools.partial(_na_kernel, rows=rows, kr=kr, use_max=use_max), grid=(batch, steps),
        in_specs=[pl.BlockSpec((blk, NA_WIDTH), lambda b, i: (b * steps + i, 0)),
                  pl.BlockSpec((seq, NA_WIDTH), lambda b, i: (b, 0)),
                  pl.BlockSpec((seq, NA_WIDTH), lambda b, i: (b, 0)),
                  _layer(bias, l)],
        out_specs=pl.BlockSpec((blk, NA_WIDTH), lambda b, i: (b * steps + i, 0)),
        out_shape=jax.ShapeDtypeStruct((t, NA_WIDTH), jnp.bfloat16),
        compiler_params=_cparams(("parallel", "arbitrary")), name="na_attn",
    )(q, k, v, bias)


def _mix_rows(rows, x_ref, oa_ref, ob_ref, ga_ref, gb_ref, wout_ref, gffn_ref, wr_ref, br_ref, x1_out, h2_out):
    mixed = jnp.concatenate([_rms_rows(oa_ref[rows, :].astype(jnp.float32), ga_ref[...]),
                             _rms_rows(ob_ref[rows, :].astype(jnp.float32), gb_ref[...])], axis=-1)
    x1 = x_ref[rows, :] + _dot(_bf16(mixed), wout_ref[...])
    x1_out[rows, :] = x1
    hi = _bf16(_rms_rows(x1, gffn_ref[...]))
    h2_out[rows, :D_MODEL] = hi
    logits = _dot(hi, wr_ref[...]) + br_ref[...]

    lane = lax.broadcasted_iota(jnp.int32, logits.shape, 1).astype(jnp.float32)
    neg = jnp.float32(-jnp.inf)
    big = jnp.float32(LANES)
    is_grp = lane < N_GROUPS
    lg = jnp.where(is_grp, logits, neg)
    gmax = jnp.max(lg, axis=-1, keepdims=True)
    gid = jnp.min(jnp.where(lg == gmax, lane, big), axis=-1, keepdims=True)
    pg_top = 1.0 / jnp.sum(jnp.where(is_grp, jnp.exp(logits - gmax), 0.0), axis=-1, keepdims=True)
    base = N_GROUPS + EXPERTS_PER_GROUP * gid
    in_sel = (lane >= base) & (lane < base + EXPERTS_PER_GROUP)
    le = jnp.where(in_sel, logits, neg)
    m1 = jnp.max(le, axis=-1, keepdims=True)
    i1 = jnp.min(jnp.where(le == m1, lane, big), axis=-1, keepdims=True)
    le2 = jnp.where(lane == i1, neg, le)
    m2 = jnp.max(le2, axis=-1, keepdims=True)
    i2 = jnp.min(jnp.where(le2 == m2, lane, big), axis=-1, keepdims=True)
    e2 = jnp.exp(m2 - m1)
    w1 = pg_top / (1.0 + e2)
    w2 = pg_top * e2 / (1.0 + e2)
    j1 = i1 - base + ROUTE_W0
    j2 = i2 - base + ROUTE_W0
    w1h = _bf16(w1).astype(jnp.float32)
    w2h = _bf16(w2).astype(jnp.float32)
    rec = jnp.where(lane == ROUTE_GID, gid,
                    jnp.where(lane == j1, w1h, jnp.where(lane == j2, w2h,
                    jnp.where(lane == j1 + ROUTE_LO, w1 - w1h, jnp.where(lane == j2 + ROUTE_LO, w2 - w2h, 0.0)))))
    h2_out[rows, D_MODEL:] = _bf16(rec)


def _mix_kernel(*refs):
    sub = TM_PROJ // MIX_SUBBLOCKS
    for sb in range(MIX_SUBBLOCKS):
        _mix_rows(slice(sb * sub, (sb + 1) * sub), *refs)


def _mix_call(x2d, oa, ob, sw, l):
    t = x2d.shape[0]
    tm = TM_PROJ
    row = lambda i: (i, 0)
    weights = [sw[n] for n in ("g_mla_out", "g_na_out", "w_out", "g_ffn", "w_r", "b_r")]
    return pl.pallas_call(
        _mix_kernel, grid=(t // tm,),
        in_specs=[pl.BlockSpec((tm, D_MODEL), row), pl.BlockSpec((tm, MLA_WIDTH), row),
                  pl.BlockSpec((tm, NA_WIDTH), row)] + [_layer(w, l) for w in weights],
        out_specs=[pl.BlockSpec((tm, D_MODEL), row), pl.BlockSpec((tm, H2_COLS), row)],
        out_shape=[jax.ShapeDtypeStruct((t, D_MODEL), jnp.float32),
                   jax.ShapeDtypeStruct((t, H2_COLS), jnp.bfloat16)],
        compiler_params=_cparams(("parallel",)), name="mix_router",
    )(x2d, oa, ob, *weights)


def _moe_kernel(x1_ref, h2_ref, wg_ref, wu_ref, wd_ref, o_ref, xs_ref, ys_ref):
    tm = TM_MOE
    route = h2_ref[:, D_MODEL:].astype(jnp.float32)
    lane = lax.broadcasted_iota(jnp.int32, route.shape, 1)
    gid = jnp.sum(jnp.where(lane == ROUTE_GID, route, 0.0), axis=-1, keepdims=True)
    onehot = jnp.where((lane.astype(jnp.float32) == gid) & (lane < N_GROUPS), 1.0, 0.0)
    r_i = lax.broadcasted_iota(jnp.int32, (RANK_BLOCK, RANK_BLOCK), 0)
    c_i = lax.broadcasted_iota(jnp.int32, (RANK_BLOCK, RANK_BLOCK), 1)
    tri = jnp.where(c_i < r_i, 1.0, 0.0).astype(jnp.bfloat16)
    ranks = []
    sizes = jnp.zeros((1, LANES), jnp.float32)
    for blk in range(tm // RANK_BLOCK):
        oh = onehot[blk * RANK_BLOCK:(blk + 1) * RANK_BLOCK]
        ranks.append(_dot(tri, _bf16(oh)) + sizes)
        sizes = sizes + jnp.sum(oh, axis=0, keepdims=True)
    rank = jnp.concatenate(ranks, axis=0)
    lane1 = lane[0:1]
    start_vec = jnp.zeros((1, LANES), jnp.float32)
    bounds = []
    acc = jnp.int32(0)
    for g in range(N_GROUPS):
        n = jnp.sum(jnp.where(lane1 == g, sizes, 0.0)).astype(jnp.int32)
        bounds.append((acc, acc + n))
        start_vec = jnp.where(lane1 == g, acc.astype(jnp.float32), start_vec)
        acc = acc + n
    dest = jnp.sum(onehot * (start_vec + rank), axis=-1, keepdims=True)
    dest_row = jnp.transpose(jnp.broadcast_to(dest, (tm, LANES)))[0:1, :]
    rowi = lax.broadcasted_iota(jnp.int32, (tm, tm), 0)
    perm = jnp.where(rowi == dest_row.astype(jnp.int32), 1.0, 0.0).astype(jnp.bfloat16)
    xs_ref[:tm, :] = _bf16(_dot(perm, h2_ref[...]))
    xs_ref[tm:, :] = jnp.zeros((MOE_ROWS - tm, H2_COLS), jnp.bfloat16)
    ys_ref[...] = jnp.zeros_like(ys_ref)

    def _experts(g, off, size, start_g, end_g):
        off = pl.multiple_of(off, BF16_ROWS)
        xc = xs_ref[pl.ds(off, size), :D_MODEL]
        cw = xs_ref[pl.ds(off, size), D_MODEL:].astype(jnp.float32)
        hid = []
        for j in range(EXPERTS_PER_GROUP):
            a = _dot(xc, wg_ref[g * EXPERTS_PER_GROUP + j])
            u = _dot(xc, wu_ref[g * EXPERTS_PER_GROUP + j])
            cj = cw[:, ROUTE_W0 + j:ROUTE_W0 + j + 1] + cw[:, ROUTE_W0 + ROUTE_LO + j:ROUTE_W0 + ROUTE_LO + j + 1]
            hid.append(a * (1.0 / (1.0 + jnp.exp(-a))) * u * cj)
        hid = _bf16(jnp.concatenate(hid, axis=-1))
        w_down = wd_ref[g * EXPERTS_PER_GROUP:(g + 1) * EXPERTS_PER_GROUP].reshape(EXPERTS_PER_GROUP * D_EXPERT, D_MODEL)
        rows = off + lax.broadcasted_iota(jnp.int32, (size, 1), 0)
        mine = (rows >= start_g) & (rows < end_g)
        old = ys_ref[pl.ds(off, size), :].astype(jnp.float32)
        ys_ref[pl.ds(off, size), :] = _bf16(jnp.where(mine, _dot(hid, w_down), old))

    for g, (start_g, end_g) in enumerate(bounds):
        first = lax.shift_left(lax.shift_right_logical(start_g, 4), 4)
        span = end_g - first
        lo = 0
        for size in MOE_SINGLE_CHUNKS:
            @pl.when((span > lo) & (span <= size))
            def _(g=g, size=size, start_g=start_g, end_g=end_g, first=first):
                _experts(g, first, size, start_g, end_g)
            lo = size

        @pl.when(span > lo)
        def _(g=g, start_g=start_g, end_g=end_g, first=first, span=span):
            def _chunk(c, carry):
                _experts(g, first + c * MOE_LOOP_CHUNK, MOE_LOOP_CHUNK, start_g, end_g)
                return carry
            lax.fori_loop(0, lax.div(span + (MOE_LOOP_CHUNK - 1), jnp.int32(MOE_LOOP_CHUNK)), _chunk, 0)

    col = lax.broadcasted_iota(jnp.int32, (tm, tm), 1)
    unperm = jnp.where(col == dest.astype(jnp.int32), 1.0, 0.0).astype(jnp.bfloat16)
    o_ref[...] = x1_ref[...] + _dot(unperm, ys_ref[:tm, :])


def _moe_call(x1, h2, sw, l):
    t = x1.shape[0]
    tm = TM_MOE
    row = lambda i: (i, 0)
    resident = lambda w: _layer(w, l, pipeline_mode=pl.Buffered(1))
    return pl.pallas_call(
        _moe_kernel, grid=(t // tm,),
        in_specs=[pl.BlockSpec((tm, D_MODEL), row), pl.BlockSpec((tm, H2_COLS), row),
                  resident(sw["w_g"]), resident(sw["w_u"]), resident(sw["w_d"])],
        out_specs=pl.BlockSpec((tm, D_MODEL), row),
        out_shape=jax.ShapeDtypeStruct((t, D_MODEL), jnp.float32),
        scratch_shapes=[pltpu.VMEM((MOE_ROWS, H2_COLS), jnp.bfloat16),
                        pltpu.VMEM((MOE_ROWS, D_MODEL), jnp.bfloat16)],
        compiler_params=_cparams(("parallel",)), name="moe",
    )(x1, h2, sw["w_g"], sw["w_u"], sw["w_d"])


def _segment_matrix(width, segments):
    m = np.zeros((width, width), np.float32)
    for lo, hi in segments:
        m[lo:hi, lo:hi] = 1.0 / (hi - lo)
    return jnp.asarray(m, jnp.bfloat16)


def _constants(seq):
    t = np.arange(seq)
    row = (t // GRID_W).astype(np.float32)
    col = (t % GRID_W).astype(np.float32)
    n_freq = MLA_ROPE // 4
    inv = (np.float32(ROPE_THETA) ** (-np.arange(n_freq, dtype=np.float32) / n_freq)).astype(np.float32)
    ang_r = row[:, None] * inv[None, :]
    ang_c = col[:, None] * inv[None, :]

    def tables(base):
        cos = np.zeros((seq, LANES), np.float32)
        cos[:, :base] = 1.0
        sa = np.zeros((seq, LANES), np.float32)
        sb = np.zeros((seq, LANES), np.float32)
        for k, ang in enumerate((ang_r, ang_c)):
            o = base + 2 * n_freq * k
            cos[:, o:o + n_freq] = np.cos(ang)
            cos[:, o + n_freq:o + 2 * n_freq] = np.cos(ang)
            sa[:, o:o + n_freq] = -np.sin(ang)
            sb[:, o + n_freq:o + 2 * n_freq] = np.sin(ang)
        return jnp.asarray(cos), jnp.asarray(sa), jnp.asarray(sb)

    cos_q, sin_qa, sin_qb = tables(MLA_NOPE)
    cos_k, sin_ka, sin_kb = tables(0)
    seg_q = []
    for hb in range(0, MXU_TILE, HEAD_PAD):
        seg_q += [(hb, hb + MLA_NOPE), (hb + MLA_NOPE, hb + MLA_QK)]
    shift_lanes = np.zeros((1, MLA_PAD_WIDTH), np.float32)
    shift_lanes[0, SHIFT_LANE::HEAD_PAD] = 1.0
    return dict(cos_q=cos_q, sin_qa=sin_qa, sin_qb=sin_qb, cos_k=cos_k, sin_ka=sin_ka, sin_kb=sin_kb,
                m_q=_segment_matrix(MXU_TILE, seg_q),
                m_kpe=_segment_matrix(LANES, [(0, MLA_ROPE)]),
                m_na=_segment_matrix(MXU_TILE, [(o, o + NA_DH) for o in range(0, MXU_TILE, NA_DH)]),
                shift_lanes=jnp.asarray(shift_lanes), k_one=jnp.asarray(shift_lanes[:, :HEAD_PAD]))


def _na_bias_tables(rpb, rows, shift):
    kr = min(NA_KR_MAX, rows)
    cols = np.arange(GRID_W)
    cs = np.clip(cols - NA_KC // 2, 0, GRID_W - NA_KC)
    col_mask = (cols[None, :] >= cs[:, None]) & (cols[None, :] < cs[:, None] + NA_KC)
    dc = np.clip(cols[None, :] - cols[:, None], -(NA_KC - 1), NA_KC - 1) + NA_KC - 1
    onehot = (dc[None] == np.arange(2 * NA_KC - 1)[:, None, None]).astype(np.float32)
    rpb_c = jnp.einsum("lhdj,jqk->lhdqk", rpb, jnp.asarray(onehot), precision=lax.Precision.HIGHEST)
    rpb_c = jnp.where(jnp.asarray(col_mask), rpb_c * LOG2E - shift[:, None, None, None, None], MASK_NEG)
    band_rows = (NA_KR_MAX - 1 - np.arange(kr))[:, None] + np.arange(kr)[None, :]
    b = rpb_c[:, :, band_rows]
    return b.transpose(0, 2, 1, 4, 3, 5).reshape(rpb.shape[0], kr, NA_HEADS, GRID_W, kr * GRID_W)


def _softmax_shifts(p):
    amax = lambda v: jnp.max(jnp.abs(v), axis=-1)
    gq, gk = p["g_mla_q"], p["g_mla_k"]
    qn = (MLA_QK ** -0.5 * LOG2E) * jnp.sqrt(MLA_NOPE * amax(gq[:, :MLA_NOPE]) ** 2 + MLA_ROPE * amax(gq[:, MLA_NOPE:]) ** 2)
    kn = jnp.sqrt(MLA_NOPE * amax(gk[:, :MLA_NOPE]) ** 2 + MLA_ROPE * amax(gk[:, MLA_NOPE:]) ** 2)
    b_mla = BOUND_SLACK * qn * kn
    mla_fast = 2.0 * b_mla <= MAX_SHIFT_GAP
    b_na = BOUND_SLACK * (NA_DH ** -0.5 * LOG2E) * NA_DH * amax(p["g_na_q"]) * amax(p["g_na_k"])
    bias_hi = jnp.max(p["na_rpb"], axis=(1, 2, 3)) * LOG2E
    bias_lo = jnp.min(p["na_rpb"], axis=(1, 2, 3)) * LOG2E
    na_fast = 2.0 * b_na + (bias_hi - bias_lo) <= MAX_SHIFT_GAP
    return (mla_fast, jnp.where(mla_fast, b_mla, 0.0)), (na_fast, jnp.where(na_fast, b_na + bias_hi, 0.0))


def _prep_weights(p, rows, consts):
    bf = jnp.bfloat16
    nl = p["w_in"].shape[0]
    (mla_fast, mla_shift), (na_fast, na_shift) = _softmax_shifts(p)
    w_in = p["w_in"]
    split = Q_LORA + KV_LORA + MLA_ROPE
    zeros = jnp.zeros((nl, D_MODEL, LANES - MLA_ROPE), w_in.dtype)
    w_in_p = jnp.concatenate([w_in[:, :, :split], zeros, w_in[:, :, split:]], axis=2)
    w_qb = jnp.pad(p["w_q_b"].reshape(nl, Q_LORA, MLA_HEADS, MLA_QK), ((0, 0), (0, 0), (0, 0), (0, HEAD_PAD - MLA_QK)))
    w_kv = p["w_kv_b"].reshape(nl, KV_LORA, MLA_HEADS, MLA_NOPE + MLA_V)
    w_kc = w_kv[..., :MLA_NOPE].reshape(nl, KV_LORA, MLA_HEADS * MLA_NOPE)
    w_vt = w_kv[..., MLA_NOPE:].reshape(nl, KV_LORA, MLA_WIDTH).transpose(0, 2, 1)
    g_q = jnp.tile(jnp.pad(p["g_mla_q"] * (MLA_QK ** -0.5 * LOG2E), ((0, 0), (0, HEAD_PAD - MLA_QK))), (1, MLA_HEADS))
    w_r = jnp.concatenate([p["w_router_group"], p["w_router_expert"]], axis=2)
    w_r = jnp.pad(w_r, ((0, 0), (0, 0), (0, LANES - w_r.shape[2])))
    b_r = jnp.concatenate([p["b_router_group"], p["b_router_expert"]], axis=1)
    b_r = jnp.pad(b_r, ((0, 0), (0, LANES - b_r.shape[1])))
    row = lambda v: v[:, None, :]
    return dict(
        g_mix=row(p["g_mix_norm"]), w_in=w_in_p.astype(bf),
        g_q_a=row(p["g_q_a"]), w_qb=w_qb.reshape(nl, Q_LORA, MLA_PAD_WIDTH).astype(bf),
        g_kv_a=row(p["g_kv_a"]), w_kc=w_kc.astype(bf), w_vt=w_vt.astype(bf),
        g_q=row(g_q), g_kc=row(jnp.tile(p["g_mla_k"][:, :MLA_NOPE], (1, MLA_HEADS))),
        g_kpe=row(jnp.pad(p["g_mla_k"][:, MLA_NOPE:], ((0, 0), (0, LANES - MLA_ROPE)))),
        g_naq=row(jnp.tile(p["g_na_q"] * (NA_DH ** -0.5 * LOG2E), (1, NA_HEADS))),
        g_nak=row(jnp.tile(p["g_na_k"], (1, NA_HEADS))),
        na_bias=_na_bias_tables(p["na_rpb"], rows, na_shift), na_fast=na_fast,
        q_shift=-mla_shift[:, None, None] * consts["shift_lanes"][None], mla_fast=mla_fast,
        g_mla_out=row(p["g_mla_out"]), g_na_out=row(p["g_na_out"]),
        w_out=p["w_out"].astype(bf), g_ffn=row(p["g_ffn_norm"]),
        w_r=w_r.astype(bf), b_r=row(b_r),
        w_g=p["w_gate"].astype(bf), w_u=p["w_up"].astype(bf), w_d=p["w_down"].astype(bf))


def kernel(x, g_mix_norm, w_in, g_q_a, w_q_b, g_kv_a, w_kv_b, g_mla_q, g_mla_k, g_na_q, g_na_k, na_rpb, g_mla_out, g_na_out, w_out, g_ffn_norm, w_router_group, b_router_group, w_router_expert, b_router_expert, w_gate, w_up, w_down):
    batch, seq, d = x.shape
    assert d == D_MODEL and seq % TM_PROJ == 0 and seq % TQ_MLA == 0 and (batch * seq) % TM_MOE == 0
    rows = seq // GRID_W
    assert rows % NA_ROWS_PER_STEP == 0 and rows >= NA_KR_MAX
    p = dict(g_mix_norm=g_mix_norm, w_in=w_in, g_q_a=g_q_a, w_q_b=w_q_b, g_kv_a=g_kv_a, w_kv_b=w_kv_b,
             g_mla_q=g_mla_q, g_mla_k=g_mla_k, g_na_q=g_na_q, g_na_k=g_na_k, na_rpb=na_rpb,
             g_mla_out=g_mla_out, g_na_out=g_na_out, w_out=w_out, g_ffn_norm=g_ffn_norm,
             w_router_group=w_router_group, b_router_group=b_router_group,
             w_router_expert=w_router_expert, b_router_expert=b_router_expert,
             w_gate=w_gate, w_up=w_up, w_down=w_down)
    consts = _constants(seq)
    sw = _prep_weights(p, rows, consts)
    xf = x.reshape(batch * seq, d)
    for l in range(w_in.shape[0]):
        q, k, vt, naq, nak, nav = _proj_call(xf, sw, l, consts, seq)
        o_a = lax.cond(sw["mla_fast"][l],
                       functools.partial(_mla_call, batch=batch, seq=seq, use_max=False),
                       functools.partial(_mla_call, batch=batch, seq=seq, use_max=True), q, k, vt)
        o_b = lax.cond(sw["na_fast"][l],
                       functools.partial(_na_call, l=l, batch=batch, seq=seq, use_max=False),
                       functools.partial(_na_call, l=l, batch=batch, seq=seq, use_max=True), naq, nak, nav, sw["na_bias"])
        x1, h2 = _mix_call(xf, o_a, o_b, sw, l)
        xf = _moe_call(x1, h2, sw, l)
    return xf.reshape(batch, seq, d)
```

```python
import functools

import numpy as np
import jax
import jax.numpy as jnp
from jax import lax
from jax.experimental import pallas as pl
from jax.experimental.pallas import tpu as pltpu

D_MODEL = 1024
GRID_W = 64
MLA_HEADS = 8
MLA_NOPE = 64
MLA_ROPE = 32
MLA_V = 64
MLA_QK = MLA_NOPE + MLA_ROPE
Q_LORA = 256
KV_LORA = 128
MLA_WIDTH = MLA_HEADS * MLA_V
ROPE_THETA = 10000.0
NA_HEADS = 8
NA_DH = 64
NA_WIDTH = NA_HEADS * NA_DH
NA_KR_MAX = 8
NA_KC = 16
N_GROUPS = 4
EXPERTS_PER_GROUP = 4
D_EXPERT = 256
EPS = 1e-6

LANES = 128
MXU_TILE = 256
BF16_ROWS = 16
HEAD_PAD = LANES
MLA_PAD_WIDTH = MLA_HEADS * HEAD_PAD
PROJ_COLS = 2048
LOG2E = 1.4426950408889634
MASK_NEG = -1e30
SHIFT_LANE = MLA_QK
MAX_SHIFT_GAP = 100.0
BOUND_SLACK = 1.02
VMEM_LIMIT = 56 * 1024 * 1024

C_CQ = 0
C_CKV = C_CQ + Q_LORA
C_KPE = C_CKV + KV_LORA
C_NAQ = C_KPE + LANES
C_NAK = C_NAQ + NA_WIDTH
C_NAV = C_NAK + NA_WIDTH

TM_PROJ = 1024
PROJ_SUBBLOCKS = 4
MIX_SUBBLOCKS = 4
TQ_MLA = 512
MLA_HEADS_PER_STEP = 4
NA_ROWS_PER_STEP = 16
TM_MOE = 512
MOE_SUBTILES = 2
MOE_SINGLE_CHUNKS = (128, 176, 224)
MOE_LOOP_CHUNK = 256
MOE_ROWS = TM_MOE + MOE_LOOP_CHUNK
RANK_BLOCK = 256
ROUTE_GID = 0
ROUTE_W0 = 1
ROUTE_LO = 8
H2_COLS = D_MODEL + LANES

assert N_GROUPS == 4
assert MLA_NOPE == NA_DH


def _cparams(sem):
    return pltpu.CompilerParams(dimension_semantics=sem, vmem_limit_bytes=VMEM_LIMIT)


def _full(shape):
    nd = len(shape)
    return pl.BlockSpec(shape, lambda *_: (0,) * nd)


def _layer(w, l, **kw):
    nd = w.ndim
    return pl.BlockSpec((None,) + w.shape[1:], lambda *_: (l,) + (0,) * (nd - 1), **kw)


def _bf16(x):
    return x.astype(jnp.bfloat16)


def _dot(a, b):
    return jnp.dot(a, b, preferred_element_type=jnp.float32)


def _dot_nt(a, b):
    return lax.dot_general(a, b, (((1,), (1,)), ((), ())), preferred_element_type=jnp.float32)


def _rms_rows(x, gain):
    ms = jnp.mean(x * x, axis=-1, keepdims=True)
    return x * lax.rsqrt(ms + EPS) * gain


def _segment_rms_scale(x, seg_mat):
    n = x.shape[-1]
    parts = []
    for j in range(0, n, seg_mat.shape[0]):
        w = min(seg_mat.shape[0], n - j)
        xs = x[:, j:j + w]
        parts.append(_dot(_bf16(xs * xs), seg_mat[:w, :w]))
    ms = parts[0] if len(parts) == 1 else jnp.concatenate(parts, axis=-1)
    return lax.rsqrt(ms + EPS)


def _rope_block(x, cos, sin_a, sin_b):
    return x * cos + pltpu.roll(x, LANES - 8, 1) * sin_a + pltpu.roll(x, 8, 1) * sin_b


def _proj_rows(rows, x_ref, gmix_ref, win_ref, gqa_ref, wqb_ref, gkva_ref, wkc_ref, wvt_ref,
               gq_ref, gkc_ref, gkpe_ref, gnaq_ref, gnak_ref,
               mq_ref, mkpe_ref, mna_ref, qshift_ref, kone_ref,
               cq_ref, sqa_ref, sqb_ref, ck_ref, ska_ref, skb_ref,
               q_out, k_out, vt_out, naq_out, nak_out, nav_out):
    h = _bf16(_rms_rows(x_ref[rows, :], gmix_ref[...]))
    proj = _dot(h, win_ref[...])

    cqn = _bf16(_rms_rows(proj[:, C_CQ:C_CKV], gqa_ref[...]))
    q = _dot(cqn, wqb_ref[...])
    q = q * _segment_rms_scale(q, mq_ref[...]) * gq_ref[...]
    cosq, sqa, sqb = cq_ref[rows, :], sqa_ref[rows, :], sqb_ref[rows, :]
    for hd in range(MLA_HEADS):
        sl = slice(hd * HEAD_PAD, (hd + 1) * HEAD_PAD)
        q_out[rows, sl] = _bf16(_rope_block(q[:, sl], cosq, sqa, sqb) + qshift_ref[:, sl])

    ckvn = _bf16(_rms_rows(proj[:, C_CKV:C_KPE], gkva_ref[...]))
    kc = _dot(ckvn, wkc_ref[...])
    kc = kc * _segment_rms_scale(kc, mna_ref[...]) * gkc_ref[...]
    vt_out[:, rows] = _bf16(_dot_nt(wvt_ref[...], ckvn))
    kp = proj[:, C_KPE:C_NAQ]
    kp = kp * lax.rsqrt(_dot(_bf16(kp * kp), mkpe_ref[...]) + EPS) * gkpe_ref[...]
    kp = _rope_block(kp, ck_ref[rows, :], ska_ref[rows, :], skb_ref[rows, :])
    tail = pltpu.roll(kp, MLA_NOPE, 1) + kone_ref[...]
    lane = lax.broadcasted_iota(jnp.int32, tail.shape, 1)
    for pr in range(MLA_HEADS // 2):
        two = kc[:, pr * LANES:(pr + 1) * LANES]
        k_out[rows, (2 * pr) * HEAD_PAD:(2 * pr + 1) * HEAD_PAD] = _bf16(jnp.where(lane < MLA_NOPE, two, tail))
        k_out[rows, (2 * pr + 1) * HEAD_PAD:(2 * pr + 2) * HEAD_PAD] = _bf16(
            jnp.where(lane < MLA_NOPE, pltpu.roll(two, MLA_NOPE, 1), tail))

    naq = proj[:, C_NAQ:C_NAK]
    naq_out[rows, :] = _bf16(naq * _segment_rms_scale(naq, mna_ref[...]) * gnaq_ref[...])
    nak = proj[:, C_NAK:C_NAV]
    nak_out[rows, :] = _bf16(nak * _segment_rms_scale(nak, mna_ref[...]) * gnak_ref[...])
    nav_out[rows, :] = _bf16(proj[:, C_NAV:PROJ_COLS])


def _proj_kernel(*refs):
    sub = TM_PROJ // PROJ_SUBBLOCKS
    for sb in range(PROJ_SUBBLOCKS):
        _proj_rows(slice(sb * sub, (sb + 1) * sub), *refs)


def _proj_call(x2d, sw, l, consts, seq):
    t = x2d.shape[0]
    tm = TM_PROJ
    n_seq_tiles = seq // tm
    row = lambda i: (i, 0)
    tab = lambda i: (i % n_seq_tiles, 0)
    layered = ("g_mix", "w_in", "g_q_a", "w_qb", "g_kv_a", "w_kc", "w_vt", "g_q", "g_kc", "g_kpe", "g_naq", "g_nak")
    weights = [sw[n] for n in layered] + [consts["m_q"], consts["m_kpe"], consts["m_na"], sw["q_shift"], consts["k_one"]]
    specs = ([_layer(sw[n], l) for n in layered] + [_full(consts[n].shape) for n in ("m_q", "m_kpe", "m_na")]
             + [_layer(sw["q_shift"], l), _full(consts["k_one"].shape)])
    tables = [consts["cos_q"], consts["sin_qa"], consts["sin_qb"], consts["cos_k"], consts["sin_ka"], consts["sin_kb"]]
    in_specs = [pl.BlockSpec((tm, D_MODEL), row)] + specs + [pl.BlockSpec((tm, LANES), tab) for _ in tables]
    out_shapes = [jax.ShapeDtypeStruct((t, MLA_PAD_WIDTH), jnp.bfloat16),
                  jax.ShapeDtypeStruct((t, MLA_PAD_WIDTH), jnp.bfloat16),
                  jax.ShapeDtypeStruct((MLA_WIDTH, t), jnp.bfloat16),
                  jax.ShapeDtypeStruct((t, NA_WIDTH), jnp.bfloat16),
                  jax.ShapeDtypeStruct((t, NA_WIDTH), jnp.bfloat16),
                  jax.ShapeDtypeStruct((t, NA_WIDTH), jnp.bfloat16)]
    out_specs = [pl.BlockSpec((MLA_WIDTH, tm), lambda i: (0, i)) if s.shape[0] == MLA_WIDTH
                 else pl.BlockSpec((tm, s.shape[1]), row) for s in out_shapes]
    return pl.pallas_call(
        _proj_kernel, grid=(t // tm,), in_specs=in_specs, out_specs=out_specs, out_shape=out_shapes,
        compiler_params=_cparams(("parallel",)), name="proj",
    )(x2d, *weights, *tables)


def _mla_kernel(q_ref, k_ref, vt_ref, o_ref, *, use_max):
    scores = []
    for hh in range(MLA_HEADS_PER_STEP):
        sl = slice(hh * HEAD_PAD, (hh + 1) * HEAD_PAD)
        scores.append(_dot_nt(k_ref[:, sl], q_ref[:, sl]))
    outs = []
    for hh, s in enumerate(scores):
        p = jnp.exp2(s - jnp.max(s, axis=0, keepdims=True)) if use_max else jnp.exp2(s)
        l = jnp.sum(p, axis=0, keepdims=True)
        pair = slice((hh // 2) * LANES, (hh // 2 + 1) * LANES)
        outs.append(_dot(vt_ref[pair, :], _bf16(p)) * (1.0 / l))
    row = lax.broadcasted_iota(jnp.int32, outs[0].shape, 0)
    for pp in range(MLA_HEADS_PER_STEP // 2):
        o_t = jnp.where(row < MLA_V, outs[2 * pp], outs[2 * pp + 1])
        o_ref[:, pp * LANES:(pp + 1) * LANES] = _bf16(o_t.T)


def _mla_call(q, k, vt, *, batch, seq, use_max):
    t = q.shape[0]
    tq = TQ_MLA
    nq = seq // tq
    hps = MLA_HEADS_PER_STEP
    return pl.pallas_call(
        functools.partial(_mla_kernel, use_max=use_max), grid=(batch, MLA_HEADS // hps, nq),
        in_specs=[pl.BlockSpec((tq, hps * HEAD_PAD), lambda b, p, i: (b * nq + i, p)),
                  pl.BlockSpec((seq, hps * HEAD_PAD), lambda b, p, i: (b, p)),
                  pl.BlockSpec((hps * MLA_V, seq), lambda b, p, i: (p, b))],
        out_specs=pl.BlockSpec((tq, hps * MLA_V), lambda b, p, i: (b * nq + i, p)),
        out_shape=jax.ShapeDtypeStruct((t, MLA_WIDTH), jnp.bfloat16),
        compiler_params=_cparams(("parallel", "parallel", "arbitrary")), name="mla_attn",
    )(q, k, vt)


def _na_kernel(q_ref, k_ref, v_ref, bias_ref, o_ref, *, rows, kr, use_max):
    step = pl.program_id(1)
    band = kr * GRID_W
    lane = lax.broadcasted_iota(jnp.int32, (GRID_W, LANES), 1)
    units = []
    for rr in range(NA_ROWS_PER_STEP):
        r = step * NA_ROWS_PER_STEP + rr
        rs = jnp.clip(r - kr // 2, 0, rows - kr)
        cls = r - rs
        start = pl.multiple_of(rs * GRID_W, GRID_W)
        qrow = q_ref[rr * GRID_W:(rr + 1) * GRID_W, :]
        for pr in range(NA_HEADS // 2):
            sl = slice(pr * LANES, (pr + 1) * LANES)
            q2 = qrow[:, sl]
            zero = jnp.zeros_like(q2)
            qq = jnp.concatenate([jnp.where(lane < NA_DH, q2, zero), jnp.where(lane >= NA_DH, q2, zero)], axis=0)
            s = _dot_nt(qq, k_ref[pl.ds(start, band), sl])
            units.append((rr, pr, start, cls, s))
    probs = []
    for rr, pr, start, cls, s in units:
        s = s + bias_ref[cls, 2 * pr:2 * pr + 2].reshape(2 * GRID_W, band)
        p = jnp.exp2(s - jnp.max(s, axis=-1, keepdims=True)) if use_max else jnp.exp2(s)
        l = jnp.sum(p, axis=-1, keepdims=True)
        probs.append((_bf16(p), 1.0 / l))
    for (rr, pr, start, cls, s), (p, rl) in zip(units, probs):
        sl = slice(pr * LANES, (pr + 1) * LANES)
        pv = _dot(p, v_ref[pl.ds(start, band), sl]) * rl
        o_ref[rr * GRID_W:(rr + 1) * GRID_W, sl] = _bf16(jnp.where(lane < NA_DH, pv[:GRID_W], pv[GRID_W:]))


def _na_call(q, k, v, bias, *, l, batch, seq, use_max):
    t = q.shape[0]
    rows = seq // GRID_W
    kr = min(NA_KR_MAX, rows)
    steps = rows // NA_ROWS_PER_STEP
    blk = NA_ROWS_PER_STEP * GRID_W
    return pl.pallas_call(
        functools.partial(_na_kernel, rows=rows, kr=kr, use_max=use_max), grid=(batch, steps),
        in_specs=[pl.BlockSpec((blk, NA_WIDTH), lambda b, i: (b * steps + i, 0)),
                  pl.BlockSpec((seq, NA_WIDTH), lambda b, i: (b, 0)),
                  pl.BlockSpec((seq, NA_WIDTH), lambda b, i: (b, 0)),
                  _layer(bias, l)],
        out_specs=pl.BlockSpec((blk, NA_WIDTH), lambda b, i: (b * steps + i, 0)),
        out_shape=jax.ShapeDtypeStruct((t, NA_WIDTH), jnp.bfloat16),
        compiler_params=_cparams(("parallel", "arbitrary")), name="na_attn",
    )(q, k, v, bias)


def _mix_rows(rows, x_ref, oa_ref, ob_ref, ga_ref, gb_ref, wout_ref, gffn_ref, wr_ref, br_ref, x1_out, h2_out):
    mixed = jnp.concatenate([_rms_rows(oa_ref[rows, :].astype(jnp.float32), ga_ref[...]),
                             _rms_rows(ob_ref[rows, :].astype(jnp.float32), gb_ref[...])], axis=-1)
    x1 = x_ref[rows, :] + _dot(_bf16(mixed), wout_ref[...])
    x1_out[rows, :] = x1
    hi = _bf16(_rms_rows(x1, gffn_ref[...]))
    h2_out[rows, :D_MODEL] = hi
    logits = _dot(hi, wr_ref[...]) + br_ref[...]

    lane = lax.broadcasted_iota(jnp.int32, logits.shape, 1).astype(jnp.float32)
    neg = jnp.float32(-jnp.inf)
    big = jnp.float32(LANES)
    is_grp = lane < N_GROUPS
    lg = jnp.where(is_grp, logits, neg)
    gmax = jnp.max(lg, axis=-1, keepdims=True)
    gid = jnp.min(jnp.where(lg == gmax, lane, big), axis=-1, keepdims=True)
    pg_top = 1.0 / jnp.sum(jnp.where(is_grp, jnp.exp(logits - gmax), 0.0), axis=-1, keepdims=True)
    base = N_GROUPS + EXPERTS_PER_GROUP * gid
    in_sel = (lane >= base) & (lane < base + EXPERTS_PER_GROUP)
    le = jnp.where(in_sel, logits, neg)
    m1 = jnp.max(le, axis=-1, keepdims=True)
    i1 = jnp.min(jnp.where(le == m1, lane, big), axis=-1, keepdims=True)
    le2 = jnp.where(lane == i1, neg, le)
    m2 = jnp.max(le2, axis=-1, keepdims=True)
    i2 = jnp.min(jnp.where(le2 == m2, lane, big), axis=-1, keepdims=True)
    e2 = jnp.exp(m2 - m1)
    w1 = pg_top / (1.0 + e2)
    w2 = pg_top * e2 / (1.0 + e2)
    j1 = i1 - base + ROUTE_W0
    j2 = i2 - base + ROUTE_W0
    w1h = _bf16(w1).astype(jnp.float32)
    w2h = _bf16(w2).astype(jnp.float32)
    rec = jnp.where(lane == ROUTE_GID, gid,
                    jnp.where(lane == j1, w1h, jnp.where(lane == j2, w2h,
                    jnp.where(lane == j1 + ROUTE_LO, w1 - w1h, jnp.where(lane == j2 + ROUTE_LO, w2 - w2h, 0.0)))))
    h2_out[rows, D_MODEL:] = _bf16(rec)


def _mix_kernel(*refs):
    sub = TM_PROJ // MIX_SUBBLOCKS
    for sb in range(MIX_SUBBLOCKS):
        _mix_rows(slice(sb * sub, (sb + 1) * sub), *refs)


def _mix_call(x2d, oa, ob, sw, l):
    t = x2d.shape[0]
    tm = TM_PROJ
    row = lambda i: (i, 0)
    weights = [sw[n] for n in ("g_mla_out", "g_na_out", "w_out", "g_ffn", "w_r", "b_r")]
    return pl.pallas_call(
        _mix_kernel, grid=(t // tm,),
        in_specs=[pl.BlockSpec((tm, D_MODEL), row), pl.BlockSpec((tm, MLA_WIDTH), row),
                  pl.BlockSpec((tm, NA_WIDTH), row)] + [_layer(w, l) for w in weights],
        out_specs=[pl.BlockSpec((tm, D_MODEL), row), pl.BlockSpec((tm, H2_COLS), row)],
        out_shape=[jax.ShapeDtypeStruct((t, D_MODEL), jnp.float32),
                   jax.ShapeDtypeStruct((t, H2_COLS), jnp.bfloat16)],
        compiler_params=_cparams(("parallel",)), name="mix_router",
    )(x2d, oa, ob, *weights)


def _moe_partition(sub, h2_ref, xs_ref, ys_ref, cnt_ref):
    tm = TM_MOE
    rows = slice(sub * tm, (sub + 1) * tm)
    route = h2_ref[rows, D_MODEL:].astype(jnp.float32)
    lane = lax.broadcasted_iota(jnp.int32, route.shape, 1)
    gid = jnp.sum(jnp.where(lane == ROUTE_GID, route, 0.0), axis=-1, keepdims=True)
    onehot = jnp.where((lane.astype(jnp.float32) == gid) & (lane < N_GROUPS), 1.0, 0.0)
    r_i = lax.broadcasted_iota(jnp.int32, (RANK_BLOCK, RANK_BLOCK), 0)
    c_i = lax.broadcasted_iota(jnp.int32, (RANK_BLOCK, RANK_BLOCK), 1)
    tri = jnp.where(c_i < r_i, 1.0, 0.0).astype(jnp.bfloat16)
    ranks = []
    sizes = jnp.zeros((1, LANES), jnp.float32)
    for blk in range(tm // RANK_BLOCK):
        oh = onehot[blk * RANK_BLOCK:(blk + 1) * RANK_BLOCK]
        ranks.append(_dot(tri, _bf16(oh)) + sizes)
        sizes = sizes + jnp.sum(oh, axis=0, keepdims=True)
    rank = jnp.concatenate(ranks, axis=0)
    lane1 = lane[0:1]
    start_vec = jnp.zeros((1, LANES), jnp.float32)
    acc = jnp.int32(0)
    for g in range(N_GROUPS):
        n = jnp.sum(jnp.where(lane1 == g, sizes, 0.0)).astype(jnp.int32)
        cnt_ref[sub * N_GROUPS + g] = acc
        cnt_ref[MOE_SUBTILES * N_GROUPS + sub * N_GROUPS + g] = acc + n
        start_vec = jnp.where(lane1 == g, acc.astype(jnp.float32), start_vec)
        acc = acc + n
    dest = jnp.sum(onehot * (start_vec + rank), axis=-1, keepdims=True)
    dest_row = jnp.transpose(jnp.broadcast_to(dest, (tm, LANES)))[0:1, :]
    rowi = lax.broadcasted_iota(jnp.int32, (tm, tm), 0)
    perm = jnp.where(rowi == dest_row.astype(jnp.int32), 1.0, 0.0).astype(jnp.bfloat16)
    xs_ref[sub, :tm, :] = _bf16(_dot(perm, h2_ref[rows, :]))
    xs_ref[sub, tm:, :] = jnp.zeros((MOE_ROWS - tm, H2_COLS), jnp.bfloat16)
    ys_ref[sub] = jnp.zeros((MOE_ROWS, D_MODEL), jnp.bfloat16)
    return dest


def _moe_experts(sub, g, off, size, start_g, end_g, xs_ref, ys_ref, wg_ref, wu_ref, wd_ref):
    off = pl.multiple_of(off, BF16_ROWS)
    xc = xs_ref[sub, pl.ds(off, size), :D_MODEL]
    cw = xs_ref[sub, pl.ds(off, size), D_MODEL:].astype(jnp.float32)
    hid = []
    for j in range(EXPERTS_PER_GROUP):
        a = _dot(xc, wg_ref[g * EXPERTS_PER_GROUP + j])
        u = _dot(xc, wu_ref[g * EXPERTS_PER_GROUP + j])
        cj = cw[:, ROUTE_W0 + j:ROUTE_W0 + j + 1] + cw[:, ROUTE_W0 + ROUTE_LO + j:ROUTE_W0 + ROUTE_LO + j + 1]
        hid.append(a * (1.0 / (1.0 + jnp.exp(-a))) * u * cj)
    hid = _bf16(jnp.concatenate(hid, axis=-1))
    w_down = wd_ref[pl.ds(g * EXPERTS_PER_GROUP, EXPERTS_PER_GROUP)].reshape(EXPERTS_PER_GROUP * D_EXPERT, D_MODEL)
    rows = off + lax.broadcasted_iota(jnp.int32, (size, 1), 0)
    mine = (rows >= start_g) & (rows < end_g)
    old = ys_ref[sub, pl.ds(off, size), :].astype(jnp.float32)
    ys_ref[sub, pl.ds(off, size), :] = _bf16(jnp.where(mine, _dot(hid, w_down), old))


def _moe_kernel(x1_ref, h2_ref, wg_ref, wu_ref, wd_ref, o_ref, xs_ref, ys_ref, cnt_ref):
    tm = TM_MOE
    dests = [_moe_partition(sub, h2_ref, xs_ref, ys_ref, cnt_ref) for sub in range(MOE_SUBTILES)]

    def _run(u, carry):
        sub = lax.shift_right_logical(u, 2)
        g = lax.bitwise_and(u, N_GROUPS - 1)
        start_g = cnt_ref[u]
        end_g = cnt_ref[MOE_SUBTILES * N_GROUPS + u]
        first = lax.shift_left(lax.shift_right_logical(start_g, 4), 4)
        span = end_g - first
        args = (xs_ref, ys_ref, wg_ref, wu_ref, wd_ref)
        lo = 0
        for size in MOE_SINGLE_CHUNKS:
            @pl.when((span > lo) & (span <= size))
            def _(size=size):
                _moe_experts(sub, g, first, size, start_g, end_g, *args)
            lo = size

        @pl.when(span > lo)
        def _():
            def _chunk(c, inner):
                _moe_experts(sub, g, first + c * MOE_LOOP_CHUNK, MOE_LOOP_CHUNK, start_g, end_g, *args)
                return inner
            lax.fori_loop(0, lax.div(span + (MOE_LOOP_CHUNK - 1), jnp.int32(MOE_LOOP_CHUNK)), _chunk, 0)
        return carry

    lax.fori_loop(0, MOE_SUBTILES * N_GROUPS, _run, 0)

    col = lax.broadcasted_iota(jnp.int32, (tm, tm), 1)
    for sub, dest in enumerate(dests):
        rows = slice(sub * tm, (sub + 1) * tm)
        unperm = jnp.where(col == dest.astype(jnp.int32), 1.0, 0.0).astype(jnp.bfloat16)
        o_ref[rows, :] = x1_ref[rows, :] + _dot(unperm, ys_ref[sub, :tm, :])


def _moe_call(x1, h2, sw, l):
    t = x1.shape[0]
    tm = TM_MOE * MOE_SUBTILES
    row = lambda i: (i, 0)
    resident = lambda w: _layer(w, l, pipeline_mode=pl.Buffered(1))
    return pl.pallas_call(
        _moe_kernel, grid=(t // tm,),
        in_specs=[pl.BlockSpec((tm, D_MODEL), row), pl.BlockSpec((tm, H2_COLS), row),
                  resident(sw["w_g"]), resident(sw["w_u"]), resident(sw["w_d"])],
        out_specs=pl.BlockSpec((tm, D_MODEL), row),
        out_shape=jax.ShapeDtypeStruct((t, D_MODEL), jnp.float32),
        scratch_shapes=[pltpu.VMEM((MOE_SUBTILES, MOE_ROWS, H2_COLS), jnp.bfloat16),
                        pltpu.VMEM((MOE_SUBTILES, MOE_ROWS, D_MODEL), jnp.bfloat16),
                        pltpu.SMEM((2 * MOE_SUBTILES * N_GROUPS,), jnp.int32)],
        compiler_params=_cparams(("parallel",)), name="moe",
    )(x1, h2, sw["w_g"], sw["w_u"], sw["w_d"])


def _segment_matrix(width, segments):
    m = np.zeros((width, width), np.float32)
    for lo, hi in segments:
        m[lo:hi, lo:hi] = 1.0 / (hi - lo)
    return jnp.asarray(m, jnp.bfloat16)


def _constants(seq):
    t = np.arange(seq)
    row = (t // GRID_W).astype(np.float32)
    col = (t % GRID_W).astype(np.float32)
    n_freq = MLA_ROPE // 4
    inv = (np.float32(ROPE_THETA) ** (-np.arange(n_freq, dtype=np.float32) / n_freq)).astype(np.float32)
    ang_r = row[:, None] * inv[None, :]
    ang_c = col[:, None] * inv[None, :]

    def tables(base):
        cos = np.zeros((seq, LANES), np.float32)
        cos[:, :base] = 1.0
        sa = np.zeros((seq, LANES), np.float32)
        sb = np.zeros((seq, LANES), np.float32)
        for k, ang in enumerate((ang_r, ang_c)):
            o = base + 2 * n_freq * k
            cos[:, o:o + n_freq] = np.cos(ang)
            cos[:, o + n_freq:o + 2 * n_freq] = np.cos(ang)
            sa[:, o:o + n_freq] = -np.sin(ang)
            sb[:, o + n_freq:o + 2 * n_freq] = np.sin(ang)
        return jnp.asarray(cos), jnp.asarray(sa), jnp.asarray(sb)

    cos_q, sin_qa, sin_qb = tables(MLA_NOPE)
    cos_k, sin_ka, sin_kb = tables(0)
    seg_q = []
    for hb in range(0, MXU_TILE, HEAD_PAD):
        seg_q += [(hb, hb + MLA_NOPE), (hb + MLA_NOPE, hb + MLA_QK)]
    shift_lanes = np.zeros((1, MLA_PAD_WIDTH), np.float32)
    shift_lanes[0, SHIFT_LANE::HEAD_PAD] = 1.0
    return dict(cos_q=cos_q, sin_qa=sin_qa, sin_qb=sin_qb, cos_k=cos_k, sin_ka=sin_ka, sin_kb=sin_kb,
                m_q=_segment_matrix(MXU_TILE, seg_q),
                m_kpe=_segment_matrix(LANES, [(0, MLA_ROPE)]),
                m_na=_segment_matrix(MXU_TILE, [(o, o + NA_DH) for o in range(0, MXU_TILE, NA_DH)]),
                shift_lanes=jnp.asarray(shift_lanes), k_one=jnp.asarray(shift_lanes[:, :HEAD_PAD]))


def _na_bias_tables(rpb, rows, shift):
    kr = min(NA_KR_MAX, rows)
    cols = np.arange(GRID_W)
    cs = np.clip(cols - NA_KC // 2, 0, GRID_W - NA_KC)
    col_mask = (cols[None, :] >= cs[:, None]) & (cols[None, :] < cs[:, None] + NA_KC)
    dc = np.clip(cols[None, :] - cols[:, None], -(NA_KC - 1), NA_KC - 1) + NA_KC - 1
    onehot = (dc[None] == np.arange(2 * NA_KC - 1)[:, None, None]).astype(np.float32)
    rpb_c = jnp.einsum("lhdj,jqk->lhdqk", rpb, jnp.asarray(onehot), precision=lax.Precision.HIGHEST)
    rpb_c = jnp.where(jnp.asarray(col_mask), rpb_c * LOG2E - shift[:, None, None, None, None], MASK_NEG)
    band_rows = (NA_KR_MAX - 1 - np.arange(kr))[:, None] + np.arange(kr)[None, :]
    b = rpb_c[:, :, band_rows]
    return b.transpose(0, 2, 1, 4, 3, 5).reshape(rpb.shape[0], kr, NA_HEADS, GRID_W, kr * GRID_W)


def _softmax_shifts(p):
    amax = lambda v: jnp.max(jnp.abs(v), axis=-1)
    gq, gk = p["g_mla_q"], p["g_mla_k"]
    qn = (MLA_QK ** -0.5 * LOG2E) * jnp.sqrt(MLA_NOPE * amax(gq[:, :MLA_NOPE]) ** 2 + MLA_ROPE * amax(gq[:, MLA_NOPE:]) ** 2)
    kn = jnp.sqrt(MLA_NOPE * amax(gk[:, :MLA_NOPE]) ** 2 + MLA_ROPE * amax(gk[:, MLA_NOPE:]) ** 2)
    b_mla = BOUND_SLACK * qn * kn
    mla_fast = 2.0 * b_mla <= MAX_SHIFT_GAP
    b_na = BOUND_SLACK * (NA_DH ** -0.5 * LOG2E) * NA_DH * amax(p["g_na_q"]) * amax(p["g_na_k"])
    bias_hi = jnp.max(p["na_rpb"], axis=(1, 2, 3)) * LOG2E
    bias_lo = jnp.min(p["na_rpb"], axis=(1, 2, 3)) * LOG2E
    na_fast = 2.0 * b_na + (bias_hi - bias_lo) <= MAX_SHIFT_GAP
    return (mla_fast, jnp.where(mla_fast, b_mla, 0.0)), (na_fast, jnp.where(na_fast, b_na + bias_hi, 0.0))


def _prep_weights(p, rows, consts):
    bf = jnp.bfloat16
    nl = p["w_in"].shape[0]
    (mla_fast, mla_shift), (na_fast, na_shift) = _softmax_shifts(p)
    w_in = p["w_in"]
    split = Q_LORA + KV_LORA + MLA_ROPE
    zeros = jnp.zeros((nl, D_MODEL, LANES - MLA_ROPE), w_in.dtype)
    w_in_p = jnp.concatenate([w_in[:, :, :split], zeros, w_in[:, :, split:]], axis=2)
    w_qb = jnp.pad(p["w_q_b"].reshape(nl, Q_LORA, MLA_HEADS, MLA_QK), ((0, 0), (0, 0), (0, 0), (0, HEAD_PAD - MLA_QK)))
    w_kv = p["w_kv_b"].reshape(nl, KV_LORA, MLA_HEADS, MLA_NOPE + MLA_V)
    w_kc = w_kv[..., :MLA_NOPE].reshape(nl, KV_LORA, MLA_HEADS * MLA_NOPE)
    w_vt = w_kv[..., MLA_NOPE:].reshape(nl, KV_LORA, MLA_WIDTH).transpose(0, 2, 1)
    g_q = jnp.tile(jnp.pad(p["g_mla_q"] * (MLA_QK ** -0.5 * LOG2E), ((0, 0), (0, HEAD_PAD - MLA_QK))), (1, MLA_HEADS))
    w_r = jnp.concatenate([p["w_router_group"], p["w_router_expert"]], axis=2)
    w_r = jnp.pad(w_r, ((0, 0), (0, 0), (0, LANES - w_r.shape[2])))
    b_r = jnp.concatenate([p["b_router_group"], p["b_router_expert"]], axis=1)
    b_r = jnp.pad(b_r, ((0, 0), (0, LANES - b_r.shape[1])))
    row = lambda v: v[:, None, :]
    return dict(
        g_mix=row(p["g_mix_norm"]), w_in=w_in_p.astype(bf),
        g_q_a=row(p["g_q_a"]), w_qb=w_qb.reshape(nl, Q_LORA, MLA_PAD_WIDTH).astype(bf),
        g_kv_a=row(p["g_kv_a"]), w_kc=w_kc.astype(bf), w_vt=w_vt.astype(bf),
        g_q=row(g_q), g_kc=row(jnp.tile(p["g_mla_k"][:, :MLA_NOPE], (1, MLA_HEADS))),
        g_kpe=row(jnp.pad(p["g_mla_k"][:, MLA_NOPE:], ((0, 0), (0, LANES - MLA_ROPE)))),
        g_naq=row(jnp.tile(p["g_na_q"] * (NA_DH ** -0.5 * LOG2E), (1, NA_HEADS))),
        g_nak=row(jnp.tile(p["g_na_k"], (1, NA_HEADS))),
        na_bias=_na_bias_tables(p["na_rpb"], rows, na_shift), na_fast=na_fast,
        q_shift=-mla_shift[:, None, None] * consts["shift_lanes"][None], mla_fast=mla_fast,
        g_mla_out=row(p["g_mla_out"]), g_na_out=row(p["g_na_out"]),
        w_out=p["w_out"].astype(bf), g_ffn=row(p["g_ffn_norm"]),
        w_r=w_r.astype(bf), b_r=row(b_r),
        w_g=p["w_gate"].astype(bf), w_u=p["w_up"].astype(bf), w_d=p["w_down"].astype(bf))


def kernel(x, g_mix_norm, w_in, g_q_a, w_q_b, g_kv_a, w_kv_b, g_mla_q, g_mla_k, g_na_q, g_na_k, na_rpb, g_mla_out, g_na_out, w_out, g_ffn_norm, w_router_group, b_router_group, w_router_expert, b_router_expert, w_gate, w_up, w_down):
    batch, seq, d = x.shape
    assert d == D_MODEL and seq % TM_PROJ == 0 and seq % TQ_MLA == 0 and (batch * seq) % (TM_MOE * MOE_SUBTILES) == 0
    rows = seq // GRID_W
    assert rows % NA_ROWS_PER_STEP == 0 and rows >= NA_KR_MAX
    p = dict(g_mix_norm=g_mix_norm, w_in=w_in, g_q_a=g_q_a, w_q_b=w_q_b, g_kv_a=g_kv_a, w_kv_b=w_kv_b,
             g_mla_q=g_mla_q, g_mla_k=g_mla_k, g_na_q=g_na_q, g_na_k=g_na_k, na_rpb=na_rpb,
             g_mla_out=g_mla_out, g_na_out=g_na_out, w_out=w_out, g_ffn_norm=g_ffn_norm,
             w_router_group=w_router_group, b_router_group=b_router_group,
             w_router_expert=w_router_expert, b_router_expert=b_router_expert,
             w_gate=w_gate, w_up=w_up, w_down=w_down)
    consts = _constants(seq)
    sw = _prep_weights(p, rows, consts)
    xf = x.reshape(batch * seq, d)
    for l in range(w_in.shape[0]):
        q, k, vt, naq, nak, nav = _proj_call(xf, sw, l, consts, seq)
        o_a = lax.cond(sw["mla_fast"][l],
                       functools.partial(_mla_call, batch=batch, seq=seq, use_max=False),
                       functools.partial(_mla_call, batch=batch, seq=seq, use_max=True), q, k, vt)
        o_b = lax.cond(sw["na_fast"][l],
                       functools.partial(_na_call, l=l, batch=batch, seq=seq, use_max=False),
                       functools.partial(_na_call, l=l, batch=batch, seq=seq, use_max=True), naq, nak, nav, sw["na_bias"])
        x1, h2 = _mix_call(xf, o_a, o_b, sw, l)
        xf = _moe_call(x1, h2, sw, l)
    return xf.reshape(batch, seq, d)
```

```python
import functools

import numpy as np
import jax
import jax.numpy as jnp
from jax import lax
from jax.experimental import pallas as pl
from jax.experimental.pallas import tpu as pltpu

D_MODEL = 1024
GRID_W = 64
MLA_HEADS = 8
MLA_NOPE = 64
MLA_ROPE = 32
MLA_V = 64
MLA_QK = MLA_NOPE + MLA_ROPE
Q_LORA = 256
KV_LORA = 128
MLA_WIDTH = MLA_HEADS * MLA_V
ROPE_THETA = 10000.0
NA_HEADS = 8
NA_DH = 64
NA_WIDTH = NA_HEADS * NA_DH
NA_KR_MAX = 8
NA_KC = 16
N_GROUPS = 4
EXPERTS_PER_GROUP = 4
D_EXPERT = 256
EPS = 1e-6

LANES = 128
MXU_TILE = 256
BF16_ROWS = 16
HEAD_PAD = LANES
MLA_PAD_WIDTH = MLA_HEADS * HEAD_PAD
PROJ_COLS = 2048
LOG2E = 1.4426950408889634
MASK_NEG = -1e30
SHIFT_LANE = MLA_QK
MAX_SHIFT_GAP = 100.0
BOUND_SLACK = 1.02
VMEM_LIMIT = 56 * 1024 * 1024

C_CQ = 0
C_CKV = C_CQ + Q_LORA
C_KPE = C_CKV + KV_LORA
C_NAQ = C_KPE + LANES
C_NAK = C_NAQ + NA_WIDTH
C_NAV = C_NAK + NA_WIDTH

TM_PROJ = 1024
PROJ_SUBBLOCKS = 4
MIX_SUBBLOCKS = 4
TQ_MLA = 512
MLA_HEADS_PER_STEP = 4
NA_ROWS_PER_STEP = 16
TM_MOE = 512
MOE_SUBTILES = 2
MOE_SINGLE_CHUNKS = (128, 176, 224)
MOE_LOOP_CHUNK = 256
MOE_ROWS = TM_MOE + MOE_LOOP_CHUNK
RANK_BLOCK = 256
ROUTE_GID = 0
ROUTE_W0 = 1
ROUTE_LO = 8
H2_COLS = D_MODEL + LANES

assert N_GROUPS == 4
assert MLA_NOPE == NA_DH


def _cparams(sem):
    return pltpu.CompilerParams(dimension_semantics=sem, vmem_limit_bytes=VMEM_LIMIT)


def _full(shape):
    nd = len(shape)
    return pl.BlockSpec(shape, lambda *_: (0,) * nd)


def _layer(w, l, **kw):
    nd = w.ndim
    return pl.BlockSpec((None,) + w.shape[1:], lambda *_: (l,) + (0,) * (nd - 1), **kw)


def _bf16(x):
    return x.astype(jnp.bfloat16)


def _dot(a, b):
    return jnp.dot(a, b, preferred_element_type=jnp.float32)


def _dot_nt(a, b):
    return lax.dot_general(a, b, (((1,), (1,)), ((), ())), preferred_element_type=jnp.float32)


def _rms_rows(x, gain):
    ms = jnp.mean(x * x, axis=-1, keepdims=True)
    return x * lax.rsqrt(ms + EPS) * gain


def _segment_rms_scale(x, seg_mat):
    n = x.shape[-1]
    parts = []
    for j in range(0, n, seg_mat.shape[0]):
        w = min(seg_mat.shape[0], n - j)
        xs = x[:, j:j + w]
        parts.append(_dot(_bf16(xs * xs), seg_mat[:w, :w]))
    ms = parts[0] if len(parts) == 1 else jnp.concatenate(parts, axis=-1)
    return lax.rsqrt(ms + EPS)


def _rope_block(x, cos, sin_a, sin_b):
    return x * cos + pltpu.roll(x, LANES - 8, 1) * sin_a + pltpu.roll(x, 8, 1) * sin_b


def _proj_rows(rows, x_ref, gmix_ref, win_ref, gqa_ref, wqb_ref, gkva_ref, wkc_ref, wvt_ref,
               gq_ref, gkc_ref, gkpe_ref, gnaq_ref, gnak_ref,
               mq_ref, mkpe_ref, mna_ref, qshift_ref, kone_ref,
               cq_ref, sqa_ref, sqb_ref, ck_ref, ska_ref, skb_ref,
               q_out, k_out, vt_out, naq_out, nak_out, nav_out):
    h = _bf16(_rms_rows(x_ref[rows, :], gmix_ref[...]))
    proj = _dot(h, win_ref[...])

    cqn = _bf16(_rms_rows(proj[:, C_CQ:C_CKV], gqa_ref[...]))
    q = _dot(cqn, wqb_ref[...])
    q = q * _segment_rms_scale(q, mq_ref[...]) * gq_ref[...]
    cosq, sqa, sqb = cq_ref[rows, :], sqa_ref[rows, :], sqb_ref[rows, :]
    for hd in range(MLA_HEADS):
        sl = slice(hd * HEAD_PAD, (hd + 1) * HEAD_PAD)
        q_out[rows, sl] = _bf16(_rope_block(q[:, sl], cosq, sqa, sqb) + qshift_ref[:, sl])

    ckvn = _bf16(_rms_rows(proj[:, C_CKV:C_KPE], gkva_ref[...]))
    kc = _dot(ckvn, wkc_ref[...])
    kc = kc * _segment_rms_scale(kc, mna_ref[...]) * gkc_ref[...]
    vt_out[:, rows] = _bf16(_dot_nt(wvt_ref[...], ckvn))
    kp = proj[:, C_KPE:C_NAQ]
    kp = kp * lax.rsqrt(_dot(_bf16(kp * kp), mkpe_ref[...]) + EPS) * gkpe_ref[...]
    kp = _rope_block(kp, ck_ref[rows, :], ska_ref[rows, :], skb_ref[rows, :])
    tail = pltpu.roll(kp, MLA_NOPE, 1) + kone_ref[...]
    lane = lax.broadcasted_iota(jnp.int32, tail.shape, 1)
    for pr in range(MLA_HEADS // 2):
        two = kc[:, pr * LANES:(pr + 1) * LANES]
        k_out[rows, (2 * pr) * HEAD_PAD:(2 * pr + 1) * HEAD_PAD] = _bf16(jnp.where(lane < MLA_NOPE, two, tail))
        k_out[rows, (2 * pr + 1) * HEAD_PAD:(2 * pr + 2) * HEAD_PAD] = _bf16(
            jnp.where(lane < MLA_NOPE, pltpu.roll(two, MLA_NOPE, 1), tail))

    naq = proj[:, C_NAQ:C_NAK]
    naq_out[rows, :] = _bf16(naq * _segment_rms_scale(naq, mna_ref[...]) * gnaq_ref[...])
    nak = proj[:, C_NAK:C_NAV]
    nak_out[rows, :] = _bf16(nak * _segment_rms_scale(nak, mna_ref[...]) * gnak_ref[...])
    nav_out[rows, :] = _bf16(proj[:, C_NAV:PROJ_COLS])


def _proj_kernel(*refs):
    sub = TM_PROJ // PROJ_SUBBLOCKS
    for sb in range(PROJ_SUBBLOCKS):
        _proj_rows(slice(sb * sub, (sb + 1) * sub), *refs)


def _proj_call(x2d, sw, l, consts, seq):
    t = x2d.shape[0]
    tm = TM_PROJ
    n_seq_tiles = seq // tm
    row = lambda i: (i, 0)
    tab = lambda i: (i % n_seq_tiles, 0)
    layered = ("g_mix", "w_in", "g_q_a", "w_qb", "g_kv_a", "w_kc", "w_vt", "g_q", "g_kc", "g_kpe", "g_naq", "g_nak")
    weights = [sw[n] for n in layered] + [consts["m_q"], consts["m_kpe"], consts["m_na"], sw["q_shift"], consts["k_one"]]
    specs = ([_layer(sw[n], l) for n in layered] + [_full(consts[n].shape) for n in ("m_q", "m_kpe", "m_na")]
             + [_layer(sw["q_shift"], l), _full(consts["k_one"].shape)])
    tables = [consts["cos_q"], consts["sin_qa"], consts["sin_qb"], consts["cos_k"], consts["sin_ka"], consts["sin_kb"]]
    in_specs = [pl.BlockSpec((tm, D_MODEL), row)] + specs + [pl.BlockSpec((tm, LANES), tab) for _ in tables]
    out_shapes = [jax.ShapeDtypeStruct((t, MLA_PAD_WIDTH), jnp.bfloat16),
                  jax.ShapeDtypeStruct((t, MLA_PAD_WIDTH), jnp.bfloat16),
                  jax.ShapeDtypeStruct((MLA_WIDTH, t), jnp.bfloat16),
                  jax.ShapeDtypeStruct((t, NA_WIDTH), jnp.bfloat16),
                  jax.ShapeDtypeStruct((t, NA_WIDTH), jnp.bfloat16),
                  jax.ShapeDtypeStruct((t, NA_WIDTH), jnp.bfloat16)]
    out_specs = [pl.BlockSpec((MLA_WIDTH, tm), lambda i: (0, i)) if s.shape[0] == MLA_WIDTH
                 else pl.BlockSpec((tm, s.shape[1]), row) for s in out_shapes]
    return pl.pallas_call(
        _proj_kernel, grid=(t // tm,), in_specs=in_specs, out_specs=out_specs, out_shape=out_shapes,
        compiler_params=_cparams(("parallel",)), name="proj",
    )(x2d, *weights, *tables)


def _mla_kernel(q_ref, k_ref, vt_ref, o_ref, *, use_max):
    scores = []
    for hh in range(MLA_HEADS_PER_STEP):
        sl = slice(hh * HEAD_PAD, (hh + 1) * HEAD_PAD)
        scores.append(_dot_nt(k_ref[:, sl], q_ref[:, sl]))
    outs = []
    for hh, s in enumerate(scores):
        p = jnp.exp2(s - jnp.max(s, axis=0, keepdims=True)) if use_max else jnp.exp2(s)
        l = jnp.sum(p, axis=0, keepdims=True)
        pair = slice((hh // 2) * LANES, (hh // 2 + 1) * LANES)
        outs.append(_dot(vt_ref[pair, :], _bf16(p)) * (1.0 / l))
    row = lax.broadcasted_iota(jnp.int32, outs[0].shape, 0)
    for pp in range(MLA_HEADS_PER_STEP // 2):
        o_t = jnp.where(row < MLA_V, outs[2 * pp], outs[2 * pp + 1])
        o_ref[:, pp * LANES:(pp + 1) * LANES] = _bf16(o_t.T)


def _mla_call(q, k, vt, *, batch, seq, use_max):
    t = q.shape[0]
    tq = TQ_MLA
    nq = seq // tq
    hps = MLA_HEADS_PER_STEP
    return pl.pallas_call(
        functools.partial(_mla_kernel, use_max=use_max), grid=(batch, MLA_HEADS // hps, nq),
        in_specs=[pl.BlockSpec((tq, hps * HEAD_PAD), lambda b, p, i: (b * nq + i, p)),
                  pl.BlockSpec((seq, hps * HEAD_PAD), lambda b, p, i: (b, p)),
                  pl.BlockSpec((hps * MLA_V, seq), lambda b, p, i: (p, b))],
        out_specs=pl.BlockSpec((tq, hps * MLA_V), lambda b, p, i: (b * nq + i, p)),
        out_shape=jax.ShapeDtypeStruct((t, MLA_WIDTH), jnp.bfloat16),
        compiler_params=_cparams(("parallel", "parallel", "arbitrary")), name="mla_attn",
    )(q, k, vt)


def _na_kernel(q_ref, k_ref, v_ref, bias_ref, o_ref, *, rows, kr, use_max):
    step = pl.program_id(1)
    band = kr * GRID_W
    lane = lax.broadcasted_iota(jnp.int32, (GRID_W, LANES), 1)
    units = []
    for rr in range(NA_ROWS_PER_STEP):
        r = step * NA_ROWS_PER_STEP + rr
        rs = jnp.clip(r - kr // 2, 0, rows - kr)
        rel0 = rs - r + (NA_KR_MAX - 1)
        start = pl.multiple_of(rs * GRID_W, GRID_W)
        qrow = q_ref[rr * GRID_W:(rr + 1) * GRID_W, :]
        for pr in range(NA_HEADS // 2):
            sl = slice(pr * LANES, (pr + 1) * LANES)
            q2 = qrow[:, sl]
            zero = jnp.zeros_like(q2)
            qq = jnp.concatenate([jnp.where(lane < NA_DH, q2, zero), jnp.where(lane >= NA_DH, q2, zero)], axis=0)
            s = _dot_nt(qq, k_ref[pl.ds(start, band), sl])
            units.append((rr, pr, start, rel0, s))
    probs = []
    for rr, pr, start, rel0, s in units:
        bias = jnp.concatenate([jnp.concatenate([bias_ref[2 * pr + hh, rel0 + 2 * m] for m in range(kr // 2)], axis=-1)
                                for hh in range(2)], axis=0)
        s = s + bias
        p = jnp.exp2(s - jnp.max(s, axis=-1, keepdims=True)) if use_max else jnp.exp2(s)
        l = jnp.sum(p, axis=-1, keepdims=True)
        probs.append((_bf16(p), 1.0 / l))
    for (rr, pr, start, rel0, s), (p, rl) in zip(units, probs):
        sl = slice(pr * LANES, (pr + 1) * LANES)
        pv = _dot(p, v_ref[pl.ds(start, band), sl]) * rl
        o_ref[rr * GRID_W:(rr + 1) * GRID_W, sl] = _bf16(jnp.where(lane < NA_DH, pv[:GRID_W], pv[GRID_W:]))


def _na_call(q, k, v, bias, *, l, batch, seq, use_max):
    t = q.shape[0]
    rows = seq // GRID_W
    kr = min(NA_KR_MAX, rows)
    steps = rows // NA_ROWS_PER_STEP
    blk = NA_ROWS_PER_STEP * GRID_W
    return pl.pallas_call(
        functools.partial(_na_kernel, rows=rows, kr=kr, use_max=use_max), grid=(batch, steps),
        in_specs=[pl.BlockSpec((blk, NA_WIDTH), lambda b, i: (b * steps + i, 0)),
                  pl.BlockSpec((seq, NA_WIDTH), lambda b, i: (b, 0)),
                  pl.BlockSpec((seq, NA_WIDTH), lambda b, i: (b, 0)),
                  _layer(bias, l)],
        out_specs=pl.BlockSpec((blk, NA_WIDTH), lambda b, i: (b * steps + i, 0)),
        out_shape=jax.ShapeDtypeStruct((t, NA_WIDTH), jnp.bfloat16),
        compiler_params=_cparams(("parallel", "arbitrary")), name="na_attn",
    )(q, k, v, bias)


def _mix_rows(rows, x_ref, oa_ref, ob_ref, ga_ref, gb_ref, wout_ref, gffn_ref, wr_ref, br_ref, x1_out, h2_out):
    mixed = jnp.concatenate([_rms_rows(oa_ref[rows, :].astype(jnp.float32), ga_ref[...]),
                             _rms_rows(ob_ref[rows, :].astype(jnp.float32), gb_ref[...])], axis=-1)
    x1 = x_ref[rows, :] + _dot(_bf16(mixed), wout_ref[...])
    x1_out[rows, :] = x1
    hi = _bf16(_rms_rows(x1, gffn_ref[...]))
    h2_out[rows, :D_MODEL] = hi
    logits = _dot(hi, wr_ref[...]) + br_ref[...]

    lane = lax.broadcasted_iota(jnp.int32, logits.shape, 1).astype(jnp.float32)
    neg = jnp.float32(-jnp.inf)
    big = jnp.float32(LANES)
    is_grp = lane < N_GROUPS
    lg = jnp.where(is_grp, logits, neg)
    gmax = jnp.max(lg, axis=-1, keepdims=True)
    gid = jnp.min(jnp.where(lg == gmax, lane, big), axis=-1, keepdims=True)
    pg_top = 1.0 / jnp.sum(jnp.where(is_grp, jnp.exp(logits - gmax), 0.0), axis=-1, keepdims=True)
    base = N_GROUPS + EXPERTS_PER_GROUP * gid
    in_sel = (lane >= base) & (lane < base + EXPERTS_PER_GROUP)
    le = jnp.where(in_sel, logits, neg)
    m1 = jnp.max(le, axis=-1, keepdims=True)
    i1 = jnp.min(jnp.where(le == m1, lane, big), axis=-1, keepdims=True)
    le2 = jnp.where(lane == i1, neg, le)
    m2 = jnp.max(le2, axis=-1, keepdims=True)
    i2 = jnp.min(jnp.where(le2 == m2, lane, big), axis=-1, keepdims=True)
    e2 = jnp.exp(m2 - m1)
    w1 = pg_top / (1.0 + e2)
    w2 = pg_top * e2 / (1.0 + e2)
    j1 = i1 - base + ROUTE_W0
    j2 = i2 - base + ROUTE_W0
    w1h = _bf16(w1).astype(jnp.float32)
    w2h = _bf16(w2).astype(jnp.float32)
    rec = jnp.where(lane == ROUTE_GID, gid,
                    jnp.where(lane == j1, w1h, jnp.where(lane == j2, w2h,
                    jnp.where(lane == j1 + ROUTE_LO, w1 - w1h, jnp.where(lane == j2 + ROUTE_LO, w2 - w2h, 0.0)))))
    h2_out[rows, D_MODEL:] = _bf16(rec)


def _mix_kernel(*refs):
    sub = TM_PROJ // MIX_SUBBLOCKS
    for sb in range(MIX_SUBBLOCKS):
        _mix_rows(slice(sb * sub, (sb + 1) * sub), *refs)


def _mix_call(x2d, oa, ob, sw, l):
    t = x2d.shape[0]
    tm = TM_PROJ
    row = lambda i: (i, 0)
    weights = [sw[n] for n in ("g_mla_out", "g_na_out", "w_out", "g_ffn", "w_r", "b_r")]
    return pl.pallas_call(
        _mix_kernel, grid=(t // tm,),
        in_specs=[pl.BlockSpec((tm, D_MODEL), row), pl.BlockSpec((tm, MLA_WIDTH), row),
                  pl.BlockSpec((tm, NA_WIDTH), row)] + [_layer(w, l) for w in weights],
        out_specs=[pl.BlockSpec((tm, D_MODEL), row), pl.BlockSpec((tm, H2_COLS), row)],
        out_shape=[jax.ShapeDtypeStruct((t, D_MODEL), jnp.float32),
                   jax.ShapeDtypeStruct((t, H2_COLS), jnp.bfloat16)],
        compiler_params=_cparams(("parallel",)), name="mix_router",
    )(x2d, oa, ob, *weights)


def _moe_partition(sub, h2_ref, xs_ref, ys_ref, cnt_ref):
    tm = TM_MOE
    rows = slice(sub * tm, (sub + 1) * tm)
    route = h2_ref[rows, D_MODEL:].astype(jnp.float32)
    lane = lax.broadcasted_iota(jnp.int32, route.shape, 1)
    gid = jnp.sum(jnp.where(lane == ROUTE_GID, route, 0.0), axis=-1, keepdims=True)
    onehot = jnp.where((lane.astype(jnp.float32) == gid) & (lane < N_GROUPS), 1.0, 0.0)
    r_i = lax.broadcasted_iota(jnp.int32, (RANK_BLOCK, RANK_BLOCK), 0)
    c_i = lax.broadcasted_iota(jnp.int32, (RANK_BLOCK, RANK_BLOCK), 1)
    tri = jnp.where(c_i < r_i, 1.0, 0.0).astype(jnp.bfloat16)
    ranks = []
    sizes = jnp.zeros((1, LANES), jnp.float32)
    for blk in range(tm // RANK_BLOCK):
        oh = onehot[blk * RANK_BLOCK:(blk + 1) * RANK_BLOCK]
        ranks.append(_dot(tri, _bf16(oh)) + sizes)
        sizes = sizes + jnp.sum(oh, axis=0, keepdims=True)
    rank = jnp.concatenate(ranks, axis=0)
    lane1 = lane[0:1]
    start_vec = jnp.zeros((1, LANES), jnp.float32)
    acc = jnp.int32(0)
    for g in range(N_GROUPS):
        n = jnp.sum(jnp.where(lane1 == g, sizes, 0.0)).astype(jnp.int32)
        cnt_ref[sub * N_GROUPS + g] = acc
        cnt_ref[MOE_SUBTILES * N_GROUPS + sub * N_GROUPS + g] = acc + n
        start_vec = jnp.where(lane1 == g, acc.astype(jnp.float32), start_vec)
        acc = acc + n
    dest = jnp.sum(onehot * (start_vec + rank), axis=-1, keepdims=True)
    dest_row = jnp.transpose(jnp.broadcast_to(dest, (tm, LANES)))[0:1, :]
    rowi = lax.broadcasted_iota(jnp.int32, (tm, tm), 0)
    perm = jnp.where(rowi == dest_row.astype(jnp.int32), 1.0, 0.0).astype(jnp.bfloat16)
    xs_ref[sub, :tm, :] = _bf16(_dot(perm, h2_ref[rows, :]))
    xs_ref[sub, tm:, :] = jnp.zeros((MOE_ROWS - tm, H2_COLS), jnp.bfloat16)
    ys_ref[sub] = jnp.zeros((MOE_ROWS, D_MODEL), jnp.bfloat16)
    return dest


def _moe_experts(sub, g, off, size, start_g, end_g, xs_ref, ys_ref, wg_ref, wu_ref, wd_ref):
    off = pl.multiple_of(off, BF16_ROWS)
    xc = xs_ref[sub, pl.ds(off, size), :D_MODEL]
    cw = xs_ref[sub, pl.ds(off, size), D_MODEL:].astype(jnp.float32)
    hid = []
    for j in range(EXPERTS_PER_GROUP):
        a = _dot(xc, wg_ref[g * EXPERTS_PER_GROUP + j])
        u = _dot(xc, wu_ref[g * EXPERTS_PER_GROUP + j])
        cj = cw[:, ROUTE_W0 + j:ROUTE_W0 + j + 1] + cw[:, ROUTE_W0 + ROUTE_LO + j:ROUTE_W0 + ROUTE_LO + j + 1]
        hid.append(a * (1.0 / (1.0 + jnp.exp(-a))) * u * cj)
    hid = _bf16(jnp.concatenate(hid, axis=-1))
    w_down = wd_ref[pl.ds(g * EXPERTS_PER_GROUP, EXPERTS_PER_GROUP)].reshape(EXPERTS_PER_GROUP * D_EXPERT, D_MODEL)
    rows = off + lax.broadcasted_iota(jnp.int32, (size, 1), 0)
    mine = (rows >= start_g) & (rows < end_g)
    old = ys_ref[sub, pl.ds(off, size), :].astype(jnp.float32)
    ys_ref[sub, pl.ds(off, size), :] = _bf16(jnp.where(mine, _dot(hid, w_down), old))


def _moe_kernel(x1_ref, h2_ref, wg_ref, wu_ref, wd_ref, o_ref, xs_ref, ys_ref, cnt_ref):
    tm = TM_MOE
    dests = [_moe_partition(sub, h2_ref, xs_ref, ys_ref, cnt_ref) for sub in range(MOE_SUBTILES)]

    def _run(u, carry):
        sub = lax.shift_right_logical(u, 2)
        g = lax.bitwise_and(u, N_GROUPS - 1)
        start_g = cnt_ref[u]
        end_g = cnt_ref[MOE_SUBTILES * N_GROUPS + u]
        first = lax.shift_left(lax.shift_right_logical(start_g, 4), 4)
        span = end_g - first
        args = (xs_ref, ys_ref, wg_ref, wu_ref, wd_ref)
        lo = 0
        for size in MOE_SINGLE_CHUNKS:
            @pl.when((span > lo) & (span <= size))
            def _(size=size):
                _moe_experts(sub, g, first, size, start_g, end_g, *args)
            lo = size

        @pl.when(span > lo)
        def _():
            def _chunk(c, inner):
                _moe_experts(sub, g, first + c * MOE_LOOP_CHUNK, MOE_LOOP_CHUNK, start_g, end_g, *args)
                return inner
            lax.fori_loop(0, lax.div(span + (MOE_LOOP_CHUNK - 1), jnp.int32(MOE_LOOP_CHUNK)), _chunk, 0)
        return carry

    lax.fori_loop(0, MOE_SUBTILES * N_GROUPS, _run, 0)

    col = lax.broadcasted_iota(jnp.int32, (tm, tm), 1)
    for sub, dest in enumerate(dests):
        rows = slice(sub * tm, (sub + 1) * tm)
        unperm = jnp.where(col == dest.astype(jnp.int32), 1.0, 0.0).astype(jnp.bfloat16)
        o_ref[rows, :] = x1_ref[rows, :] + _dot(unperm, ys_ref[sub, :tm, :])


def _moe_call(x1, h2, sw, l):
    t = x1.shape[0]
    tm = TM_MOE * MOE_SUBTILES
    row = lambda i: (i, 0)
    resident = lambda w: _layer(w, l, pipeline_mode=pl.Buffered(1))
    return pl.pallas_call(
        _moe_kernel, grid=(t // tm,),
        in_specs=[pl.BlockSpec((tm, D_MODEL), row), pl.BlockSpec((tm, H2_COLS), row),
                  resident(sw["w_g"]), resident(sw["w_u"]), resident(sw["w_d"])],
        out_specs=pl.BlockSpec((tm, D_MODEL), row),
        out_shape=jax.ShapeDtypeStruct((t, D_MODEL), jnp.float32),
        scratch_shapes=[pltpu.VMEM((MOE_SUBTILES, MOE_ROWS, H2_COLS), jnp.bfloat16),
                        pltpu.VMEM((MOE_SUBTILES, MOE_ROWS, D_MODEL), jnp.bfloat16),
                        pltpu.SMEM((2 * MOE_SUBTILES * N_GROUPS,), jnp.int32)],
        compiler_params=_cparams(("parallel",)), name="moe",
    )(x1, h2, sw["w_g"], sw["w_u"], sw["w_d"])


def _segment_matrix(width, segments):
    m = np.zeros((width, width), np.float32)
    for lo, hi in segments:
        m[lo:hi, lo:hi] = 1.0 / (hi - lo)
    return jnp.asarray(m, jnp.bfloat16)


def _constants(seq):
    t = np.arange(seq)
    row = (t // GRID_W).astype(np.float32)
    col = (t % GRID_W).astype(np.float32)
    n_freq = MLA_ROPE // 4
    inv = (np.float32(ROPE_THETA) ** (-np.arange(n_freq, dtype=np.float32) / n_freq)).astype(np.float32)
    ang_r = row[:, None] * inv[None, :]
    ang_c = col[:, None] * inv[None, :]

    def tables(base):
        cos = np.zeros((seq, LANES), np.float32)
        cos[:, :base] = 1.0
        sa = np.zeros((seq, LANES), np.float32)
        sb = np.zeros((seq, LANES), np.float32)
        for k, ang in enumerate((ang_r, ang_c)):
            o = base + 2 * n_freq * k
            cos[:, o:o + n_freq] = np.cos(ang)
            cos[:, o + n_freq:o + 2 * n_freq] = np.cos(ang)
            sa[:, o:o + n_freq] = -np.sin(ang)
            sb[:, o + n_freq:o + 2 * n_freq] = np.sin(ang)
        return jnp.asarray(cos), jnp.asarray(sa), jnp.asarray(sb)

    cos_q, sin_qa, sin_qb = tables(MLA_NOPE)
    cos_k, sin_ka, sin_kb = tables(0)
    seg_q = []
    for hb in range(0, MXU_TILE, HEAD_PAD):
        seg_q += [(hb, hb + MLA_NOPE), (hb + MLA_NOPE, hb + MLA_QK)]
    shift_lanes = np.zeros((1, MLA_PAD_WIDTH), np.float32)
    shift_lanes[0, SHIFT_LANE::HEAD_PAD] = 1.0
    return dict(cos_q=cos_q, sin_qa=sin_qa, sin_qb=sin_qb, cos_k=cos_k, sin_ka=sin_ka, sin_kb=sin_kb,
                m_q=_segment_matrix(MXU_TILE, seg_q),
                m_kpe=_segment_matrix(LANES, [(0, MLA_ROPE)]),
                m_na=_segment_matrix(MXU_TILE, [(o, o + NA_DH) for o in range(0, MXU_TILE, NA_DH)]),
                shift_lanes=jnp.asarray(shift_lanes), k_one=jnp.asarray(shift_lanes[:, :HEAD_PAD]))


def _na_bias_tables(rpb, shift):
    cols = np.arange(GRID_W)
    cs = np.clip(cols - NA_KC // 2, 0, GRID_W - NA_KC)
    col_mask = (cols[None, :] >= cs[:, None]) & (cols[None, :] < cs[:, None] + NA_KC)
    dc = np.clip(cols[None, :] - cols[:, None], -(NA_KC - 1), NA_KC - 1) + NA_KC - 1
    onehot = (dc[None] == np.arange(2 * NA_KC - 1)[:, None, None]).astype(np.float32)
    rpb_c = jnp.einsum("lhdj,jqk->lhdqk", rpb, jnp.asarray(onehot), precision=lax.Precision.HIGHEST)
    rpb_c = jnp.where(jnp.asarray(col_mask), rpb_c * LOG2E - shift[:, None, None, None, None], MASK_NEG)
    return jnp.concatenate([rpb_c[:, :, :-1], rpb_c[:, :, 1:]], axis=-1)


def _softmax_shifts(p):
    amax = lambda v: jnp.max(jnp.abs(v), axis=-1)
    gq, gk = p["g_mla_q"], p["g_mla_k"]
    qn = (MLA_QK ** -0.5 * LOG2E) * jnp.sqrt(MLA_NOPE * amax(gq[:, :MLA_NOPE]) ** 2 + MLA_ROPE * amax(gq[:, MLA_NOPE:]) ** 2)
    kn = jnp.sqrt(MLA_NOPE * amax(gk[:, :MLA_NOPE]) ** 2 + MLA_ROPE * amax(gk[:, MLA_NOPE:]) ** 2)
    b_mla = BOUND_SLACK * qn * kn
    mla_fast = 2.0 * b_mla <= MAX_SHIFT_GAP
    b_na = BOUND_SLACK * (NA_DH ** -0.5 * LOG2E) * NA_DH * amax(p["g_na_q"]) * amax(p["g_na_k"])
    bias_hi = jnp.max(p["na_rpb"], axis=(1, 2, 3)) * LOG2E
    bias_lo = jnp.min(p["na_rpb"], axis=(1, 2, 3)) * LOG2E
    na_fast = 2.0 * b_na + (bias_hi - bias_lo) <= MAX_SHIFT_GAP
    return (mla_fast, jnp.where(mla_fast, b_mla, 0.0)), (na_fast, jnp.where(na_fast, b_na + bias_hi, 0.0))


def _prep_weights(p, consts):
    bf = jnp.bfloat16
    nl = p["w_in"].shape[0]
    (mla_fast, mla_shift), (na_fast, na_shift) = _softmax_shifts(p)
    w_in = p["w_in"]
    split = Q_LORA + KV_LORA + MLA_ROPE
    w_in_p = jnp.zeros((nl, D_MODEL, PROJ_COLS), bf)
    w_in_p = w_in_p.at[:, :, :split].set(w_in[:, :, :split].astype(bf)).at[:, :, C_NAQ:].set(w_in[:, :, split:].astype(bf))
    w_qb = jnp.pad(p["w_q_b"].reshape(nl, Q_LORA, MLA_HEADS, MLA_QK), ((0, 0), (0, 0), (0, 0), (0, HEAD_PAD - MLA_QK)))
    w_kv = p["w_kv_b"].reshape(nl, KV_LORA, MLA_HEADS, MLA_NOPE + MLA_V)
    w_kc = w_kv[..., :MLA_NOPE].reshape(nl, KV_LORA, MLA_HEADS * MLA_NOPE)
    w_vt = w_kv[..., MLA_NOPE:].reshape(nl, KV_LORA, MLA_WIDTH).transpose(0, 2, 1)
    g_q = jnp.tile(jnp.pad(p["g_mla_q"] * (MLA_QK ** -0.5 * LOG2E), ((0, 0), (0, HEAD_PAD - MLA_QK))), (1, MLA_HEADS))
    w_r = jnp.concatenate([p["w_router_group"], p["w_router_expert"]], axis=2)
    w_r = jnp.pad(w_r, ((0, 0), (0, 0), (0, LANES - w_r.shape[2])))
    b_r = jnp.concatenate([p["b_router_group"], p["b_router_expert"]], axis=1)
    b_r = jnp.pad(b_r, ((0, 0), (0, LANES - b_r.shape[1])))
    row = lambda v: v[:, None, :]
    return dict(
        g_mix=row(p["g_mix_norm"]), w_in=w_in_p,
        g_q_a=row(p["g_q_a"]), w_qb=w_qb.reshape(nl, Q_LORA, MLA_PAD_WIDTH).astype(bf),
        g_kv_a=row(p["g_kv_a"]), w_kc=w_kc.astype(bf), w_vt=w_vt.astype(bf),
        g_q=row(g_q), g_kc=row(jnp.tile(p["g_mla_k"][:, :MLA_NOPE], (1, MLA_HEADS))),
        g_kpe=row(jnp.pad(p["g_mla_k"][:, MLA_NOPE:], ((0, 0), (0, LANES - MLA_ROPE)))),
        g_naq=row(jnp.tile(p["g_na_q"] * (NA_DH ** -0.5 * LOG2E), (1, NA_HEADS))),
        g_nak=row(jnp.tile(p["g_na_k"], (1, NA_HEADS))),
        na_bias=_na_bias_tables(p["na_rpb"], na_shift), na_fast=na_fast,
        q_shift=-mla_shift[:, None, None] * consts["shift_lanes"][None], mla_fast=mla_fast,
        g_mla_out=row(p["g_mla_out"]), g_na_out=row(p["g_na_out"]),
        w_out=p["w_out"].astype(bf), g_ffn=row(p["g_ffn_norm"]),
        w_r=w_r.astype(bf), b_r=row(b_r),
        w_g=p["w_gate"].astype(bf), w_u=p["w_up"].astype(bf), w_d=p["w_down"].astype(bf))


def kernel(x, g_mix_norm, w_in, g_q_a, w_q_b, g_kv_a, w_kv_b, g_mla_q, g_mla_k, g_na_q, g_na_k, na_rpb, g_mla_out, g_na_out, w_out, g_ffn_norm, w_router_group, b_router_group, w_router_expert, b_router_expert, w_gate, w_up, w_down):
    batch, seq, d = x.shape
    assert d == D_MODEL and seq % TM_PROJ == 0 and seq % TQ_MLA == 0 and (batch * seq) % (TM_MOE * MOE_SUBTILES) == 0
    rows = seq // GRID_W
    assert rows % NA_ROWS_PER_STEP == 0 and rows >= NA_KR_MAX and NA_KR_MAX % 2 == 0
    p = dict(g_mix_norm=g_mix_norm, w_in=w_in, g_q_a=g_q_a, w_q_b=w_q_b, g_kv_a=g_kv_a, w_kv_b=w_kv_b,
             g_mla_q=g_mla_q, g_mla_k=g_mla_k, g_na_q=g_na_q, g_na_k=g_na_k, na_rpb=na_rpb,
             g_mla_out=g_mla_out, g_na_out=g_na_out, w_out=w_out, g_ffn_norm=g_ffn_norm,
             w_router_group=w_router_group, b_router_group=b_router_group,
             w_router_expert=w_router_expert, b_router_expert=b_router_expert,
             w_gate=w_gate, w_up=w_up, w_down=w_down)
    consts = _constants(seq)
    sw = _prep_weights(p, consts)
    xf = x.reshape(batch * seq, d)
    for l in range(w_in.shape[0]):
        q, k, vt, naq, nak, nav = _proj_call(xf, sw, l, consts, seq)
        o_a = lax.cond(sw["mla_fast"][l],
                       functools.partial(_mla_call, batch=batch, seq=seq, use_max=False),
                       functools.partial(_mla_call, batch=batch, seq=seq, use_max=True), q, k, vt)
        o_b = lax.cond(sw["na_fast"][l],
                       functools.partial(_na_call, l=l, batch=batch, seq=seq, use_max=False),
                       functools.partial(_na_call, l=l, batch=batch, seq=seq, use_max=True), naq, nak, nav, sw["na_bias"])
        x1, h2 = _mix_call(xf, o_a, o_b, sw, l)
        xf = _moe_call(x1, h2, sw, l)
    return xf.reshape(batch, seq, d)
```

```python
import functools

import numpy as np
import jax
import jax.numpy as jnp
from jax import lax
from jax.experimental import pallas as pl
from jax.experimental.pallas import tpu as pltpu

D_MODEL = 1024
GRID_W = 64
MLA_HEADS = 8
MLA_NOPE = 64
MLA_ROPE = 32
MLA_V = 64
MLA_QK = MLA_NOPE + MLA_ROPE
Q_LORA = 256
KV_LORA = 128
MLA_WIDTH = MLA_HEADS * MLA_V
ROPE_THETA = 10000.0
NA_HEADS = 8
NA_DH = 64
NA_WIDTH = NA_HEADS * NA_DH
NA_KR_MAX = 8
NA_KC = 16
N_GROUPS = 4
EXPERTS_PER_GROUP = 4
D_EXPERT = 256
EPS = 1e-6

LANES = 128
MXU_TILE = 256
BF16_ROWS = 16
HEAD_PAD = LANES
MLA_PAD_WIDTH = MLA_HEADS * HEAD_PAD
PROJ_COLS = 2048
LOG2E = 1.4426950408889634
MASK_NEG = -1e30
SHIFT_LANE = MLA_QK
MAX_SHIFT_GAP = 100.0
BOUND_SLACK = 1.02
VMEM_LIMIT = 56 * 1024 * 1024

C_CQ = 0
C_CKV = C_CQ + Q_LORA
C_KPE = C_CKV + KV_LORA
C_NAQ = C_KPE + LANES
C_NAK = C_NAQ + NA_WIDTH
C_NAV = C_NAK + NA_WIDTH

TM_PROJ = 1024
PROJ_SUBBLOCKS = 4
MIX_SUBBLOCKS = 4
TQ_MLA = 512
MLA_HEADS_PER_STEP = 4
NA_ROWS_PER_STEP = 16
TM_MOE = 512
MOE_SUBTILES = 2
MOE_LOOP_CHUNK = 256
MOE_TAIL_CHUNKS = (128, 176, 224, 256)
MOE_ROWS = TM_MOE + MOE_LOOP_CHUNK
RANK_BLOCK = 256
ROUTE_GID = 0
ROUTE_W0 = 1
ROUTE_LO = 8
H2_COLS = D_MODEL + LANES

assert N_GROUPS == 4
assert MOE_LOOP_CHUNK & (MOE_LOOP_CHUNK - 1) == 0 and MOE_TAIL_CHUNKS[-1] == MOE_LOOP_CHUNK
assert MLA_NOPE == NA_DH


def _cparams(sem):
    return pltpu.CompilerParams(dimension_semantics=sem, vmem_limit_bytes=VMEM_LIMIT)


def _full(shape):
    nd = len(shape)
    return pl.BlockSpec(shape, lambda *_: (0,) * nd)


def _layer(w, l, **kw):
    nd = w.ndim
    return pl.BlockSpec((None,) + w.shape[1:], lambda *_: (l,) + (0,) * (nd - 1), **kw)


def _bf16(x):
    return x.astype(jnp.bfloat16)


def _dot(a, b):
    return jnp.dot(a, b, preferred_element_type=jnp.float32)


def _dot_nt(a, b):
    return lax.dot_general(a, b, (((1,), (1,)), ((), ())), preferred_element_type=jnp.float32)


def _rms_rows(x, gain):
    ms = jnp.mean(x * x, axis=-1, keepdims=True)
    return x * lax.rsqrt(ms + EPS) * gain


def _segment_rms_scale(x, seg_mat):
    n = x.shape[-1]
    parts = []
    for j in range(0, n, seg_mat.shape[0]):
        w = min(seg_mat.shape[0], n - j)
        xs = x[:, j:j + w]
        parts.append(_dot(_bf16(xs * xs), seg_mat[:w, :w]))
    ms = parts[0] if len(parts) == 1 else jnp.concatenate(parts, axis=-1)
    return lax.rsqrt(ms + EPS)


def _rope_block(x, cos, sin_a, sin_b):
    return x * cos + pltpu.roll(x, LANES - 8, 1) * sin_a + pltpu.roll(x, 8, 1) * sin_b


def _proj_rows(rows, x_ref, gmix_ref, win_ref, gqa_ref, wqb_ref, gkva_ref, wkc_ref, wvt_ref,
               gq_ref, gkc_ref, gkpe_ref, gnaq_ref, gnak_ref,
               mq_ref, mkpe_ref, mna_ref, qshift_ref, kone_ref,
               cq_ref, sqa_ref, sqb_ref, ck_ref, ska_ref, skb_ref,
               q_out, k_out, vt_out, naq_out, nak_out, nav_out):
    h = _bf16(_rms_rows(x_ref[rows, :], gmix_ref[...]))
    proj = _dot(h, win_ref[...])

    cqn = _bf16(_rms_rows(proj[:, C_CQ:C_CKV], gqa_ref[...]))
    q = _dot(cqn, wqb_ref[...])
    q = q * _segment_rms_scale(q, mq_ref[...]) * gq_ref[...]
    cosq, sqa, sqb = cq_ref[rows, :], sqa_ref[rows, :], sqb_ref[rows, :]
    for hd in range(MLA_HEADS):
        sl = slice(hd * HEAD_PAD, (hd + 1) * HEAD_PAD)
        q_out[rows, sl] = _bf16(_rope_block(q[:, sl], cosq, sqa, sqb) + qshift_ref[:, sl])

    ckvn = _bf16(_rms_rows(proj[:, C_CKV:C_KPE], gkva_ref[...]))
    kc = _dot(ckvn, wkc_ref[...])
    kc = kc * _segment_rms_scale(kc, mna_ref[...]) * gkc_ref[...]
    vt_out[:, rows] = _bf16(_dot_nt(wvt_ref[...], ckvn))
    kp = proj[:, C_KPE:C_NAQ]
    kp = kp * lax.rsqrt(_dot(_bf16(kp * kp), mkpe_ref[...]) + EPS) * gkpe_ref[...]
    kp = _rope_block(kp, ck_ref[rows, :], ska_ref[rows, :], skb_ref[rows, :])
    tail = pltpu.roll(kp, MLA_NOPE, 1) + kone_ref[...]
    lane = lax.broadcasted_iota(jnp.int32, tail.shape, 1)
    for pr in range(MLA_HEADS // 2):
        two = kc[:, pr * LANES:(pr + 1) * LANES]
        k_out[rows, (2 * pr) * HEAD_PAD:(2 * pr + 1) * HEAD_PAD] = _bf16(jnp.where(lane < MLA_NOPE, two, tail))
        k_out[rows, (2 * pr + 1) * HEAD_PAD:(2 * pr + 2) * HEAD_PAD] = _bf16(
            jnp.where(lane < MLA_NOPE, pltpu.roll(two, MLA_NOPE, 1), tail))

    naq = proj[:, C_NAQ:C_NAK]
    naq_out[rows, :] = _bf16(naq * _segment_rms_scale(naq, mna_ref[...]) * gnaq_ref[...])
    nak = proj[:, C_NAK:C_NAV]
    nak_out[rows, :] = _bf16(nak * _segment_rms_scale(nak, mna_ref[...]) * gnak_ref[...])
    nav_out[rows, :] = _bf16(proj[:, C_NAV:PROJ_COLS])


def _proj_kernel(*refs):
    sub = TM_PROJ // PROJ_SUBBLOCKS
    for sb in range(PROJ_SUBBLOCKS):
        _proj_rows(slice(sb * sub, (sb + 1) * sub), *refs)


def _proj_call(x2d, sw, l, consts, seq):
    t = x2d.shape[0]
    tm = TM_PROJ
    n_seq_tiles = seq // tm
    row = lambda i: (i, 0)
    tab = lambda i: (i % n_seq_tiles, 0)
    layered = ("g_mix", "w_in", "g_q_a", "w_qb", "g_kv_a", "w_kc", "w_vt", "g_q", "g_kc", "g_kpe", "g_naq", "g_nak")
    weights = [sw[n] for n in layered] + [consts["m_q"], consts["m_kpe"], consts["m_na"], sw["q_shift"], consts["k_one"]]
    specs = ([_layer(sw[n], l) for n in layered] + [_full(consts[n].shape) for n in ("m_q", "m_kpe", "m_na")]
             + [_layer(sw["q_shift"], l), _full(consts["k_one"].shape)])
    tables = [consts["cos_q"], consts["sin_qa"], consts["sin_qb"], consts["cos_k"], consts["sin_ka"], consts["sin_kb"]]
    in_specs = [pl.BlockSpec((tm, D_MODEL), row)] + specs + [pl.BlockSpec((tm, LANES), tab) for _ in tables]
    out_shapes = [jax.ShapeDtypeStruct((t, MLA_PAD_WIDTH), jnp.bfloat16),
                  jax.ShapeDtypeStruct((t, MLA_PAD_WIDTH), jnp.bfloat16),
                  jax.ShapeDtypeStruct((MLA_WIDTH, t), jnp.bfloat16),
                  jax.ShapeDtypeStruct((t, NA_WIDTH), jnp.bfloat16),
                  jax.ShapeDtypeStruct((t, NA_WIDTH), jnp.bfloat16),
                  jax.ShapeDtypeStruct((t, NA_WIDTH), jnp.bfloat16)]
    out_specs = [pl.BlockSpec((MLA_WIDTH, tm), lambda i: (0, i)) if s.shape[0] == MLA_WIDTH
                 else pl.BlockSpec((tm, s.shape[1]), row) for s in out_shapes]
    return pl.pallas_call(
        _proj_kernel, grid=(t // tm,), in_specs=in_specs, out_specs=out_specs, out_shape=out_shapes,
        compiler_params=_cparams(("parallel",)), name="proj",
    )(x2d, *weights, *tables)


def _mla_kernel(q_ref, k_ref, vt_ref, o_ref, *, use_max):
    scores = []
    for hh in range(MLA_HEADS_PER_STEP):
        sl = slice(hh * HEAD_PAD, (hh + 1) * HEAD_PAD)
        scores.append(_dot_nt(k_ref[:, sl], q_ref[:, sl]))
    outs = []
    for hh, s in enumerate(scores):
        p = jnp.exp2(s - jnp.max(s, axis=0, keepdims=True)) if use_max else jnp.exp2(s)
        l = jnp.sum(p, axis=0, keepdims=True)
        pair = slice((hh // 2) * LANES, (hh // 2 + 1) * LANES)
        outs.append(_dot(vt_ref[pair, :], _bf16(p)) * (1.0 / l))
    row = lax.broadcasted_iota(jnp.int32, outs[0].shape, 0)
    for pp in range(MLA_HEADS_PER_STEP // 2):
        o_t = jnp.where(row < MLA_V, outs[2 * pp], outs[2 * pp + 1])
        o_ref[:, pp * LANES:(pp + 1) * LANES] = _bf16(o_t.T)


def _mla_call(q, k, vt, *, batch, seq, use_max):
    t = q.shape[0]
    tq = TQ_MLA
    nq = seq // tq
    hps = MLA_HEADS_PER_STEP
    return pl.pallas_call(
        functools.partial(_mla_kernel, use_max=use_max), grid=(batch, MLA_HEADS // hps, nq),
        in_specs=[pl.BlockSpec((tq, hps * HEAD_PAD), lambda b, p, i: (b * nq + i, p)),
                  pl.BlockSpec((seq, hps * HEAD_PAD), lambda b, p, i: (b, p)),
                  pl.BlockSpec((hps * MLA_V, seq), lambda b, p, i: (p, b))],
        out_specs=pl.BlockSpec((tq, hps * MLA_V), lambda b, p, i: (b * nq + i, p)),
        out_shape=jax.ShapeDtypeStruct((t, MLA_WIDTH), jnp.bfloat16),
        compiler_params=_cparams(("parallel", "parallel", "arbitrary")), name="mla_attn",
    )(q, k, vt)


def _na_body(q_ref, k_ref, v_ref, bias_ref, o_ref, *, rows, kr, use_max):
    step = pl.program_id(1)
    band = kr * GRID_W
    lane = lax.broadcasted_iota(jnp.int32, (GRID_W, LANES), 1)
    units = []
    for rr in range(NA_ROWS_PER_STEP):
        r = step * NA_ROWS_PER_STEP + rr
        rs = jnp.clip(r - kr // 2, 0, rows - kr)
        rel0 = rs - r + (NA_KR_MAX - 1)
        start = pl.multiple_of(rs * GRID_W, GRID_W)
        qrow = q_ref[rr * GRID_W:(rr + 1) * GRID_W, :]
        for pr in range(NA_HEADS // 2):
            sl = slice(pr * LANES, (pr + 1) * LANES)
            q2 = qrow[:, sl]
            zero = jnp.zeros_like(q2)
            qq = jnp.concatenate([jnp.where(lane < NA_DH, q2, zero), jnp.where(lane >= NA_DH, q2, zero)], axis=0)
            s = _dot_nt(qq, k_ref[pl.ds(start, band), sl])
            units.append((rr, pr, start, rel0, s))
    probs = []
    for rr, pr, start, rel0, s in units:
        bias = jnp.concatenate([jnp.concatenate([bias_ref[2 * pr + hh, rel0 + 2 * m] for m in range(kr // 2)], axis=-1)
                                for hh in range(2)], axis=0)
        s = s + bias
        p = jnp.exp2(s - jnp.max(s, axis=-1, keepdims=True)) if use_max else jnp.exp2(s)
        l = jnp.sum(p, axis=-1, keepdims=True)
        probs.append((_bf16(p), 1.0 / l))
    for (rr, pr, start, rel0, s), (p, rl) in zip(units, probs):
        sl = slice(pr * LANES, (pr + 1) * LANES)
        pv = _dot(p, v_ref[pl.ds(start, band), sl]) * rl
        o_ref[rr * GRID_W:(rr + 1) * GRID_W, sl] = _bf16(jnp.where(lane < NA_DH, pv[:GRID_W], pv[GRID_W:]))


def _na_kernel(fast_ref, *refs, rows, kr):
    @pl.when(fast_ref[0] != 0)
    def _():
        _na_body(*refs, rows=rows, kr=kr, use_max=False)

    @pl.when(fast_ref[0] == 0)
    def _():
        _na_body(*refs, rows=rows, kr=kr, use_max=True)


def _na_call(fast, q, k, v, bias, *, l, batch, seq):
    t = q.shape[0]
    rows = seq // GRID_W
    kr = min(NA_KR_MAX, rows)
    steps = rows // NA_ROWS_PER_STEP
    blk = NA_ROWS_PER_STEP * GRID_W
    grid_spec = pltpu.PrefetchScalarGridSpec(
        num_scalar_prefetch=1, grid=(batch, steps),
        in_specs=[pl.BlockSpec((blk, NA_WIDTH), lambda b, i, f: (b * steps + i, 0)),
                  pl.BlockSpec((seq, NA_WIDTH), lambda b, i, f: (b, 0)),
                  pl.BlockSpec((seq, NA_WIDTH), lambda b, i, f: (b, 0)),
                  _layer(bias, l)],
        out_specs=pl.BlockSpec((blk, NA_WIDTH), lambda b, i, f: (b * steps + i, 0)))
    return pl.pallas_call(
        functools.partial(_na_kernel, rows=rows, kr=kr), grid_spec=grid_spec,
        out_shape=jax.ShapeDtypeStruct((t, NA_WIDTH), jnp.bfloat16),
        compiler_params=_cparams(("parallel", "arbitrary")), name="na_attn",
    )(fast, q, k, v, bias)


def _mix_rows(rows, x_ref, oa_ref, ob_ref, ga_ref, gb_ref, wout_ref, gffn_ref, wr_ref, br_ref, x1_out, h2_out):
    mixed = jnp.concatenate([_rms_rows(oa_ref[rows, :].astype(jnp.float32), ga_ref[...]),
                             _rms_rows(ob_ref[rows, :].astype(jnp.float32), gb_ref[...])], axis=-1)
    x1 = x_ref[rows, :] + _dot(_bf16(mixed), wout_ref[...])
    x1_out[rows, :] = x1
    hi = _bf16(_rms_rows(x1, gffn_ref[...]))
    h2_out[rows, :D_MODEL] = hi
    logits = _dot(hi, wr_ref[...]) + br_ref[...]

    lane = lax.broadcasted_iota(jnp.int32, logits.shape, 1).astype(jnp.float32)
    neg = jnp.float32(-jnp.inf)
    big = jnp.float32(LANES)
    is_grp = lane < N_GROUPS
    lg = jnp.where(is_grp, logits, neg)
    gmax = jnp.max(lg, axis=-1, keepdims=True)
    gid = jnp.min(jnp.where(lg == gmax, lane, big), axis=-1, keepdims=True)
    pg_top = 1.0 / jnp.sum(jnp.where(is_grp, jnp.exp(logits - gmax), 0.0), axis=-1, keepdims=True)
    base = N_GROUPS + EXPERTS_PER_GROUP * gid
    in_sel = (lane >= base) & (lane < base + EXPERTS_PER_GROUP)
    le = jnp.where(in_sel, logits, neg)
    m1 = jnp.max(le, axis=-1, keepdims=True)
    i1 = jnp.min(jnp.where(le == m1, lane, big), axis=-1, keepdims=True)
    le2 = jnp.where(lane == i1, neg, le)
    m2 = jnp.max(le2, axis=-1, keepdims=True)
    i2 = jnp.min(jnp.where(le2 == m2, lane, big), axis=-1, keepdims=True)
    e2 = jnp.exp(m2 - m1)
    w1 = pg_top / (1.0 + e2)
    w2 = pg_top * e2 / (1.0 + e2)
    j1 = i1 - base + ROUTE_W0
    j2 = i2 - base + ROUTE_W0
    w1h = _bf16(w1).astype(jnp.float32)
    w2h = _bf16(w2).astype(jnp.float32)
    rec = jnp.where(lane == ROUTE_GID, gid,
                    jnp.where(lane == j1, w1h, jnp.where(lane == j2, w2h,
                    jnp.where(lane == j1 + ROUTE_LO, w1 - w1h, jnp.where(lane == j2 + ROUTE_LO, w2 - w2h, 0.0)))))
    h2_out[rows, D_MODEL:] = _bf16(rec)


def _mix_kernel(*refs):
    sub = TM_PROJ // MIX_SUBBLOCKS
    for sb in range(MIX_SUBBLOCKS):
        _mix_rows(slice(sb * sub, (sb + 1) * sub), *refs)


def _mix_call(x2d, oa, ob, sw, l):
    t = x2d.shape[0]
    tm = TM_PROJ
    row = lambda i: (i, 0)
    weights = [sw[n] for n in ("g_mla_out", "g_na_out", "w_out", "g_ffn", "w_r", "b_r")]
    return pl.pallas_call(
        _mix_kernel, grid=(t // tm,),
        in_specs=[pl.BlockSpec((tm, D_MODEL), row), pl.BlockSpec((tm, MLA_WIDTH), row),
                  pl.BlockSpec((tm, NA_WIDTH), row)] + [_layer(w, l) for w in weights],
        out_specs=[pl.BlockSpec((tm, D_MODEL), row), pl.BlockSpec((tm, H2_COLS), row)],
        out_shape=[jax.ShapeDtypeStruct((t, D_MODEL), jnp.float32),
                   jax.ShapeDtypeStruct((t, H2_COLS), jnp.bfloat16)],
        compiler_params=_cparams(("parallel",)), name="mix_router",
    )(x2d, oa, ob, *weights)


def _moe_partition(sub, h2_ref, xs_ref, ys_ref, cnt_ref):
    tm = TM_MOE
    rows = slice(sub * tm, (sub + 1) * tm)
    route = h2_ref[rows, D_MODEL:].astype(jnp.float32)
    lane = lax.broadcasted_iota(jnp.int32, route.shape, 1)
    gid = jnp.sum(jnp.where(lane == ROUTE_GID, route, 0.0), axis=-1, keepdims=True)
    onehot = jnp.where((lane.astype(jnp.float32) == gid) & (lane < N_GROUPS), 1.0, 0.0)
    r_i = lax.broadcasted_iota(jnp.int32, (RANK_BLOCK, RANK_BLOCK), 0)
    c_i = lax.broadcasted_iota(jnp.int32, (RANK_BLOCK, RANK_BLOCK), 1)
    tri = jnp.where(c_i < r_i, 1.0, 0.0).astype(jnp.bfloat16)
    ranks = []
    sizes = jnp.zeros((1, LANES), jnp.float32)
    for blk in range(tm // RANK_BLOCK):
        oh = onehot[blk * RANK_BLOCK:(blk + 1) * RANK_BLOCK]
        ranks.append(_dot(tri, _bf16(oh)) + sizes)
        sizes = sizes + jnp.sum(oh, axis=0, keepdims=True)
    rank = jnp.concatenate(ranks, axis=0)
    lane1 = lane[0:1]
    start_vec = jnp.zeros((1, LANES), jnp.float32)
    acc = jnp.int32(0)
    for g in range(N_GROUPS):
        n = jnp.sum(jnp.where(lane1 == g, sizes, 0.0)).astype(jnp.int32)
        cnt_ref[sub * N_GROUPS + g] = acc
        cnt_ref[MOE_SUBTILES * N_GROUPS + sub * N_GROUPS + g] = acc + n
        start_vec = jnp.where(lane1 == g, acc.astype(jnp.float32), start_vec)
        acc = acc + n
    dest = jnp.sum(onehot * (start_vec + rank), axis=-1, keepdims=True)
    dest_row = jnp.transpose(jnp.broadcast_to(dest, (tm, LANES)))[0:1, :]
    rowi = lax.broadcasted_iota(jnp.int32, (tm, tm), 0)
    perm = jnp.where(rowi == dest_row.astype(jnp.int32), 1.0, 0.0).astype(jnp.bfloat16)
    xs_ref[sub, :tm, :] = _bf16(_dot(perm, h2_ref[rows, :]))
    xs_ref[sub, tm:, :] = jnp.zeros((MOE_ROWS - tm, H2_COLS), jnp.bfloat16)
    ys_ref[sub] = jnp.zeros((MOE_ROWS, D_MODEL), jnp.bfloat16)
    return dest


def _moe_experts(sub, g, off, size, start_g, end_g, xs_ref, ys_ref, wg_ref, wu_ref, wd_ref):
    off = pl.multiple_of(off, BF16_ROWS)
    xc = xs_ref[sub, pl.ds(off, size), :D_MODEL]
    cw = xs_ref[sub, pl.ds(off, size), D_MODEL:].astype(jnp.float32)
    hid = []
    for j in range(EXPERTS_PER_GROUP):
        a = _dot(xc, wg_ref[g * EXPERTS_PER_GROUP + j])
        u = _dot(xc, wu_ref[g * EXPERTS_PER_GROUP + j])
        cj = cw[:, ROUTE_W0 + j:ROUTE_W0 + j + 1] + cw[:, ROUTE_W0 + ROUTE_LO + j:ROUTE_W0 + ROUTE_LO + j + 1]
        hid.append(a * (1.0 / (1.0 + jnp.exp(-a))) * u * cj)
    hid = _bf16(jnp.concatenate(hid, axis=-1))
    w_down = wd_ref[pl.ds(g * EXPERTS_PER_GROUP, EXPERTS_PER_GROUP)].reshape(EXPERTS_PER_GROUP * D_EXPERT, D_MODEL)
    rows = off + lax.broadcasted_iota(jnp.int32, (size, 1), 0)
    mine = (rows >= start_g) & (rows < end_g)
    old = ys_ref[sub, pl.ds(off, size), :].astype(jnp.float32)
    ys_ref[sub, pl.ds(off, size), :] = _bf16(jnp.where(mine, _dot(hid, w_down), old))


def _moe_kernel(x1_ref, h2_ref, wg_ref, wu_ref, wd_ref, o_ref, xs_ref, ys_ref, cnt_ref):
    tm = TM_MOE
    dests = [_moe_partition(sub, h2_ref, xs_ref, ys_ref, cnt_ref) for sub in range(MOE_SUBTILES)]

    def _run(u, carry):
        sub = lax.shift_right_logical(u, 2)
        g = lax.bitwise_and(u, N_GROUPS - 1)
        start_g = cnt_ref[u]
        end_g = cnt_ref[MOE_SUBTILES * N_GROUPS + u]
        first = lax.shift_left(lax.shift_right_logical(start_g, 4), 4)
        span = end_g - first
        args = (xs_ref, ys_ref, wg_ref, wu_ref, wd_ref)
        n_full = lax.shift_right_logical(span, MOE_LOOP_CHUNK.bit_length() - 1)

        def _chunk(c, inner):
            _moe_experts(sub, g, first + c * MOE_LOOP_CHUNK, MOE_LOOP_CHUNK, start_g, end_g, *args)
            return inner
        lax.fori_loop(0, n_full, _chunk, 0)
        tail = first + n_full * MOE_LOOP_CHUNK
        rest = end_g - tail
        lo = 0
        for size in MOE_TAIL_CHUNKS:
            @pl.when((rest > lo) & (rest <= size))
            def _(size=size):
                _moe_experts(sub, g, tail, size, start_g, end_g, *args)
            lo = size
        return carry

    lax.fori_loop(0, MOE_SUBTILES * N_GROUPS, _run, 0)

    col = lax.broadcasted_iota(jnp.int32, (tm, tm), 1)
    for sub, dest in enumerate(dests):
        rows = slice(sub * tm, (sub + 1) * tm)
        unperm = jnp.where(col == dest.astype(jnp.int32), 1.0, 0.0).astype(jnp.bfloat16)
        o_ref[rows, :] = x1_ref[rows, :] + _dot(unperm, ys_ref[sub, :tm, :])


def _moe_call(x1, h2, sw, l):
    t = x1.shape[0]
    tm = TM_MOE * MOE_SUBTILES
    row = lambda i: (i, 0)
    resident = lambda w: _layer(w, l, pipeline_mode=pl.Buffered(1))
    return pl.pallas_call(
        _moe_kernel, grid=(t // tm,),
        in_specs=[pl.BlockSpec((tm, D_MODEL), row), pl.BlockSpec((tm, H2_COLS), row),
                  resident(sw["w_g"]), resident(sw["w_u"]), resident(sw["w_d"])],
        out_specs=pl.BlockSpec((tm, D_MODEL), row),
        out_shape=jax.ShapeDtypeStruct((t, D_MODEL), jnp.float32),
        scratch_shapes=[pltpu.VMEM((MOE_SUBTILES, MOE_ROWS, H2_COLS), jnp.bfloat16),
                        pltpu.VMEM((MOE_SUBTILES, MOE_ROWS, D_MODEL), jnp.bfloat16),
                        pltpu.SMEM((2 * MOE_SUBTILES * N_GROUPS,), jnp.int32)],
        compiler_params=_cparams(("parallel",)), name="moe",
    )(x1, h2, sw["w_g"], sw["w_u"], sw["w_d"])


def _segment_matrix(width, segments):
    m = np.zeros((width, width), np.float32)
    for lo, hi in segments:
        m[lo:hi, lo:hi] = 1.0 / (hi - lo)
    return jnp.asarray(m, jnp.bfloat16)


def _constants(seq):
    t = np.arange(seq)
    row = (t // GRID_W).astype(np.float32)
    col = (t % GRID_W).astype(np.float32)
    n_freq = MLA_ROPE // 4
    inv = (np.float32(ROPE_THETA) ** (-np.arange(n_freq, dtype=np.float32) / n_freq)).astype(np.float32)
    ang_r = row[:, None] * inv[None, :]
    ang_c = col[:, None] * inv[None, :]

    def tables(base):
        cos = np.zeros((seq, LANES), np.float32)
        cos[:, :base] = 1.0
        sa = np.zeros((seq, LANES), np.float32)
        sb = np.zeros((seq, LANES), np.float32)
        for k, ang in enumerate((ang_r, ang_c)):
            o = base + 2 * n_freq * k
            cos[:, o:o + n_freq] = np.cos(ang)
            cos[:, o + n_freq:o + 2 * n_freq] = np.cos(ang)
            sa[:, o:o + n_freq] = -np.sin(ang)
            sb[:, o + n_freq:o + 2 * n_freq] = np.sin(ang)
        return jnp.asarray(cos), jnp.asarray(sa), jnp.asarray(sb)

    cos_q, sin_qa, sin_qb = tables(MLA_NOPE)
    cos_k, sin_ka, sin_kb = tables(0)
    seg_q = []
    for hb in range(0, MXU_TILE, HEAD_PAD):
        seg_q += [(hb, hb + MLA_NOPE), (hb + MLA_NOPE, hb + MLA_QK)]
    shift_lanes = np.zeros((1, MLA_PAD_WIDTH), np.float32)
    shift_lanes[0, SHIFT_LANE::HEAD_PAD] = 1.0
    return dict(cos_q=cos_q, sin_qa=sin_qa, sin_qb=sin_qb, cos_k=cos_k, sin_ka=sin_ka, sin_kb=sin_kb,
                m_q=_segment_matrix(MXU_TILE, seg_q),
                m_kpe=_segment_matrix(LANES, [(0, MLA_ROPE)]),
                m_na=_segment_matrix(MXU_TILE, [(o, o + NA_DH) for o in range(0, MXU_TILE, NA_DH)]),
                shift_lanes=jnp.asarray(shift_lanes), k_one=jnp.asarray(shift_lanes[:, :HEAD_PAD]))


def _na_bias_tables(rpb, shift):
    cols = np.arange(GRID_W)
    cs = np.clip(cols - NA_KC // 2, 0, GRID_W - NA_KC)
    col_mask = (cols[None, :] >= cs[:, None]) & (cols[None, :] < cs[:, None] + NA_KC)
    dc = np.clip(cols[None, :] - cols[:, None], -(NA_KC - 1), NA_KC - 1) + NA_KC - 1
    onehot = (dc[None] == np.arange(2 * NA_KC - 1)[:, None, None]).astype(np.float32)
    rpb_c = jnp.einsum("lhdj,jqk->lhdqk", rpb, jnp.asarray(onehot), precision=lax.Precision.HIGHEST)
    rpb_c = jnp.where(jnp.asarray(col_mask), rpb_c * LOG2E - shift[:, None, None, None, None], MASK_NEG)
    return jnp.concatenate([rpb_c[:, :, :-1], rpb_c[:, :, 1:]], axis=-1)


def _softmax_shifts(p):
    amax = lambda v: jnp.max(jnp.abs(v), axis=-1)
    gq, gk = p["g_mla_q"], p["g_mla_k"]
    qn = (MLA_QK ** -0.5 * LOG2E) * jnp.sqrt(MLA_NOPE * amax(gq[:, :MLA_NOPE]) ** 2 + MLA_ROPE * amax(gq[:, MLA_NOPE:]) ** 2)
    kn = jnp.sqrt(MLA_NOPE * amax(gk[:, :MLA_NOPE]) ** 2 + MLA_ROPE * amax(gk[:, MLA_NOPE:]) ** 2)
    b_mla = BOUND_SLACK * qn * kn
    mla_fast = 2.0 * b_mla <= MAX_SHIFT_GAP
    b_na = BOUND_SLACK * (NA_DH ** -0.5 * LOG2E) * NA_DH * amax(p["g_na_q"]) * amax(p["g_na_k"])
    bias_hi = jnp.max(p["na_rpb"], axis=(1, 2, 3)) * LOG2E
    bias_lo = jnp.min(p["na_rpb"], axis=(1, 2, 3)) * LOG2E
    na_fast = 2.0 * b_na + (bias_hi - bias_lo) <= MAX_SHIFT_GAP
    return (mla_fast, jnp.where(mla_fast, b_mla, 0.0)), (na_fast, jnp.where(na_fast, b_na + bias_hi, 0.0))


def _prep_weights(p, consts):
    bf = jnp.bfloat16
    nl = p["w_in"].shape[0]
    (mla_fast, mla_shift), (na_fast, na_shift) = _softmax_shifts(p)
    w_in = p["w_in"]
    split = Q_LORA + KV_LORA + MLA_ROPE
    w_in_p = jnp.zeros((nl, D_MODEL, PROJ_COLS), bf)
    w_in_p = w_in_p.at[:, :, :split].set(w_in[:, :, :split].astype(bf)).at[:, :, C_NAQ:].set(w_in[:, :, split:].astype(bf))
    w_qb = jnp.pad(p["w_q_b"].reshape(nl, Q_LORA, MLA_HEADS, MLA_QK), ((0, 0), (0, 0), (0, 0), (0, HEAD_PAD - MLA_QK)))
    w_kv = p["w_kv_b"].reshape(nl, KV_LORA, MLA_HEADS, MLA_NOPE + MLA_V)
    w_kc = w_kv[..., :MLA_NOPE].reshape(nl, KV_LORA, MLA_HEADS * MLA_NOPE)
    w_vt = w_kv[..., MLA_NOPE:].reshape(nl, KV_LORA, MLA_WIDTH).transpose(0, 2, 1)
    g_q = jnp.tile(jnp.pad(p["g_mla_q"] * (MLA_QK ** -0.5 * LOG2E), ((0, 0), (0, HEAD_PAD - MLA_QK))), (1, MLA_HEADS))
    w_r = jnp.concatenate([p["w_router_group"], p["w_router_expert"]], axis=2)
    w_r = jnp.pad(w_r, ((0, 0), (0, 0), (0, LANES - w_r.shape[2])))
    b_r = jnp.concatenate([p["b_router_group"], p["b_router_expert"]], axis=1)
    b_r = jnp.pad(b_r, ((0, 0), (0, LANES - b_r.shape[1])))
    row = lambda v: v[:, None, :]
    return dict(
        g_mix=row(p["g_mix_norm"]), w_in=w_in_p,
        g_q_a=row(p["g_q_a"]), w_qb=w_qb.reshape(nl, Q_LORA, MLA_PAD_WIDTH).astype(bf),
        g_kv_a=row(p["g_kv_a"]), w_kc=w_kc.astype(bf), w_vt=w_vt.astype(bf),
        g_q=row(g_q), g_kc=row(jnp.tile(p["g_mla_k"][:, :MLA_NOPE], (1, MLA_HEADS))),
        g_kpe=row(jnp.pad(p["g_mla_k"][:, MLA_NOPE:], ((0, 0), (0, LANES - MLA_ROPE)))),
        g_naq=row(jnp.tile(p["g_na_q"] * (NA_DH ** -0.5 * LOG2E), (1, NA_HEADS))),
        g_nak=row(jnp.tile(p["g_na_k"], (1, NA_HEADS))),
        na_bias=_na_bias_tables(p["na_rpb"], na_shift), na_fast=na_fast,
        q_shift=-mla_shift[:, None, None] * consts["shift_lanes"][None], mla_fast=mla_fast,
        g_mla_out=row(p["g_mla_out"]), g_na_out=row(p["g_na_out"]),
        w_out=p["w_out"].astype(bf), g_ffn=row(p["g_ffn_norm"]),
        w_r=w_r.astype(bf), b_r=row(b_r),
        w_g=p["w_gate"].astype(bf), w_u=p["w_up"].astype(bf), w_d=p["w_down"].astype(bf))


def kernel(x, g_mix_norm, w_in, g_q_a, w_q_b, g_kv_a, w_kv_b, g_mla_q, g_mla_k, g_na_q, g_na_k, na_rpb, g_mla_out, g_na_out, w_out, g_ffn_norm, w_router_group, b_router_group, w_router_expert, b_router_expert, w_gate, w_up, w_down):
    batch, seq, d = x.shape
    assert d == D_MODEL and seq % TM_PROJ == 0 and seq % TQ_MLA == 0 and (batch * seq) % (TM_MOE * MOE_SUBTILES) == 0
    rows = seq // GRID_W
    assert rows % NA_ROWS_PER_STEP == 0 and rows >= NA_KR_MAX and NA_KR_MAX % 2 == 0
    p = dict(g_mix_norm=g_mix_norm, w_in=w_in, g_q_a=g_q_a, w_q_b=w_q_b, g_kv_a=g_kv_a, w_kv_b=w_kv_b,
             g_mla_q=g_mla_q, g_mla_k=g_mla_k, g_na_q=g_na_q, g_na_k=g_na_k, na_rpb=na_rpb,
             g_mla_out=g_mla_out, g_na_out=g_na_out, w_out=w_out, g_ffn_norm=g_ffn_norm,
             w_router_group=w_router_group, b_router_group=b_router_group,
             w_router_expert=w_router_expert, b_router_expert=b_router_expert,
             w_gate=w_gate, w_up=w_up, w_down=w_down)
    consts = _constants(seq)
    sw = _prep_weights(p, consts)
    xf = x.reshape(batch * seq, d)
    for l in range(w_in.shape[0]):
        q, k, vt, naq, nak, nav = _proj_call(xf, sw, l, consts, seq)
        o_a = lax.cond(sw["mla_fast"][l],
                       functools.partial(_mla_call, batch=batch, seq=seq, use_max=False),
                       functools.partial(_mla_call, batch=batch, seq=seq, use_max=True), q, k, vt)
        o_b = _na_call(sw["na_fast"][l:l + 1].astype(jnp.int32), naq, nak, nav, sw["na_bias"], l=l, batch=batch, seq=seq)
        x1, h2 = _mix_call(xf, o_a, o_b, sw, l)
        xf = _moe_call(x1, h2, sw, l)
    return xf.reshape(batch, seq, d)
```

```python
import functools

import numpy as np
import jax
import jax.numpy as jnp
from jax import lax
from jax.experimental import pallas as pl
from jax.experimental.pallas import tpu as pltpu

D_MODEL = 1024
GRID_W = 64
MLA_HEADS = 8
MLA_NOPE = 64
MLA_ROPE = 32
MLA_V = 64
MLA_QK = MLA_NOPE + MLA_ROPE
Q_LORA = 256
KV_LORA = 128
MLA_WIDTH = MLA_HEADS * MLA_V
ROPE_THETA = 10000.0
NA_HEADS = 8
NA_DH = 64
NA_WIDTH = NA_HEADS * NA_DH
NA_KR_MAX = 8
NA_KC = 16
N_GROUPS = 4
EXPERTS_PER_GROUP = 4
D_EXPERT = 256
EPS = 1e-6

LANES = 128
MXU_TILE = 256
BF16_ROWS = 16
HEAD_PAD = LANES
MLA_PAD_WIDTH = MLA_HEADS * HEAD_PAD
PROJ_COLS = 2048
LOG2E = 1.4426950408889634
MASK_NEG = -1e30
SHIFT_LANE = MLA_QK
MAX_SHIFT_GAP = 100.0
BOUND_SLACK = 1.02
VMEM_LIMIT = 56 * 1024 * 1024

C_CQ = 0
C_CKV = C_CQ + Q_LORA
C_KPE = C_CKV + KV_LORA
C_NAQ = C_KPE + LANES
C_NAK = C_NAQ + NA_WIDTH
C_NAV = C_NAK + NA_WIDTH

TM_PROJ = 1024
PROJ_SUBBLOCKS = 4
MIX_SUBBLOCKS = 4
TQ_MLA = 1024
MLA_HEADS_PER_STEP = 4
NA_ROWS_PER_STEP = 16
TM_MOE = 512
MOE_SUBTILES = 2
MOE_LOOP_CHUNK = 256
MOE_TAIL_CHUNKS = (128, 176, 224, 256)
MOE_ROWS = TM_MOE + MOE_LOOP_CHUNK
RANK_BLOCK = 256
ROUTE_GID = 0
ROUTE_W0 = 1
ROUTE_LO = 8
H2_COLS = D_MODEL + LANES

assert N_GROUPS == 4
assert MOE_LOOP_CHUNK & (MOE_LOOP_CHUNK - 1) == 0 and MOE_TAIL_CHUNKS[-1] == MOE_LOOP_CHUNK
assert MLA_NOPE == NA_DH


def _cparams(sem):
    return pltpu.CompilerParams(dimension_semantics=sem, vmem_limit_bytes=VMEM_LIMIT)


def _full(shape):
    nd = len(shape)
    return pl.BlockSpec(shape, lambda *_: (0,) * nd)


def _layer(w, l, **kw):
    nd = w.ndim
    return pl.BlockSpec((None,) + w.shape[1:], lambda *_: (l,) + (0,) * (nd - 1), **kw)


def _bf16(x):
    return x.astype(jnp.bfloat16)


def _dot(a, b):
    return jnp.dot(a, b, preferred_element_type=jnp.float32)


def _dot_nt(a, b):
    return lax.dot_general(a, b, (((1,), (1,)), ((), ())), preferred_element_type=jnp.float32)


def _rms_rows(x, gain):
    ms = jnp.mean(x * x, axis=-1, keepdims=True)
    return x * lax.rsqrt(ms + EPS) * gain


def _segment_rms_scale(x, seg_mat):
    n = x.shape[-1]
    parts = []
    for j in range(0, n, seg_mat.shape[0]):
        w = min(seg_mat.shape[0], n - j)
        xs = x[:, j:j + w]
        parts.append(_dot(_bf16(xs * xs), seg_mat[:w, :w]))
    ms = parts[0] if len(parts) == 1 else jnp.concatenate(parts, axis=-1)
    return lax.rsqrt(ms + EPS)


def _rope_block(x, cos, sin_a, sin_b):
    return x * cos + pltpu.roll(x, LANES - 8, 1) * sin_a + pltpu.roll(x, 8, 1) * sin_b


def _proj_rows(rows, x_ref, gmix_ref, win_ref, gqa_ref, wqb_ref, gkva_ref, wkc_ref, wvt_ref,
               gq_ref, gkc_ref, gkpe_ref, gnaq_ref, gnak_ref,
               mq_ref, mkpe_ref, mna_ref, qshift_ref, kone_ref,
               cq_ref, sqa_ref, sqb_ref, ck_ref, ska_ref, skb_ref,
               q_out, k_out, vt_out, naq_out, nak_out, nav_out):
    h = _bf16(_rms_rows(x_ref[rows, :], gmix_ref[...]))
    proj = _dot(h, win_ref[...])

    cqn = _bf16(_rms_rows(proj[:, C_CQ:C_CKV], gqa_ref[...]))
    q = _dot(cqn, wqb_ref[...])
    q = q * _segment_rms_scale(q, mq_ref[...]) * gq_ref[...]
    cosq, sqa, sqb = cq_ref[rows, :], sqa_ref[rows, :], sqb_ref[rows, :]
    for hd in range(MLA_HEADS):
        sl = slice(hd * HEAD_PAD, (hd + 1) * HEAD_PAD)
        q_out[rows, sl] = _bf16(_rope_block(q[:, sl], cosq, sqa, sqb) + qshift_ref[:, sl])

    ckvn = _bf16(_rms_rows(proj[:, C_CKV:C_KPE], gkva_ref[...]))
    kc = _dot(ckvn, wkc_ref[...])
    kc = kc * _segment_rms_scale(kc, mna_ref[...]) * gkc_ref[...]
    vt_out[:, rows] = _bf16(_dot_nt(wvt_ref[...], ckvn))
    kp = proj[:, C_KPE:C_NAQ]
    kp = kp * lax.rsqrt(_dot(_bf16(kp * kp), mkpe_ref[...]) + EPS) * gkpe_ref[...]
    kp = _rope_block(kp, ck_ref[rows, :], ska_ref[rows, :], skb_ref[rows, :])
    tail = pltpu.roll(kp, MLA_NOPE, 1) + kone_ref[...]
    lane = lax.broadcasted_iota(jnp.int32, tail.shape, 1)
    for pr in range(MLA_HEADS // 2):
        two = kc[:, pr * LANES:(pr + 1) * LANES]
        k_out[rows, (2 * pr) * HEAD_PAD:(2 * pr + 1) * HEAD_PAD] = _bf16(jnp.where(lane < MLA_NOPE, two, tail))
        k_out[rows, (2 * pr + 1) * HEAD_PAD:(2 * pr + 2) * HEAD_PAD] = _bf16(
            jnp.where(lane < MLA_NOPE, pltpu.roll(two, MLA_NOPE, 1), tail))

    naq = proj[:, C_NAQ:C_NAK]
    naq_out[rows, :] = _bf16(naq * _segment_rms_scale(naq, mna_ref[...]) * gnaq_ref[...])
    nak = proj[:, C_NAK:C_NAV]
    nak_out[rows, :] = _bf16(nak * _segment_rms_scale(nak, mna_ref[...]) * gnak_ref[...])
    nav_out[rows, :] = _bf16(proj[:, C_NAV:PROJ_COLS])


def _proj_kernel(*refs):
    sub = TM_PROJ // PROJ_SUBBLOCKS
    for sb in range(PROJ_SUBBLOCKS):
        _proj_rows(slice(sb * sub, (sb + 1) * sub), *refs)


def _proj_call(x2d, sw, l, consts, seq):
    t = x2d.shape[0]
    tm = TM_PROJ
    n_seq_tiles = seq // tm
    row = lambda i: (i, 0)
    tab = lambda i: (i % n_seq_tiles, 0)
    layered = ("g_mix", "w_in", "g_q_a", "w_qb", "g_kv_a", "w_kc", "w_vt", "g_q", "g_kc", "g_kpe", "g_naq", "g_nak")
    weights = [sw[n] for n in layered] + [consts["m_q"], consts["m_kpe"], consts["m_na"], sw["q_shift"], consts["k_one"]]
    specs = ([_layer(sw[n], l) for n in layered] + [_full(consts[n].shape) for n in ("m_q", "m_kpe", "m_na")]
             + [_layer(sw["q_shift"], l), _full(consts["k_one"].shape)])
    tables = [consts["cos_q"], consts["sin_qa"], consts["sin_qb"], consts["cos_k"], consts["sin_ka"], consts["sin_kb"]]
    in_specs = [pl.BlockSpec((tm, D_MODEL), row)] + specs + [pl.BlockSpec((tm, LANES), tab) for _ in tables]
    out_shapes = [jax.ShapeDtypeStruct((t, MLA_PAD_WIDTH), jnp.bfloat16),
                  jax.ShapeDtypeStruct((t, MLA_PAD_WIDTH), jnp.bfloat16),
                  jax.ShapeDtypeStruct((MLA_WIDTH, t), jnp.bfloat16),
                  jax.ShapeDtypeStruct((t, NA_WIDTH), jnp.bfloat16),
                  jax.ShapeDtypeStruct((t, NA_WIDTH), jnp.bfloat16),
                  jax.ShapeDtypeStruct((t, NA_WIDTH), jnp.bfloat16)]
    out_specs = [pl.BlockSpec((MLA_WIDTH, tm), lambda i: (0, i)) if s.shape[0] == MLA_WIDTH
                 else pl.BlockSpec((tm, s.shape[1]), row) for s in out_shapes]
    return pl.pallas_call(
        _proj_kernel, grid=(t // tm,), in_specs=in_specs, out_specs=out_specs, out_shape=out_shapes,
        compiler_params=_cparams(("parallel",)), name="proj",
    )(x2d, *weights, *tables)


def _mla_kernel(q_ref, k_ref, vt_ref, o_ref, *, use_max):
    scores = []
    for hh in range(MLA_HEADS_PER_STEP):
        sl = slice(hh * HEAD_PAD, (hh + 1) * HEAD_PAD)
        scores.append(_dot_nt(k_ref[:, sl], q_ref[:, sl]))
    outs = []
    for hh, s in enumerate(scores):
        p = jnp.exp2(s - jnp.max(s, axis=0, keepdims=True)) if use_max else jnp.exp2(s)
        l = jnp.sum(p, axis=0, keepdims=True)
        pair = slice((hh // 2) * LANES, (hh // 2 + 1) * LANES)
        outs.append(_dot(vt_ref[pair, :], _bf16(p)) * (1.0 / l))
    row = lax.broadcasted_iota(jnp.int32, outs[0].shape, 0)
    for pp in range(MLA_HEADS_PER_STEP // 2):
        o_t = jnp.where(row < MLA_V, outs[2 * pp], outs[2 * pp + 1])
        o_ref[:, pp * LANES:(pp + 1) * LANES] = _bf16(o_t.T)


def _mla_call(q, k, vt, *, batch, seq, use_max):
    t = q.shape[0]
    tq = TQ_MLA
    nq = seq // tq
    hps = MLA_HEADS_PER_STEP
    return pl.pallas_call(
        functools.partial(_mla_kernel, use_max=use_max), grid=(batch, MLA_HEADS // hps, nq),
        in_specs=[pl.BlockSpec((tq, hps * HEAD_PAD), lambda b, p, i: (b * nq + i, p)),
                  pl.BlockSpec((seq, hps * HEAD_PAD), lambda b, p, i: (b, p)),
                  pl.BlockSpec((hps * MLA_V, seq), lambda b, p, i: (p, b))],
        out_specs=pl.BlockSpec((tq, hps * MLA_V), lambda b, p, i: (b * nq + i, p)),
        out_shape=jax.ShapeDtypeStruct((t, MLA_WIDTH), jnp.bfloat16),
        compiler_params=_cparams(("parallel", "parallel", "arbitrary")), name="mla_attn",
    )(q, k, vt)


def _na_body(q_ref, k_ref, v_ref, bias_ref, o_ref, *, rows, kr, use_max):
    step = pl.program_id(1)
    band = kr * GRID_W
    lane = lax.broadcasted_iota(jnp.int32, (GRID_W, LANES), 1)
    units = []
    for rr in range(NA_ROWS_PER_STEP):
        r = step * NA_ROWS_PER_STEP + rr
        rs = jnp.clip(r - kr // 2, 0, rows - kr)
        rel0 = rs - r + (NA_KR_MAX - 1)
        start = pl.multiple_of(rs * GRID_W, GRID_W)
        qrow = q_ref[rr * GRID_W:(rr + 1) * GRID_W, :]
        for pr in range(NA_HEADS // 2):
            sl = slice(pr * LANES, (pr + 1) * LANES)
            q2 = qrow[:, sl]
            zero = jnp.zeros_like(q2)
            qq = jnp.concatenate([jnp.where(lane < NA_DH, q2, zero), jnp.where(lane >= NA_DH, q2, zero)], axis=0)
            s = _dot_nt(qq, k_ref[pl.ds(start, band), sl])
            units.append((rr, pr, start, rel0, s))
    probs = []
    for rr, pr, start, rel0, s in units:
        bias = jnp.concatenate([jnp.concatenate([bias_ref[2 * pr + hh, rel0 + 2 * m] for m in range(kr // 2)], axis=-1)
                                for hh in range(2)], axis=0)
        s = s + bias
        p = jnp.exp2(s - jnp.max(s, axis=-1, keepdims=True)) if use_max else jnp.exp2(s)
        l = jnp.sum(p, axis=-1, keepdims=True)
        probs.append((_bf16(p), 1.0 / l))
    for (rr, pr, start, rel0, s), (p, rl) in zip(units, probs):
        sl = slice(pr * LANES, (pr + 1) * LANES)
        pv = _dot(p, v_ref[pl.ds(start, band), sl]) * rl
        o_ref[rr * GRID_W:(rr + 1) * GRID_W, sl] = _bf16(jnp.where(lane < NA_DH, pv[:GRID_W], pv[GRID_W:]))


def _na_kernel(fast_ref, *refs, rows, kr):
    @pl.when(fast_ref[0] != 0)
    def _():
        _na_body(*refs, rows=rows, kr=kr, use_max=False)

    @pl.when(fast_ref[0] == 0)
    def _():
        _na_body(*refs, rows=rows, kr=kr, use_max=True)


def _na_call(fast, q, k, v, bias, *, l, batch, seq):
    t = q.shape[0]
    rows = seq // GRID_W
    kr = min(NA_KR_MAX, rows)
    steps = rows // NA_ROWS_PER_STEP
    blk = NA_ROWS_PER_STEP * GRID_W
    grid_spec = pltpu.PrefetchScalarGridSpec(
        num_scalar_prefetch=1, grid=(batch, steps),
        in_specs=[pl.BlockSpec((blk, NA_WIDTH), lambda b, i, f: (b * steps + i, 0)),
                  pl.BlockSpec((seq, NA_WIDTH), lambda b, i, f: (b, 0)),
                  pl.BlockSpec((seq, NA_WIDTH), lambda b, i, f: (b, 0)),
                  _layer(bias, l)],
        out_specs=pl.BlockSpec((blk, NA_WIDTH), lambda b, i, f: (b * steps + i, 0)))
    return pl.pallas_call(
        functools.partial(_na_kernel, rows=rows, kr=kr), grid_spec=grid_spec,
        out_shape=jax.ShapeDtypeStruct((t, NA_WIDTH), jnp.bfloat16),
        compiler_params=_cparams(("parallel", "arbitrary")), name="na_attn",
    )(fast, q, k, v, bias)


def _mix_rows(rows, x_ref, oa_ref, ob_ref, ga_ref, gb_ref, wout_ref, gffn_ref, wr_ref, br_ref, x1_out, h2_out):
    mixed = jnp.concatenate([_rms_rows(oa_ref[rows, :].astype(jnp.float32), ga_ref[...]),
                             _rms_rows(ob_ref[rows, :].astype(jnp.float32), gb_ref[...])], axis=-1)
    x1 = x_ref[rows, :] + _dot(_bf16(mixed), wout_ref[...])
    x1_out[rows, :] = x1
    hi = _bf16(_rms_rows(x1, gffn_ref[...]))
    h2_out[rows, :D_MODEL] = hi
    logits = _dot(hi, wr_ref[...]) + br_ref[...]

    lane = lax.broadcasted_iota(jnp.int32, logits.shape, 1).astype(jnp.float32)
    neg = jnp.float32(-jnp.inf)
    big = jnp.float32(LANES)
    is_grp = lane < N_GROUPS
    lg = jnp.where(is_grp, logits, neg)
    gmax = jnp.max(lg, axis=-1, keepdims=True)
    gid = jnp.min(jnp.where(lg == gmax, lane, big), axis=-1, keepdims=True)
    pg_top = 1.0 / jnp.sum(jnp.where(is_grp, jnp.exp(logits - gmax), 0.0), axis=-1, keepdims=True)
    base = N_GROUPS + EXPERTS_PER_GROUP * gid
    in_sel = (lane >= base) & (lane < base + EXPERTS_PER_GROUP)
    le = jnp.where(in_sel, logits, neg)
    m1 = jnp.max(le, axis=-1, keepdims=True)
    i1 = jnp.min(jnp.where(le == m1, lane, big), axis=-1, keepdims=True)
    le2 = jnp.where(lane == i1, neg, le)
    m2 = jnp.max(le2, axis=-1, keepdims=True)
    i2 = jnp.min(jnp.where(le2 == m2, lane, big), axis=-1, keepdims=True)
    e2 = jnp.exp(m2 - m1)
    w1 = pg_top / (1.0 + e2)
    w2 = pg_top * e2 / (1.0 + e2)
    j1 = i1 - base + ROUTE_W0
    j2 = i2 - base + ROUTE_W0
    w1h = _bf16(w1).astype(jnp.float32)
    w2h = _bf16(w2).astype(jnp.float32)
    rec = jnp.where(lane == ROUTE_GID, gid,
                    jnp.where(lane == j1, w1h, jnp.where(lane == j2, w2h,
                    jnp.where(lane == j1 + ROUTE_LO, w1 - w1h, jnp.where(lane == j2 + ROUTE_LO, w2 - w2h, 0.0)))))
    h2_out[rows, D_MODEL:] = _bf16(rec)


def _mix_kernel(*refs):
    sub = TM_PROJ // MIX_SUBBLOCKS
    for sb in range(MIX_SUBBLOCKS):
        _mix_rows(slice(sb * sub, (sb + 1) * sub), *refs)


def _mix_call(x2d, oa, ob, sw, l):
    t = x2d.shape[0]
    tm = TM_PROJ
    row = lambda i: (i, 0)
    weights = [sw[n] for n in ("g_mla_out", "g_na_out", "w_out", "g_ffn", "w_r", "b_r")]
    return pl.pallas_call(
        _mix_kernel, grid=(t // tm,),
        in_specs=[pl.BlockSpec((tm, D_MODEL), row), pl.BlockSpec((tm, MLA_WIDTH), row),
                  pl.BlockSpec((tm, NA_WIDTH), row)] + [_layer(w, l) for w in weights],
        out_specs=[pl.BlockSpec((tm, D_MODEL), row), pl.BlockSpec((tm, H2_COLS), row)],
        out_shape=[jax.ShapeDtypeStruct((t, D_MODEL), jnp.float32),
                   jax.ShapeDtypeStruct((t, H2_COLS), jnp.bfloat16)],
        compiler_params=_cparams(("parallel",)), name="mix_router",
    )(x2d, oa, ob, *weights)


def _moe_partition(sub, h2_ref, xs_ref, ys_ref, cnt_ref):
    tm = TM_MOE
    rows = slice(sub * tm, (sub + 1) * tm)
    route = h2_ref[rows, D_MODEL:].astype(jnp.float32)
    lane = lax.broadcasted_iota(jnp.int32, route.shape, 1)
    gid = jnp.sum(jnp.where(lane == ROUTE_GID, route, 0.0), axis=-1, keepdims=True)
    onehot = jnp.where((lane.astype(jnp.float32) == gid) & (lane < N_GROUPS), 1.0, 0.0)
    r_i = lax.broadcasted_iota(jnp.int32, (RANK_BLOCK, RANK_BLOCK), 0)
    c_i = lax.broadcasted_iota(jnp.int32, (RANK_BLOCK, RANK_BLOCK), 1)
    tri = jnp.where(c_i < r_i, 1.0, 0.0).astype(jnp.bfloat16)
    ranks = []
    sizes = jnp.zeros((1, LANES), jnp.float32)
    for blk in range(tm // RANK_BLOCK):
        oh = onehot[blk * RANK_BLOCK:(blk + 1) * RANK_BLOCK]
        ranks.append(_dot(tri, _bf16(oh)) + sizes)
        sizes = sizes + jnp.sum(oh, axis=0, keepdims=True)
    rank = jnp.concatenate(ranks, axis=0)
    lane1 = lane[0:1]
    start_vec = jnp.zeros((1, LANES), jnp.float32)
    acc = jnp.int32(0)
    for g in range(N_GROUPS):
        n = jnp.sum(jnp.where(lane1 == g, sizes, 0.0)).astype(jnp.int32)
        cnt_ref[sub * N_GROUPS + g] = acc
        cnt_ref[MOE_SUBTILES * N_GROUPS + sub * N_GROUPS + g] = acc + n
        start_vec = jnp.where(lane1 == g, acc.astype(jnp.float32), start_vec)
        acc = acc + n
    dest = jnp.sum(onehot * (start_vec + rank), axis=-1, keepdims=True)
    dest_row = jnp.transpose(jnp.broadcast_to(dest, (tm, LANES)))[0:1, :]
    rowi = lax.broadcasted_iota(jnp.int32, (tm, tm), 0)
    perm = jnp.where(rowi == dest_row.astype(jnp.int32), 1.0, 0.0).astype(jnp.bfloat16)
    xs_ref[sub, :tm, :] = _bf16(_dot(perm, h2_ref[rows, :]))
    xs_ref[sub, tm:, :] = jnp.zeros((MOE_ROWS - tm, H2_COLS), jnp.bfloat16)
    ys_ref[sub] = jnp.zeros((MOE_ROWS, D_MODEL), jnp.bfloat16)
    return dest


def _moe_experts(sub, g, off, size, start_g, end_g, xs_ref, ys_ref, wg_ref, wu_ref, wd_ref):
    off = pl.multiple_of(off, BF16_ROWS)
    xc = xs_ref[sub, pl.ds(off, size), :D_MODEL]
    cw = xs_ref[sub, pl.ds(off, size), D_MODEL:].astype(jnp.float32)
    hid = []
    for j in range(EXPERTS_PER_GROUP):
        a = _dot(xc, wg_ref[g * EXPERTS_PER_GROUP + j])
        u = _dot(xc, wu_ref[g * EXPERTS_PER_GROUP + j])
        cj = cw[:, ROUTE_W0 + j:ROUTE_W0 + j + 1] + cw[:, ROUTE_W0 + ROUTE_LO + j:ROUTE_W0 + ROUTE_LO + j + 1]
        hid.append(a * (1.0 / (1.0 + jnp.exp(-a))) * u * cj)
    hid = _bf16(jnp.concatenate(hid, axis=-1))
    w_down = wd_ref[pl.ds(g * EXPERTS_PER_GROUP, EXPERTS_PER_GROUP)].reshape(EXPERTS_PER_GROUP * D_EXPERT, D_MODEL)
    rows = off + lax.broadcasted_iota(jnp.int32, (size, 1), 0)
    mine = (rows >= start_g) & (rows < end_g)
    old = ys_ref[sub, pl.ds(off, size), :].astype(jnp.float32)
    ys_ref[sub, pl.ds(off, size), :] = _bf16(jnp.where(mine, _dot(hid, w_down), old))


def _moe_kernel(x1_ref, h2_ref, wg_ref, wu_ref, wd_ref, o_ref, xs_ref, ys_ref, cnt_ref):
    tm = TM_MOE
    dests = [_moe_partition(sub, h2_ref, xs_ref, ys_ref, cnt_ref) for sub in range(MOE_SUBTILES)]

    def _run(u, carry):
        sub = lax.shift_right_logical(u, 2)
        g = lax.bitwise_and(u, N_GROUPS - 1)
        start_g = cnt_ref[u]
        end_g = cnt_ref[MOE_SUBTILES * N_GROUPS + u]
        first = lax.shift_left(lax.shift_right_logical(start_g, 4), 4)
        span = end_g - first
        args = (xs_ref, ys_ref, wg_ref, wu_ref, wd_ref)
        n_full = lax.shift_right_logical(span, MOE_LOOP_CHUNK.bit_length() - 1)

        def _chunk(c, inner):
            _moe_experts(sub, g, first + c * MOE_LOOP_CHUNK, MOE_LOOP_CHUNK, start_g, end_g, *args)
            return inner
        lax.fori_loop(0, n_full, _chunk, 0)
        tail = first + n_full * MOE_LOOP_CHUNK
        rest = end_g - tail
        lo = 0
        for size in MOE_TAIL_CHUNKS:
            @pl.when((rest > lo) & (rest <= size))
            def _(size=size):
                _moe_experts(sub, g, tail, size, start_g, end_g, *args)
            lo = size
        return carry

    lax.fori_loop(0, MOE_SUBTILES * N_GROUPS, _run, 0)

    col = lax.broadcasted_iota(jnp.int32, (tm, tm), 1)
    for sub, dest in enumerate(dests):
        rows = slice(sub * tm, (sub + 1) * tm)
        unperm = jnp.where(col == dest.astype(jnp.int32), 1.0, 0.0).astype(jnp.bfloat16)
        o_ref[rows, :] = x1_ref[rows, :] + _dot(unperm, ys_ref[sub, :tm, :])


def _moe_call(x1, h2, sw, l):
    t = x1.shape[0]
    tm = TM_MOE * MOE_SUBTILES
    row = lambda i: (i, 0)
    resident = lambda w: _layer(w, l, pipeline_mode=pl.Buffered(1))
    return pl.pallas_call(
        _moe_kernel, grid=(t // tm,),
        in_specs=[pl.BlockSpec((tm, D_MODEL), row), pl.BlockSpec((tm, H2_COLS), row),
                  resident(sw["w_g"]), resident(sw["w_u"]), resident(sw["w_d"])],
        out_specs=pl.BlockSpec((tm, D_MODEL), row),
        out_shape=jax.ShapeDtypeStruct((t, D_MODEL), jnp.float32),
        scratch_shapes=[pltpu.VMEM((MOE_SUBTILES, MOE_ROWS, H2_COLS), jnp.bfloat16),
                        pltpu.VMEM((MOE_SUBTILES, MOE_ROWS, D_MODEL), jnp.bfloat16),
                        pltpu.SMEM((2 * MOE_SUBTILES * N_GROUPS,), jnp.int32)],
        compiler_params=_cparams(("parallel",)), name="moe",
    )(x1, h2, sw["w_g"], sw["w_u"], sw["w_d"])


def _segment_matrix(width, segments):
    m = np.zeros((width, width), np.float32)
    for lo, hi in segments:
        m[lo:hi, lo:hi] = 1.0 / (hi - lo)
    return jnp.asarray(m, jnp.bfloat16)


def _constants(seq):
    t = np.arange(seq)
    row = (t // GRID_W).astype(np.float32)
    col = (t % GRID_W).astype(np.float32)
    n_freq = MLA_ROPE // 4
    inv = (np.float32(ROPE_THETA) ** (-np.arange(n_freq, dtype=np.float32) / n_freq)).astype(np.float32)
    ang_r = row[:, None] * inv[None, :]
    ang_c = col[:, None] * inv[None, :]

    def tables(base):
        cos = np.zeros((seq, LANES), np.float32)
        cos[:, :base] = 1.0
        sa = np.zeros((seq, LANES), np.float32)
        sb = np.zeros((seq, LANES), np.float32)
        for k, ang in enumerate((ang_r, ang_c)):
            o = base + 2 * n_freq * k
            cos[:, o:o + n_freq] = np.cos(ang)
            cos[:, o + n_freq:o + 2 * n_freq] = np.cos(ang)
            sa[:, o:o + n_freq] = -np.sin(ang)
            sb[:, o + n_freq:o + 2 * n_freq] = np.sin(ang)
        return jnp.asarray(cos), jnp.asarray(sa), jnp.asarray(sb)

    cos_q, sin_qa, sin_qb = tables(MLA_NOPE)
    cos_k, sin_ka, sin_kb = tables(0)
    seg_q = []
    for hb in range(0, MXU_TILE, HEAD_PAD):
        seg_q += [(hb, hb + MLA_NOPE), (hb + MLA_NOPE, hb + MLA_QK)]
    shift_lanes = np.zeros((1, MLA_PAD_WIDTH), np.float32)
    shift_lanes[0, SHIFT_LANE::HEAD_PAD] = 1.0
    return dict(cos_q=cos_q, sin_qa=sin_qa, sin_qb=sin_qb, cos_k=cos_k, sin_ka=sin_ka, sin_kb=sin_kb,
                m_q=_segment_matrix(MXU_TILE, seg_q),
                m_kpe=_segment_matrix(LANES, [(0, MLA_ROPE)]),
                m_na=_segment_matrix(MXU_TILE, [(o, o + NA_DH) for o in range(0, MXU_TILE, NA_DH)]),
                shift_lanes=jnp.asarray(shift_lanes), k_one=jnp.asarray(shift_lanes[:, :HEAD_PAD]))


def _na_bias_tables(rpb, shift):
    cols = np.arange(GRID_W)
    cs = np.clip(cols - NA_KC // 2, 0, GRID_W - NA_KC)
    col_mask = (cols[None, :] >= cs[:, None]) & (cols[None, :] < cs[:, None] + NA_KC)
    dc = np.clip(cols[None, :] - cols[:, None], -(NA_KC - 1), NA_KC - 1) + NA_KC - 1
    onehot = (dc[None] == np.arange(2 * NA_KC - 1)[:, None, None]).astype(np.float32)
    rpb_c = jnp.einsum("lhdj,jqk->lhdqk", rpb, jnp.asarray(onehot), precision=lax.Precision.HIGHEST)
    rpb_c = jnp.where(jnp.asarray(col_mask), rpb_c * LOG2E - shift[:, None, None, None, None], MASK_NEG)
    return jnp.concatenate([rpb_c[:, :, :-1], rpb_c[:, :, 1:]], axis=-1)


def _softmax_shifts(p):
    amax = lambda v: jnp.max(jnp.abs(v), axis=-1)
    gq, gk = p["g_mla_q"], p["g_mla_k"]
    qn = (MLA_QK ** -0.5 * LOG2E) * jnp.sqrt(MLA_NOPE * amax(gq[:, :MLA_NOPE]) ** 2 + MLA_ROPE * amax(gq[:, MLA_NOPE:]) ** 2)
    kn = jnp.sqrt(MLA_NOPE * amax(gk[:, :MLA_NOPE]) ** 2 + MLA_ROPE * amax(gk[:, MLA_NOPE:]) ** 2)
    b_mla = BOUND_SLACK * qn * kn
    mla_fast = 2.0 * b_mla <= MAX_SHIFT_GAP
    b_na = BOUND_SLACK * (NA_DH ** -0.5 * LOG2E) * NA_DH * amax(p["g_na_q"]) * amax(p["g_na_k"])
    bias_hi = jnp.max(p["na_rpb"], axis=(1, 2, 3)) * LOG2E
    bias_lo = jnp.min(p["na_rpb"], axis=(1, 2, 3)) * LOG2E
    na_fast = 2.0 * b_na + (bias_hi - bias_lo) <= MAX_SHIFT_GAP
    return (mla_fast, jnp.where(mla_fast, b_mla, 0.0)), (na_fast, jnp.where(na_fast, b_na + bias_hi, 0.0))


def _prep_weights(p, consts):
    bf = jnp.bfloat16
    nl = p["w_in"].shape[0]
    (mla_fast, mla_shift), (na_fast, na_shift) = _softmax_shifts(p)
    w_in = p["w_in"]
    split = Q_LORA + KV_LORA + MLA_ROPE
    w_in_p = jnp.zeros((nl, D_MODEL, PROJ_COLS), bf)
    w_in_p = w_in_p.at[:, :, :split].set(w_in[:, :, :split].astype(bf)).at[:, :, C_NAQ:].set(w_in[:, :, split:].astype(bf))
    w_qb = jnp.pad(p["w_q_b"].reshape(nl, Q_LORA, MLA_HEADS, MLA_QK), ((0, 0), (0, 0), (0, 0), (0, HEAD_PAD - MLA_QK)))
    w_kv = p["w_kv_b"].reshape(nl, KV_LORA, MLA_HEADS, MLA_NOPE + MLA_V)
    w_kc = w_kv[..., :MLA_NOPE].reshape(nl, KV_LORA, MLA_HEADS * MLA_NOPE)
    w_vt = w_kv[..., MLA_NOPE:].reshape(nl, KV_LORA, MLA_WIDTH).transpose(0, 2, 1)
    g_q = jnp.tile(jnp.pad(p["g_mla_q"] * (MLA_QK ** -0.5 * LOG2E), ((0, 0), (0, HEAD_PAD - MLA_QK))), (1, MLA_HEADS))
    w_r = jnp.concatenate([p["w_router_group"], p["w_router_expert"]], axis=2)
    w_r = jnp.pad(w_r, ((0, 0), (0, 0), (0, LANES - w_r.shape[2])))
    b_r = jnp.concatenate([p["b_router_group"], p["b_router_expert"]], axis=1)
    b_r = jnp.pad(b_r, ((0, 0), (0, LANES - b_r.shape[1])))
    row = lambda v: v[:, None, :]
    return dict(
        g_mix=row(p["g_mix_norm"]), w_in=w_in_p,
        g_q_a=row(p["g_q_a"]), w_qb=w_qb.reshape(nl, Q_LORA, MLA_PAD_WIDTH).astype(bf),
        g_kv_a=row(p["g_kv_a"]), w_kc=w_kc.astype(bf), w_vt=w_vt.astype(bf),
        g_q=row(g_q), g_kc=row(jnp.tile(p["g_mla_k"][:, :MLA_NOPE], (1, MLA_HEADS))),
        g_kpe=row(jnp.pad(p["g_mla_k"][:, MLA_NOPE:], ((0, 0), (0, LANES - MLA_ROPE)))),
        g_naq=row(jnp.tile(p["g_na_q"] * (NA_DH ** -0.5 * LOG2E), (1, NA_HEADS))),
        g_nak=row(jnp.tile(p["g_na_k"], (1, NA_HEADS))),
        na_bias=_na_bias_tables(p["na_rpb"], na_shift), na_fast=na_fast,
        q_shift=-mla_shift[:, None, None] * consts["shift_lanes"][None], mla_fast=mla_fast,
        g_mla_out=row(p["g_mla_out"]), g_na_out=row(p["g_na_out"]),
        w_out=p["w_out"].astype(bf), g_ffn=row(p["g_ffn_norm"]),
        w_r=w_r.astype(bf), b_r=row(b_r),
        w_g=p["w_gate"].astype(bf), w_u=p["w_up"].astype(bf), w_d=p["w_down"].astype(bf))


def kernel(x, g_mix_norm, w_in, g_q_a, w_q_b, g_kv_a, w_kv_b, g_mla_q, g_mla_k, g_na_q, g_na_k, na_rpb, g_mla_out, g_na_out, w_out, g_ffn_norm, w_router_group, b_router_group, w_router_expert, b_router_expert, w_gate, w_up, w_down):
    batch, seq, d = x.shape
    assert d == D_MODEL and seq % TM_PROJ == 0 and seq % TQ_MLA == 0 and (batch * seq) % (TM_MOE * MOE_SUBTILES) == 0
    rows = seq // GRID_W
    assert rows % NA_ROWS_PER_STEP == 0 and rows >= NA_KR_MAX and NA_KR_MAX % 2 == 0
    p = dict(g_mix_norm=g_mix_norm, w_in=w_in, g_q_a=g_q_a, w_q_b=w_q_b, g_kv_a=g_kv_a, w_kv_b=w_kv_b,
             g_mla_q=g_mla_q, g_mla_k=g_mla_k, g_na_q=g_na_q, g_na_k=g_na_k, na_rpb=na_rpb,
             g_mla_out=g_mla_out, g_na_out=g_na_out, w_out=w_out, g_ffn_norm=g_ffn_norm,
             w_router_group=w_router_group, b_router_group=b_router_group,
             w_router_expert=w_router_expert, b_router_expert=b_router_expert,
             w_gate=w_gate, w_up=w_up, w_down=w_down)
    consts = _constants(seq)
    sw = _prep_weights(p, consts)
    xf = x.reshape(batch * seq, d)
    for l in range(w_in.shape[0]):
        q, k, vt, naq, nak, nav = _proj_call(xf, sw, l, consts, seq)
        o_a = lax.cond(sw["mla_fast"][l],
                       functools.partial(_mla_call, batch=batch, seq=seq, use_max=False),
                       functools.partial(_mla_call, batch=batch, seq=seq, use_max=True), q, k, vt)
        o_b = _na_call(sw["na_fast"][l:l + 1].astype(jnp.int32), naq, nak, nav, sw["na_bias"], l=l, batch=batch, seq=seq)
        x1, h2 = _mix_call(xf, o_a, o_b, sw, l)
        xf = _moe_call(x1, h2, sw, l)
    return xf.reshape(batch, seq, d)
```

```python
import functools

import numpy as np
import jax
import jax.numpy as jnp
from jax import lax
from jax.experimental import pallas as pl
from jax.experimental.pallas import tpu as pltpu

D_MODEL = 1024
GRID_W = 64
MLA_HEADS = 8
MLA_NOPE = 64
MLA_ROPE = 32
MLA_V = 64
MLA_QK = MLA_NOPE + MLA_ROPE
Q_LORA = 256
KV_LORA = 128
MLA_WIDTH = MLA_HEADS * MLA_V
ROPE_THETA = 10000.0
NA_HEADS = 8
NA_DH = 64
NA_WIDTH = NA_HEADS * NA_DH
NA_KR_MAX = 8
NA_KC = 16
N_GROUPS = 4
EXPERTS_PER_GROUP = 4
D_EXPERT = 256
EPS = 1e-6

LANES = 128
MXU_TILE = 256
BF16_ROWS = 16
HEAD_PAD = LANES
MLA_PAD_WIDTH = MLA_HEADS * HEAD_PAD
PROJ_COLS = 2048
LOG2E = 1.4426950408889634
MASK_NEG = -1e30
SHIFT_LANE = MLA_QK
MAX_SHIFT_GAP = 100.0
BOUND_SLACK = 1.02
VMEM_LIMIT = 56 * 1024 * 1024

C_CQ = 0
C_CKV = C_CQ + Q_LORA
C_KPE = C_CKV + KV_LORA
C_NAQ = C_KPE + LANES
C_NAK = C_NAQ + NA_WIDTH
C_NAV = C_NAK + NA_WIDTH

TM_PROJ = 1024
PROJ_SUBBLOCKS = 4
MIX_SUBBLOCKS = 2
TQ_MLA = 1024
MLA_HEADS_PER_STEP = 4
NA_ROWS_PER_STEP = 16
TM_MOE = 512
MOE_SUBTILES = 2
MOE_LOOP_CHUNK = 256
MOE_TAIL_CHUNKS = (128, 176, 224, 256)
MOE_ROWS = TM_MOE + MOE_LOOP_CHUNK
RANK_BLOCK = 256
ROUTE_GID = 0
ROUTE_W0 = 1
ROUTE_LO = 8
H2_COLS = D_MODEL + LANES

assert N_GROUPS == 4
assert MOE_LOOP_CHUNK & (MOE_LOOP_CHUNK - 1) == 0 and MOE_TAIL_CHUNKS[-1] == MOE_LOOP_CHUNK
assert MLA_NOPE == NA_DH


def _cparams(sem):
    return pltpu.CompilerParams(dimension_semantics=sem, vmem_limit_bytes=VMEM_LIMIT)


def _full(shape):
    nd = len(shape)
    return pl.BlockSpec(shape, lambda *_: (0,) * nd)


def _layer(w, l, **kw):
    nd = w.ndim
    return pl.BlockSpec((None,) + w.shape[1:], lambda *_: (l,) + (0,) * (nd - 1), **kw)


def _bf16(x):
    return x.astype(jnp.bfloat16)


def _dot(a, b):
    return jnp.dot(a, b, preferred_element_type=jnp.float32)


def _dot_nt(a, b):
    return lax.dot_general(a, b, (((1,), (1,)), ((), ())), preferred_element_type=jnp.float32)


def _rms_rows(x, gain):
    ms = jnp.mean(x * x, axis=-1, keepdims=True)
    return x * lax.rsqrt(ms + EPS) * gain


def _segment_rms_scale(x, seg_mat):
    n = x.shape[-1]
    parts = []
    for j in range(0, n, seg_mat.shape[0]):
        w = min(seg_mat.shape[0], n - j)
        xs = x[:, j:j + w]
        parts.append(_dot(_bf16(xs * xs), seg_mat[:w, :w]))
    ms = parts[0] if len(parts) == 1 else jnp.concatenate(parts, axis=-1)
    return lax.rsqrt(ms + EPS)


def _rope_block(x, cos, sin_a, sin_b):
    return x * cos + pltpu.roll(x, LANES - 8, 1) * sin_a + pltpu.roll(x, 8, 1) * sin_b


def _proj_rows(rows, x_ref, gmix_ref, win_ref, gqa_ref, wqb_ref, gkva_ref, wkc_ref, wvt_ref,
               gq_ref, gkc_ref, gkpe_ref, gnaq_ref, gnak_ref,
               mq_ref, mkpe_ref, mna_ref, qshift_ref, kone_ref,
               cq_ref, sqa_ref, sqb_ref, ck_ref, ska_ref, skb_ref,
               q_out, k_out, vt_out, naq_out, nak_out, nav_out):
    h = _bf16(_rms_rows(x_ref[rows, :], gmix_ref[...]))
    proj = _dot(h, win_ref[...])

    cqn = _bf16(_rms_rows(proj[:, C_CQ:C_CKV], gqa_ref[...]))
    q = _dot(cqn, wqb_ref[...])
    q = q * _segment_rms_scale(q, mq_ref[...]) * gq_ref[...]
    cosq, sqa, sqb = cq_ref[rows, :], sqa_ref[rows, :], sqb_ref[rows, :]
    for hd in range(MLA_HEADS):
        sl = slice(hd * HEAD_PAD, (hd + 1) * HEAD_PAD)
        q_out[rows, sl] = _bf16(_rope_block(q[:, sl], cosq, sqa, sqb) + qshift_ref[:, sl])

    ckvn = _bf16(_rms_rows(proj[:, C_CKV:C_KPE], gkva_ref[...]))
    kc = _dot(ckvn, wkc_ref[...])
    kc = kc * _segment_rms_scale(kc, mna_ref[...]) * gkc_ref[...]
    vt_out[:, rows] = _bf16(_dot_nt(wvt_ref[...], ckvn))
    kp = proj[:, C_KPE:C_NAQ]
    kp = kp * lax.rsqrt(_dot(_bf16(kp * kp), mkpe_ref[...]) + EPS) * gkpe_ref[...]
    kp = _rope_block(kp, ck_ref[rows, :], ska_ref[rows, :], skb_ref[rows, :])
    tail = pltpu.roll(kp, MLA_NOPE, 1) + kone_ref[...]
    lane = lax.broadcasted_iota(jnp.int32, tail.shape, 1)
    for pr in range(MLA_HEADS // 2):
        two = kc[:, pr * LANES:(pr + 1) * LANES]
        k_out[rows, (2 * pr) * HEAD_PAD:(2 * pr + 1) * HEAD_PAD] = _bf16(jnp.where(lane < MLA_NOPE, two, tail))
        k_out[rows, (2 * pr + 1) * HEAD_PAD:(2 * pr + 2) * HEAD_PAD] = _bf16(
            jnp.where(lane < MLA_NOPE, pltpu.roll(two, MLA_NOPE, 1), tail))

    naq = proj[:, C_NAQ:C_NAK]
    naq_out[rows, :] = _bf16(naq * _segment_rms_scale(naq, mna_ref[...]) * gnaq_ref[...])
    nak = proj[:, C_NAK:C_NAV]
    nak_out[rows, :] = _bf16(nak * _segment_rms_scale(nak, mna_ref[...]) * gnak_ref[...])
    nav_out[rows, :] = _bf16(proj[:, C_NAV:PROJ_COLS])


def _proj_kernel(*refs):
    sub = TM_PROJ // PROJ_SUBBLOCKS
    for sb in range(PROJ_SUBBLOCKS):
        _proj_rows(slice(sb * sub, (sb + 1) * sub), *refs)


def _proj_call(x2d, sw, l, consts, seq):
    t = x2d.shape[0]
    tm = TM_PROJ
    n_seq_tiles = seq // tm
    row = lambda i: (i, 0)
    tab = lambda i: (i % n_seq_tiles, 0)
    layered = ("g_mix", "w_in", "g_q_a", "w_qb", "g_kv_a", "w_kc", "w_vt", "g_q", "g_kc", "g_kpe", "g_naq", "g_nak")
    weights = [sw[n] for n in layered] + [consts["m_q"], consts["m_kpe"], consts["m_na"], sw["q_shift"], consts["k_one"]]
    specs = ([_layer(sw[n], l) for n in layered] + [_full(consts[n].shape) for n in ("m_q", "m_kpe", "m_na")]
             + [_layer(sw["q_shift"], l), _full(consts["k_one"].shape)])
    tables = [consts["cos_q"], consts["sin_qa"], consts["sin_qb"], consts["cos_k"], consts["sin_ka"], consts["sin_kb"]]
    in_specs = [pl.BlockSpec((tm, D_MODEL), row)] + specs + [pl.BlockSpec((tm, LANES), tab) for _ in tables]
    out_shapes = [jax.ShapeDtypeStruct((t, MLA_PAD_WIDTH), jnp.bfloat16),
                  jax.ShapeDtypeStruct((t, MLA_PAD_WIDTH), jnp.bfloat16),
                  jax.ShapeDtypeStruct((MLA_WIDTH, t), jnp.bfloat16),
                  jax.ShapeDtypeStruct((t, NA_WIDTH), jnp.bfloat16),
                  jax.ShapeDtypeStruct((t, NA_WIDTH), jnp.bfloat16),
                  jax.ShapeDtypeStruct((t, NA_WIDTH), jnp.bfloat16)]
    out_specs = [pl.BlockSpec((MLA_WIDTH, tm), lambda i: (0, i)) if s.shape[0] == MLA_WIDTH
                 else pl.BlockSpec((tm, s.shape[1]), row) for s in out_shapes]
    return pl.pallas_call(
        _proj_kernel, grid=(t // tm,), in_specs=in_specs, out_specs=out_specs, out_shape=out_shapes,
        compiler_params=_cparams(("parallel",)), name="proj",
    )(x2d, *weights, *tables)


def _mla_kernel(q_ref, k_ref, vt_ref, o_ref, *, use_max):
    scores = []
    for hh in range(MLA_HEADS_PER_STEP):
        sl = slice(hh * HEAD_PAD, (hh + 1) * HEAD_PAD)
        scores.append(_dot_nt(k_ref[:, sl], q_ref[:, sl]))
    outs = []
    for hh, s in enumerate(scores):
        p = jnp.exp2(s - jnp.max(s, axis=0, keepdims=True)) if use_max else jnp.exp2(s)
        l = jnp.sum(p, axis=0, keepdims=True)
        pair = slice((hh // 2) * LANES, (hh // 2 + 1) * LANES)
        outs.append(_dot(vt_ref[pair, :], _bf16(p)) * (1.0 / l))
    row = lax.broadcasted_iota(jnp.int32, outs[0].shape, 0)
    for pp in range(MLA_HEADS_PER_STEP // 2):
        o_t = jnp.where(row < MLA_V, outs[2 * pp], outs[2 * pp + 1])
        o_ref[:, pp * LANES:(pp + 1) * LANES] = _bf16(o_t.T)


def _mla_call(q, k, vt, *, batch, seq, use_max):
    t = q.shape[0]
    tq = TQ_MLA
    nq = seq // tq
    hps = MLA_HEADS_PER_STEP
    return pl.pallas_call(
        functools.partial(_mla_kernel, use_max=use_max), grid=(batch, MLA_HEADS // hps, nq),
        in_specs=[pl.BlockSpec((tq, hps * HEAD_PAD), lambda b, p, i: (b * nq + i, p)),
                  pl.BlockSpec((seq, hps * HEAD_PAD), lambda b, p, i: (b, p)),
                  pl.BlockSpec((hps * MLA_V, seq), lambda b, p, i: (p, b))],
        out_specs=pl.BlockSpec((tq, hps * MLA_V), lambda b, p, i: (b * nq + i, p)),
        out_shape=jax.ShapeDtypeStruct((t, MLA_WIDTH), jnp.bfloat16),
        compiler_params=_cparams(("parallel", "parallel", "arbitrary")), name="mla_attn",
    )(q, k, vt)


def _na_body(q_ref, k_ref, v_ref, bias_ref, o_ref, *, rows, kr, use_max):
    step = pl.program_id(1)
    band = kr * GRID_W
    lane = lax.broadcasted_iota(jnp.int32, (GRID_W, LANES), 1)
    units = []
    for rr in range(NA_ROWS_PER_STEP):
        r = step * NA_ROWS_PER_STEP + rr
        rs = jnp.clip(r - kr // 2, 0, rows - kr)
        rel0 = rs - r + (NA_KR_MAX - 1)
        start = pl.multiple_of(rs * GRID_W, GRID_W)
        qrow = q_ref[rr * GRID_W:(rr + 1) * GRID_W, :]
        for pr in range(NA_HEADS // 2):
            sl = slice(pr * LANES, (pr + 1) * LANES)
            q2 = qrow[:, sl]
            zero = jnp.zeros_like(q2)
            qq = jnp.concatenate([jnp.where(lane < NA_DH, q2, zero), jnp.where(lane >= NA_DH, q2, zero)], axis=0)
            s = _dot_nt(qq, k_ref[pl.ds(start, band), sl])
            units.append((rr, pr, start, rel0, s))
    probs = []
    for rr, pr, start, rel0, s in units:
        bias = jnp.concatenate([jnp.concatenate([bias_ref[2 * pr + hh, rel0 + 2 * m] for m in range(kr // 2)], axis=-1)
                                for hh in range(2)], axis=0)
        s = s + bias
        p = jnp.exp2(s - jnp.max(s, axis=-1, keepdims=True)) if use_max else jnp.exp2(s)
        l = jnp.sum(p, axis=-1, keepdims=True)
        probs.append((_bf16(p), 1.0 / l))
    for (rr, pr, start, rel0, s), (p, rl) in zip(units, probs):
        sl = slice(pr * LANES, (pr + 1) * LANES)
        pv = _dot(p, v_ref[pl.ds(start, band), sl]) * rl
        o_ref[rr * GRID_W:(rr + 1) * GRID_W, sl] = _bf16(jnp.where(lane < NA_DH, pv[:GRID_W], pv[GRID_W:]))


def _na_kernel(fast_ref, *refs, rows, kr):
    @pl.when(fast_ref[0] != 0)
    def _():
        _na_body(*refs, rows=rows, kr=kr, use_max=False)

    @pl.when(fast_ref[0] == 0)
    def _():
        _na_body(*refs, rows=rows, kr=kr, use_max=True)


def _na_call(fast, q, k, v, bias, *, l, batch, seq):
    t = q.shape[0]
    rows = seq // GRID_W
    kr = min(NA_KR_MAX, rows)
    steps = rows // NA_ROWS_PER_STEP
    blk = NA_ROWS_PER_STEP * GRID_W
    grid_spec = pltpu.PrefetchScalarGridSpec(
        num_scalar_prefetch=1, grid=(batch, steps),
        in_specs=[pl.BlockSpec((blk, NA_WIDTH), lambda b, i, f: (b * steps + i, 0)),
                  pl.BlockSpec((seq, NA_WIDTH), lambda b, i, f: (b, 0)),
                  pl.BlockSpec((seq, NA_WIDTH), lambda b, i, f: (b, 0)),
                  _layer(bias, l)],
        out_specs=pl.BlockSpec((blk, NA_WIDTH), lambda b, i, f: (b * steps + i, 0)))
    return pl.pallas_call(
        functools.partial(_na_kernel, rows=rows, kr=kr), grid_spec=grid_spec,
        out_shape=jax.ShapeDtypeStruct((t, NA_WIDTH), jnp.bfloat16),
        compiler_params=_cparams(("parallel", "arbitrary")), name="na_attn",
    )(fast, q, k, v, bias)


def _mix_rows(rows, x_ref, oa_ref, ob_ref, ga_ref, gb_ref, wout_ref, gffn_ref, wr_ref, br_ref, x1_out, h2_out):
    mixed = jnp.concatenate([_rms_rows(oa_ref[rows, :].astype(jnp.float32), ga_ref[...]),
                             _rms_rows(ob_ref[rows, :].astype(jnp.float32), gb_ref[...])], axis=-1)
    x1 = x_ref[rows, :] + _dot(_bf16(mixed), wout_ref[...])
    x1_out[rows, :] = x1
    hi = _bf16(_rms_rows(x1, gffn_ref[...]))
    h2_out[rows, :D_MODEL] = hi
    logits = _dot(hi, wr_ref[...]) + br_ref[...]

    lane = lax.broadcasted_iota(jnp.int32, logits.shape, 1).astype(jnp.float32)
    neg = jnp.float32(-jnp.inf)
    big = jnp.float32(LANES)
    is_grp = lane < N_GROUPS
    lg = jnp.where(is_grp, logits, neg)
    gmax = jnp.max(lg, axis=-1, keepdims=True)
    gid = jnp.min(jnp.where(lg == gmax, lane, big), axis=-1, keepdims=True)
    pg_top = 1.0 / jnp.sum(jnp.where(is_grp, jnp.exp(logits - gmax), 0.0), axis=-1, keepdims=True)
    base = N_GROUPS + EXPERTS_PER_GROUP * gid
    in_sel = (lane >= base) & (lane < base + EXPERTS_PER_GROUP)
    le = jnp.where(in_sel, logits, neg)
    m1 = jnp.max(le, axis=-1, keepdims=True)
    i1 = jnp.min(jnp.where(le == m1, lane, big), axis=-1, keepdims=True)
    le2 = jnp.where(lane == i1, neg, le)
    m2 = jnp.max(le2, axis=-1, keepdims=True)
    i2 = jnp.min(jnp.where(le2 == m2, lane, big), axis=-1, keepdims=True)
    e2 = jnp.exp(m2 - m1)
    w1 = pg_top / (1.0 + e2)
    w2 = pg_top * e2 / (1.0 + e2)
    j1 = i1 - base + ROUTE_W0
    j2 = i2 - base + ROUTE_W0
    w1h = _bf16(w1).astype(jnp.float32)
    w2h = _bf16(w2).astype(jnp.float32)
    rec = jnp.where(lane == ROUTE_GID, gid,
                    jnp.where(lane == j1, w1h, jnp.where(lane == j2, w2h,
                    jnp.where(lane == j1 + ROUTE_LO, w1 - w1h, jnp.where(lane == j2 + ROUTE_LO, w2 - w2h, 0.0)))))
    h2_out[rows, D_MODEL:] = _bf16(rec)


def _mix_kernel(*refs):
    sub = TM_PROJ // MIX_SUBBLOCKS
    for sb in range(MIX_SUBBLOCKS):
        _mix_rows(slice(sb * sub, (sb + 1) * sub), *refs)


def _mix_call(x2d, oa, ob, sw, l):
    t = x2d.shape[0]
    tm = TM_PROJ
    row = lambda i: (i, 0)
    weights = [sw[n] for n in ("g_mla_out", "g_na_out", "w_out", "g_ffn", "w_r", "b_r")]
    return pl.pallas_call(
        _mix_kernel, grid=(t // tm,),
        in_specs=[pl.BlockSpec((tm, D_MODEL), row), pl.BlockSpec((tm, MLA_WIDTH), row),
                  pl.BlockSpec((tm, NA_WIDTH), row)] + [_layer(w, l) for w in weights],
        out_specs=[pl.BlockSpec((tm, D_MODEL), row), pl.BlockSpec((tm, H2_COLS), row)],
        out_shape=[jax.ShapeDtypeStruct((t, D_MODEL), jnp.float32),
                   jax.ShapeDtypeStruct((t, H2_COLS), jnp.bfloat16)],
        compiler_params=_cparams(("parallel",)), name="mix_router",
    )(x2d, oa, ob, *weights)


def _moe_partition(sub, h2_ref, xs_ref, ys_ref, cnt_ref):
    tm = TM_MOE
    rows = slice(sub * tm, (sub + 1) * tm)
    route = h2_ref[rows, D_MODEL:].astype(jnp.float32)
    lane = lax.broadcasted_iota(jnp.int32, route.shape, 1)
    gid = jnp.sum(jnp.where(lane == ROUTE_GID, route, 0.0), axis=-1, keepdims=True)
    onehot = jnp.where((lane.astype(jnp.float32) == gid) & (lane < N_GROUPS), 1.0, 0.0)
    r_i = lax.broadcasted_iota(jnp.int32, (RANK_BLOCK, RANK_BLOCK), 0)
    c_i = lax.broadcasted_iota(jnp.int32, (RANK_BLOCK, RANK_BLOCK), 1)
    tri = jnp.where(c_i < r_i, 1.0, 0.0).astype(jnp.bfloat16)
    ranks = []
    sizes = jnp.zeros((1, LANES), jnp.float32)
    for blk in range(tm // RANK_BLOCK):
        oh = onehot[blk * RANK_BLOCK:(blk + 1) * RANK_BLOCK]
        ranks.append(_dot(tri, _bf16(oh)) + sizes)
        sizes = sizes + jnp.sum(oh, axis=0, keepdims=True)
    rank = jnp.concatenate(ranks, axis=0)
    lane1 = lane[0:1]
    start_vec = jnp.zeros((1, LANES), jnp.float32)
    acc = jnp.int32(0)
    for g in range(N_GROUPS):
        n = jnp.sum(jnp.where(lane1 == g, sizes, 0.0)).astype(jnp.int32)
        cnt_ref[sub * N_GROUPS + g] = acc
        cnt_ref[MOE_SUBTILES * N_GROUPS + sub * N_GROUPS + g] = acc + n
        start_vec = jnp.where(lane1 == g, acc.astype(jnp.float32), start_vec)
        acc = acc + n
    dest = jnp.sum(onehot * (start_vec + rank), axis=-1, keepdims=True)
    dest_row = jnp.transpose(jnp.broadcast_to(dest, (tm, LANES)))[0:1, :]
    rowi = lax.broadcasted_iota(jnp.int32, (tm, tm), 0)
    perm = jnp.where(rowi == dest_row.astype(jnp.int32), 1.0, 0.0).astype(jnp.bfloat16)
    xs_ref[sub, :tm, :] = _bf16(_dot(perm, h2_ref[rows, :]))
    xs_ref[sub, tm:, :] = jnp.zeros((MOE_ROWS - tm, H2_COLS), jnp.bfloat16)
    ys_ref[sub] = jnp.zeros((MOE_ROWS, D_MODEL), jnp.bfloat16)
    return dest


def _moe_experts(sub, g, off, size, start_g, end_g, xs_ref, ys_ref, wg_ref, wu_ref, wd_ref):
    off = pl.multiple_of(off, BF16_ROWS)
    xc = xs_ref[sub, pl.ds(off, size), :D_MODEL]
    cw = xs_ref[sub, pl.ds(off, size), D_MODEL:].astype(jnp.float32)
    hid = []
    for j in range(EXPERTS_PER_GROUP):
        a = _dot(xc, wg_ref[g * EXPERTS_PER_GROUP + j])
        u = _dot(xc, wu_ref[g * EXPERTS_PER_GROUP + j])
        cj = cw[:, ROUTE_W0 + j:ROUTE_W0 + j + 1] + cw[:, ROUTE_W0 + ROUTE_LO + j:ROUTE_W0 + ROUTE_LO + j + 1]
        hid.append(a * (1.0 / (1.0 + jnp.exp(-a))) * u * cj)
    hid = _bf16(jnp.concatenate(hid, axis=-1))
    w_down = wd_ref[pl.ds(g * EXPERTS_PER_GROUP, EXPERTS_PER_GROUP)].reshape(EXPERTS_PER_GROUP * D_EXPERT, D_MODEL)
    rows = off + lax.broadcasted_iota(jnp.int32, (size, 1), 0)
    mine = (rows >= start_g) & (rows < end_g)
    old = ys_ref[sub, pl.ds(off, size), :].astype(jnp.float32)
    ys_ref[sub, pl.ds(off, size), :] = _bf16(jnp.where(mine, _dot(hid, w_down), old))


def _moe_kernel(x1_ref, h2_ref, wg_ref, wu_ref, wd_ref, o_ref, xs_ref, ys_ref, cnt_ref):
    tm = TM_MOE
    dests = [_moe_partition(sub, h2_ref, xs_ref, ys_ref, cnt_ref) for sub in range(MOE_SUBTILES)]

    def _run(u, carry):
        sub = lax.shift_right_logical(u, 2)
        g = lax.bitwise_and(u, N_GROUPS - 1)
        start_g = cnt_ref[u]
        end_g = cnt_ref[MOE_SUBTILES * N_GROUPS + u]
        first = lax.shift_left(lax.shift_right_logical(start_g, 4), 4)
        span = end_g - first
        args = (xs_ref, ys_ref, wg_ref, wu_ref, wd_ref)
        n_full = lax.shift_right_logical(span, MOE_LOOP_CHUNK.bit_length() - 1)

        def _chunk(c, inner):
            _moe_experts(sub, g, first + c * MOE_LOOP_CHUNK, MOE_LOOP_CHUNK, start_g, end_g, *args)
            return inner
        lax.fori_loop(0, n_full, _chunk, 0)
        tail = first + n_full * MOE_LOOP_CHUNK
        rest = end_g - tail
        lo = 0
        for size in MOE_TAIL_CHUNKS:
            @pl.when((rest > lo) & (rest <= size))
            def _(size=size):
                _moe_experts(sub, g, tail, size, start_g, end_g, *args)
            lo = size
        return carry

    lax.fori_loop(0, MOE_SUBTILES * N_GROUPS, _run, 0)

    col = lax.broadcasted_iota(jnp.int32, (tm, tm), 1)
    for sub, dest in enumerate(dests):
        rows = slice(sub * tm, (sub + 1) * tm)
        unperm = jnp.where(col == dest.astype(jnp.int32), 1.0, 0.0).astype(jnp.bfloat16)
        o_ref[rows, :] = x1_ref[rows, :] + _dot(unperm, ys_ref[sub, :tm, :])


def _moe_call(x1, h2, sw, l):
    t = x1.shape[0]
    tm = TM_MOE * MOE_SUBTILES
    row = lambda i: (i, 0)
    resident = lambda w: _layer(w, l, pipeline_mode=pl.Buffered(1))
    return pl.pallas_call(
        _moe_kernel, grid=(t // tm,),
        in_specs=[pl.BlockSpec((tm, D_MODEL), row), pl.BlockSpec((tm, H2_COLS), row),
                  resident(sw["w_g"]), resident(sw["w_u"]), resident(sw["w_d"])],
        out_specs=pl.BlockSpec((tm, D_MODEL), row),
        out_shape=jax.ShapeDtypeStruct((t, D_MODEL), jnp.float32),
        scratch_shapes=[pltpu.VMEM((MOE_SUBTILES, MOE_ROWS, H2_COLS), jnp.bfloat16),
                        pltpu.VMEM((MOE_SUBTILES, MOE_ROWS, D_MODEL), jnp.bfloat16),
                        pltpu.SMEM((2 * MOE_SUBTILES * N_GROUPS,), jnp.int32)],
        compiler_params=_cparams(("parallel",)), name="moe",
    )(x1, h2, sw["w_g"], sw["w_u"], sw["w_d"])


def _segment_matrix(width, segments):
    m = np.zeros((width, width), np.float32)
    for lo, hi in segments:
        m[lo:hi, lo:hi] = 1.0 / (hi - lo)
    return jnp.asarray(m, jnp.bfloat16)


def _constants(seq):
    t = np.arange(seq)
    row = (t // GRID_W).astype(np.float32)
    col = (t % GRID_W).astype(np.float32)
    n_freq = MLA_ROPE // 4
    inv = (np.float32(ROPE_THETA) ** (-np.arange(n_freq, dtype=np.float32) / n_freq)).astype(np.float32)
    ang_r = row[:, None] * inv[None, :]
    ang_c = col[:, None] * inv[None, :]

    def tables(base):
        cos = np.zeros((seq, LANES), np.float32)
        cos[:, :base] = 1.0
        sa = np.zeros((seq, LANES), np.float32)
        sb = np.zeros((seq, LANES), np.float32)
        for k, ang in enumerate((ang_r, ang_c)):
            o = base + 2 * n_freq * k
            cos[:, o:o + n_freq] = np.cos(ang)
            cos[:, o + n_freq:o + 2 * n_freq] = np.cos(ang)
            sa[:, o:o + n_freq] = -np.sin(ang)
            sb[:, o + n_freq:o + 2 * n_freq] = np.sin(ang)
        return jnp.asarray(cos), jnp.asarray(sa), jnp.asarray(sb)

    cos_q, sin_qa, sin_qb = tables(MLA_NOPE)
    cos_k, sin_ka, sin_kb = tables(0)
    seg_q = []
    for hb in range(0, MXU_TILE, HEAD_PAD):
        seg_q += [(hb, hb + MLA_NOPE), (hb + MLA_NOPE, hb + MLA_QK)]
    shift_lanes = np.zeros((1, MLA_PAD_WIDTH), np.float32)
    shift_lanes[0, SHIFT_LANE::HEAD_PAD] = 1.0
    return dict(cos_q=cos_q, sin_qa=sin_qa, sin_qb=sin_qb, cos_k=cos_k, sin_ka=sin_ka, sin_kb=sin_kb,
                m_q=_segment_matrix(MXU_TILE, seg_q),
                m_kpe=_segment_matrix(LANES, [(0, MLA_ROPE)]),
                m_na=_segment_matrix(MXU_TILE, [(o, o + NA_DH) for o in range(0, MXU_TILE, NA_DH)]),
                shift_lanes=jnp.asarray(shift_lanes), k_one=jnp.asarray(shift_lanes[:, :HEAD_PAD]))


def _na_bias_tables(rpb, shift):
    cols = np.arange(GRID_W)
    cs = np.clip(cols - NA_KC // 2, 0, GRID_W - NA_KC)
    col_mask = (cols[None, :] >= cs[:, None]) & (cols[None, :] < cs[:, None] + NA_KC)
    dc = np.clip(cols[None, :] - cols[:, None], -(NA_KC - 1), NA_KC - 1) + NA_KC - 1
    onehot = (dc[None] == np.arange(2 * NA_KC - 1)[:, None, None]).astype(np.float32)
    rpb_c = jnp.einsum("lhdj,jqk->lhdqk", rpb, jnp.asarray(onehot), precision=lax.Precision.HIGHEST)
    rpb_c = jnp.where(jnp.asarray(col_mask), rpb_c * LOG2E - shift[:, None, None, None, None], MASK_NEG)
    return jnp.concatenate([rpb_c[:, :, :-1], rpb_c[:, :, 1:]], axis=-1)


def _softmax_shifts(p):
    amax = lambda v: jnp.max(jnp.abs(v), axis=-1)
    gq, gk = p["g_mla_q"], p["g_mla_k"]
    qn = (MLA_QK ** -0.5 * LOG2E) * jnp.sqrt(MLA_NOPE * amax(gq[:, :MLA_NOPE]) ** 2 + MLA_ROPE * amax(gq[:, MLA_NOPE:]) ** 2)
    kn = jnp.sqrt(MLA_NOPE * amax(gk[:, :MLA_NOPE]) ** 2 + MLA_ROPE * amax(gk[:, MLA_NOPE:]) ** 2)
    b_mla = BOUND_SLACK * qn * kn
    mla_fast = 2.0 * b_mla <= MAX_SHIFT_GAP
    b_na = BOUND_SLACK * (NA_DH ** -0.5 * LOG2E) * NA_DH * amax(p["g_na_q"]) * amax(p["g_na_k"])
    bias_hi = jnp.max(p["na_rpb"], axis=(1, 2, 3)) * LOG2E
    bias_lo = jnp.min(p["na_rpb"], axis=(1, 2, 3)) * LOG2E
    na_fast = 2.0 * b_na + (bias_hi - bias_lo) <= MAX_SHIFT_GAP
    return (mla_fast, jnp.where(mla_fast, b_mla, 0.0)), (na_fast, jnp.where(na_fast, b_na + bias_hi, 0.0))


def _prep_weights(p, consts):
    bf = jnp.bfloat16
    nl = p["w_in"].shape[0]
    (mla_fast, mla_shift), (na_fast, na_shift) = _softmax_shifts(p)
    w_in = p["w_in"]
    split = Q_LORA + KV_LORA + MLA_ROPE
    w_in_p = jnp.zeros((nl, D_MODEL, PROJ_COLS), bf)
    w_in_p = w_in_p.at[:, :, :split].set(w_in[:, :, :split].astype(bf)).at[:, :, C_NAQ:].set(w_in[:, :, split:].astype(bf))
    w_qb = jnp.pad(p["w_q_b"].reshape(nl, Q_LORA, MLA_HEADS, MLA_QK), ((0, 0), (0, 0), (0, 0), (0, HEAD_PAD - MLA_QK)))
    w_kv = p["w_kv_b"].reshape(nl, KV_LORA, MLA_HEADS, MLA_NOPE + MLA_V)
    w_kc = w_kv[..., :MLA_NOPE].reshape(nl, KV_LORA, MLA_HEADS * MLA_NOPE)
    w_vt = w_kv[..., MLA_NOPE:].reshape(nl, KV_LORA, MLA_WIDTH).transpose(0, 2, 1)
    g_q = jnp.tile(jnp.pad(p["g_mla_q"] * (MLA_QK ** -0.5 * LOG2E), ((0, 0), (0, HEAD_PAD - MLA_QK))), (1, MLA_HEADS))
    w_r = jnp.concatenate([p["w_router_group"], p["w_router_expert"]], axis=2)
    w_r = jnp.pad(w_r, ((0, 0), (0, 0), (0, LANES - w_r.shape[2])))
    b_r = jnp.concatenate([p["b_router_group"], p["b_router_expert"]], axis=1)
    b_r = jnp.pad(b_r, ((0, 0), (0, LANES - b_r.shape[1])))
    row = lambda v: v[:, None, :]
    return dict(
        g_mix=row(p["g_mix_norm"]), w_in=w_in_p,
        g_q_a=row(p["g_q_a"]), w_qb=w_qb.reshape(nl, Q_LORA, MLA_PAD_WIDTH).astype(bf),
        g_kv_a=row(p["g_kv_a"]), w_kc=w_kc.astype(bf), w_vt=w_vt.astype(bf),
        g_q=row(g_q), g_kc=row(jnp.tile(p["g_mla_k"][:, :MLA_NOPE], (1, MLA_HEADS))),
        g_kpe=row(jnp.pad(p["g_mla_k"][:, MLA_NOPE:], ((0, 0), (0, LANES - MLA_ROPE)))),
        g_naq=row(jnp.tile(p["g_na_q"] * (NA_DH ** -0.5 * LOG2E), (1, NA_HEADS))),
        g_nak=row(jnp.tile(p["g_na_k"], (1, NA_HEADS))),
        na_bias=_na_bias_tables(p["na_rpb"], na_shift), na_fast=na_fast,
        q_shift=-mla_shift[:, None, None] * consts["shift_lanes"][None], mla_fast=mla_fast,
        g_mla_out=row(p["g_mla_out"]), g_na_out=row(p["g_na_out"]),
        w_out=p["w_out"].astype(bf), g_ffn=row(p["g_ffn_norm"]),
        w_r=w_r.astype(bf), b_r=row(b_r),
        w_g=p["w_gate"].astype(bf), w_u=p["w_up"].astype(bf), w_d=p["w_down"].astype(bf))


def kernel(x, g_mix_norm, w_in, g_q_a, w_q_b, g_kv_a, w_kv_b, g_mla_q, g_mla_k, g_na_q, g_na_k, na_rpb, g_mla_out, g_na_out, w_out, g_ffn_norm, w_router_group, b_router_group, w_router_expert, b_router_expert, w_gate, w_up, w_down):
    batch, seq, d = x.shape
    assert d == D_MODEL and seq % TM_PROJ == 0 and seq % TQ_MLA == 0 and (batch * seq) % (TM_MOE * MOE_SUBTILES) == 0
    rows = seq // GRID_W
    assert rows % NA_ROWS_PER_STEP == 0 and rows >= NA_KR_MAX and NA_KR_MAX % 2 == 0
    p = dict(g_mix_norm=g_mix_norm, w_in=w_in, g_q_a=g_q_a, w_q_b=w_q_b, g_kv_a=g_kv_a, w_kv_b=w_kv_b,
             g_mla_q=g_mla_q, g_mla_k=g_mla_k, g_na_q=g_na_q, g_na_k=g_na_k, na_rpb=na_rpb,
             g_mla_out=g_mla_out, g_na_out=g_na_out, w_out=w_out, g_ffn_norm=g_ffn_norm,
             w_router_group=w_router_group, b_router_group=b_router_group,
             w_router_expert=w_router_expert, b_router_expert=b_router_expert,
             w_gate=w_gate, w_up=w_up, w_down=w_down)
    consts = _constants(seq)
    sw = _prep_weights(p, consts)
    xf = x.reshape(batch * seq, d)
    for l in range(w_in.shape[0]):
        q, k, vt, naq, nak, nav = _proj_call(xf, sw, l, consts, seq)
        o_a = lax.cond(sw["mla_fast"][l],
                       functools.partial(_mla_call, batch=batch, seq=seq, use_max=False),
                       functools.partial(_mla_call, batch=batch, seq=seq, use_max=True), q, k, vt)
        o_b = _na_call(sw["na_fast"][l:l + 1].astype(jnp.int32), naq, nak, nav, sw["na_bias"], l=l, batch=batch, seq=seq)
        x1, h2 = _mix_call(xf, o_a, o_b, sw, l)
        xf = _moe_call(x1, h2, sw, l)
    return xf.reshape(batch, seq, d)
```

```python
import functools

import numpy as np
import jax
import jax.numpy as jnp
from jax import lax
from jax.experimental import pallas as pl
from jax.experimental.pallas import tpu as pltpu

D_MODEL = 1024
GRID_W = 64
MLA_HEADS = 8
MLA_NOPE = 64
MLA_ROPE = 32
MLA_V = 64
MLA_QK = MLA_NOPE + MLA_ROPE
Q_LORA = 256
KV_LORA = 128
MLA_WIDTH = MLA_HEADS * MLA_V
ROPE_THETA = 10000.0
NA_HEADS = 8
NA_DH = 64
NA_WIDTH = NA_HEADS * NA_DH
NA_KR_MAX = 8
NA_KC = 16
N_GROUPS = 4
EXPERTS_PER_GROUP = 4
D_EXPERT = 256
EPS = 1e-6

LANES = 128
MXU_TILE = 256
BF16_ROWS = 16
HEAD_PAD = LANES
MLA_PAD_WIDTH = MLA_HEADS * HEAD_PAD
PROJ_COLS = 2048
LOG2E = 1.4426950408889634
MASK_NEG = -1e30
SHIFT_LANE = MLA_QK
MAX_SHIFT_GAP = 100.0
BOUND_SLACK = 1.02
VMEM_LIMIT = 56 * 1024 * 1024

C_CQ = 0
C_CKV = C_CQ + Q_LORA
C_KPE = C_CKV + KV_LORA
C_NAQ = C_KPE + LANES
C_NAK = C_NAQ + NA_WIDTH
C_NAV = C_NAK + NA_WIDTH

TM_PROJ = 1024
PROJ_SUBBLOCKS = 4
MIX_SUBBLOCKS = 2
TQ_MLA = 1024
MLA_HEADS_PER_STEP = 4
NA_ROWS_PER_STEP = 16
TM_MOE = 512
MOE_SUBTILES = 2
MOE_LOOP_CHUNK = 256
MOE_TAIL_CHUNKS = (128, 160, 192, 224, 256)
MOE_ROWS = TM_MOE + MOE_LOOP_CHUNK
RANK_BLOCK = 256
ROUTE_GID = 0
ROUTE_W0 = 1
ROUTE_LO = 8
H2_COLS = D_MODEL + LANES

assert N_GROUPS == 4
assert MOE_LOOP_CHUNK & (MOE_LOOP_CHUNK - 1) == 0 and MOE_TAIL_CHUNKS[-1] == MOE_LOOP_CHUNK
assert MLA_NOPE == NA_DH


def _cparams(sem):
    return pltpu.CompilerParams(dimension_semantics=sem, vmem_limit_bytes=VMEM_LIMIT)


def _full(shape):
    nd = len(shape)
    return pl.BlockSpec(shape, lambda *_: (0,) * nd)


def _layer(w, l, **kw):
    nd = w.ndim
    return pl.BlockSpec((None,) + w.shape[1:], lambda *_: (l,) + (0,) * (nd - 1), **kw)


def _bf16(x):
    return x.astype(jnp.bfloat16)


def _dot(a, b):
    return jnp.dot(a, b, preferred_element_type=jnp.float32)


def _dot_nt(a, b):
    return lax.dot_general(a, b, (((1,), (1,)), ((), ())), preferred_element_type=jnp.float32)


def _rms_rows(x, gain):
    ms = jnp.mean(x * x, axis=-1, keepdims=True)
    return x * lax.rsqrt(ms + EPS) * gain


def _segment_rms_scale(x, seg_mat):
    n = x.shape[-1]
    parts = []
    for j in range(0, n, seg_mat.shape[0]):
        w = min(seg_mat.shape[0], n - j)
        xs = x[:, j:j + w]
        parts.append(_dot(_bf16(xs * xs), seg_mat[:w, :w]))
    ms = parts[0] if len(parts) == 1 else jnp.concatenate(parts, axis=-1)
    return lax.rsqrt(ms + EPS)


def _rope_block(x, cos, sin_a, sin_b):
    return x * cos + pltpu.roll(x, LANES - 8, 1) * sin_a + pltpu.roll(x, 8, 1) * sin_b


def _proj_rows(rows, x_ref, gmix_ref, win_ref, gqa_ref, wqb_ref, gkva_ref, wkc_ref, wvt_ref,
               gq_ref, gkc_ref, gkpe_ref, gnaq_ref, gnak_ref,
               mq_ref, mkpe_ref, mna_ref, qshift_ref, kone_ref,
               cq_ref, sqa_ref, sqb_ref, ck_ref, ska_ref, skb_ref,
               q_out, k_out, vt_out, naq_out, nak_out, nav_out):
    h = _bf16(_rms_rows(x_ref[rows, :], gmix_ref[...]))
    proj = _dot(h, win_ref[...])

    cqn = _bf16(_rms_rows(proj[:, C_CQ:C_CKV], gqa_ref[...]))
    q = _dot(cqn, wqb_ref[...])
    q = q * _segment_rms_scale(q, mq_ref[...]) * gq_ref[...]
    cosq, sqa, sqb = cq_ref[rows, :], sqa_ref[rows, :], sqb_ref[rows, :]
    for hd in range(MLA_HEADS):
        sl = slice(hd * HEAD_PAD, (hd + 1) * HEAD_PAD)
        q_out[rows, sl] = _bf16(_rope_block(q[:, sl], cosq, sqa, sqb) + qshift_ref[:, sl])

    ckvn = _bf16(_rms_rows(proj[:, C_CKV:C_KPE], gkva_ref[...]))
    kc = _dot(ckvn, wkc_ref[...])
    kc = kc * _segment_rms_scale(kc, mna_ref[...]) * gkc_ref[...]
    vt_out[:, rows] = _bf16(_dot_nt(wvt_ref[...], ckvn))
    kp = proj[:, C_KPE:C_NAQ]
    kp = kp * lax.rsqrt(_dot(_bf16(kp * kp), mkpe_ref[...]) + EPS) * gkpe_ref[...]
    kp = _rope_block(kp, ck_ref[rows, :], ska_ref[rows, :], skb_ref[rows, :])
    tail = pltpu.roll(kp, MLA_NOPE, 1) + kone_ref[...]
    lane = lax.broadcasted_iota(jnp.int32, tail.shape, 1)
    for pr in range(MLA_HEADS // 2):
        two = kc[:, pr * LANES:(pr + 1) * LANES]
        k_out[rows, (2 * pr) * HEAD_PAD:(2 * pr + 1) * HEAD_PAD] = _bf16(jnp.where(lane < MLA_NOPE, two, tail))
        k_out[rows, (2 * pr + 1) * HEAD_PAD:(2 * pr + 2) * HEAD_PAD] = _bf16(
            jnp.where(lane < MLA_NOPE, pltpu.roll(two, MLA_NOPE, 1), tail))

    naq = proj[:, C_NAQ:C_NAK]
    naq_out[rows, :] = _bf16(naq * _segment_rms_scale(naq, mna_ref[...]) * gnaq_ref[...])
    nak = proj[:, C_NAK:C_NAV]
    nak_out[rows, :] = _bf16(nak * _segment_rms_scale(nak, mna_ref[...]) * gnak_ref[...])
    nav_out[rows, :] = _bf16(proj[:, C_NAV:PROJ_COLS])


def _proj_kernel(*refs):
    sub = TM_PROJ // PROJ_SUBBLOCKS
    for sb in range(PROJ_SUBBLOCKS):
        _proj_rows(slice(sb * sub, (sb + 1) * sub), *refs)


def _proj_call(x2d, sw, l, consts, seq):
    t = x2d.shape[0]
    tm = TM_PROJ
    n_seq_tiles = seq // tm
    row = lambda i: (i, 0)
    tab = lambda i: (i % n_seq_tiles, 0)
    layered = ("g_mix", "w_in", "g_q_a", "w_qb", "g_kv_a", "w_kc", "w_vt", "g_q", "g_kc", "g_kpe", "g_naq", "g_nak")
    weights = [sw[n] for n in layered] + [consts["m_q"], consts["m_kpe"], consts["m_na"], sw["q_shift"], consts["k_one"]]
    specs = ([_layer(sw[n], l) for n in layered] + [_full(consts[n].shape) for n in ("m_q", "m_kpe", "m_na")]
             + [_layer(sw["q_shift"], l), _full(consts["k_one"].shape)])
    tables = [consts["cos_q"], consts["sin_qa"], consts["sin_qb"], consts["cos_k"], consts["sin_ka"], consts["sin_kb"]]
    in_specs = [pl.BlockSpec((tm, D_MODEL), row)] + specs + [pl.BlockSpec((tm, LANES), tab) for _ in tables]
    out_shapes = [jax.ShapeDtypeStruct((t, MLA_PAD_WIDTH), jnp.bfloat16),
                  jax.ShapeDtypeStruct((t, MLA_PAD_WIDTH), jnp.bfloat16),
                  jax.ShapeDtypeStruct((MLA_WIDTH, t), jnp.bfloat16),
                  jax.ShapeDtypeStruct((t, NA_WIDTH), jnp.bfloat16),
                  jax.ShapeDtypeStruct((t, NA_WIDTH), jnp.bfloat16),
                  jax.ShapeDtypeStruct((t, NA_WIDTH), jnp.bfloat16)]
    out_specs = [pl.BlockSpec((MLA_WIDTH, tm), lambda i: (0, i)) if s.shape[0] == MLA_WIDTH
                 else pl.BlockSpec((tm, s.shape[1]), row) for s in out_shapes]
    return pl.pallas_call(
        _proj_kernel, grid=(t // tm,), in_specs=in_specs, out_specs=out_specs, out_shape=out_shapes,
        compiler_params=_cparams(("parallel",)), name="proj",
    )(x2d, *weights, *tables)


def _mla_kernel(q_ref, k_ref, vt_ref, o_ref, *, use_max):
    scores = []
    for hh in range(MLA_HEADS_PER_STEP):
        sl = slice(hh * HEAD_PAD, (hh + 1) * HEAD_PAD)
        scores.append(_dot_nt(k_ref[:, sl], q_ref[:, sl]))
    outs = []
    for hh, s in enumerate(scores):
        p = jnp.exp2(s - jnp.max(s, axis=0, keepdims=True)) if use_max else jnp.exp2(s)
        l = jnp.sum(p, axis=0, keepdims=True)
        pair = slice((hh // 2) * LANES, (hh // 2 + 1) * LANES)
        outs.append(_dot(vt_ref[pair, :], _bf16(p)) * (1.0 / l))
    row = lax.broadcasted_iota(jnp.int32, outs[0].shape, 0)
    for pp in range(MLA_HEADS_PER_STEP // 2):
        o_t = jnp.where(row < MLA_V, outs[2 * pp], outs[2 * pp + 1])
        o_ref[:, pp * LANES:(pp + 1) * LANES] = _bf16(o_t.T)


def _mla_call(q, k, vt, *, batch, seq, use_max):
    t = q.shape[0]
    tq = TQ_MLA
    nq = seq // tq
    hps = MLA_HEADS_PER_STEP
    return pl.pallas_call(
        functools.partial(_mla_kernel, use_max=use_max), grid=(batch, MLA_HEADS // hps, nq),
        in_specs=[pl.BlockSpec((tq, hps * HEAD_PAD), lambda b, p, i: (b * nq + i, p)),
                  pl.BlockSpec((seq, hps * HEAD_PAD), lambda b, p, i: (b, p)),
                  pl.BlockSpec((hps * MLA_V, seq), lambda b, p, i: (p, b))],
        out_specs=pl.BlockSpec((tq, hps * MLA_V), lambda b, p, i: (b * nq + i, p)),
        out_shape=jax.ShapeDtypeStruct((t, MLA_WIDTH), jnp.bfloat16),
        compiler_params=_cparams(("parallel", "parallel", "arbitrary")), name="mla_attn",
    )(q, k, vt)


def _na_body(q_ref, k_ref, v_ref, bias_ref, o_ref, *, rows, kr, use_max):
    step = pl.program_id(1)
    band = kr * GRID_W
    lane = lax.broadcasted_iota(jnp.int32, (GRID_W, LANES), 1)
    units = []
    for rr in range(NA_ROWS_PER_STEP):
        r = step * NA_ROWS_PER_STEP + rr
        rs = jnp.clip(r - kr // 2, 0, rows - kr)
        rel0 = rs - r + (NA_KR_MAX - 1)
        start = pl.multiple_of(rs * GRID_W, GRID_W)
        qrow = q_ref[rr * GRID_W:(rr + 1) * GRID_W, :]
        for pr in range(NA_HEADS // 2):
            sl = slice(pr * LANES, (pr + 1) * LANES)
            q2 = qrow[:, sl]
            zero = jnp.zeros_like(q2)
            qq = jnp.concatenate([jnp.where(lane < NA_DH, q2, zero), jnp.where(lane >= NA_DH, q2, zero)], axis=0)
            s = _dot_nt(qq, k_ref[pl.ds(start, band), sl])
            units.append((rr, pr, start, rel0, s))
    probs = []
    for rr, pr, start, rel0, s in units:
        bias = jnp.concatenate([jnp.concatenate([bias_ref[2 * pr + hh, rel0 + 2 * m] for m in range(kr // 2)], axis=-1)
                                for hh in range(2)], axis=0)
        s = s + bias
        p = jnp.exp2(s - jnp.max(s, axis=-1, keepdims=True)) if use_max else jnp.exp2(s)
        l = jnp.sum(p, axis=-1, keepdims=True)
        probs.append((_bf16(p), 1.0 / l))
    for (rr, pr, start, rel0, s), (p, rl) in zip(units, probs):
        sl = slice(pr * LANES, (pr + 1) * LANES)
        pv = _dot(p, v_ref[pl.ds(start, band), sl]) * rl
        o_ref[rr * GRID_W:(rr + 1) * GRID_W, sl] = _bf16(jnp.where(lane < NA_DH, pv[:GRID_W], pv[GRID_W:]))


def _na_kernel(fast_ref, *refs, rows, kr):
    @pl.when(fast_ref[0] != 0)
    def _():
        _na_body(*refs, rows=rows, kr=kr, use_max=False)

    @pl.when(fast_ref[0] == 0)
    def _():
        _na_body(*refs, rows=rows, kr=kr, use_max=True)


def _na_call(fast, q, k, v, bias, *, l, batch, seq):
    t = q.shape[0]
    rows = seq // GRID_W
    kr = min(NA_KR_MAX, rows)
    steps = rows // NA_ROWS_PER_STEP
    blk = NA_ROWS_PER_STEP * GRID_W
    grid_spec = pltpu.PrefetchScalarGridSpec(
        num_scalar_prefetch=1, grid=(batch, steps),
        in_specs=[pl.BlockSpec((blk, NA_WIDTH), lambda b, i, f: (b * steps + i, 0)),
                  pl.BlockSpec((seq, NA_WIDTH), lambda b, i, f: (b, 0)),
                  pl.BlockSpec((seq, NA_WIDTH), lambda b, i, f: (b, 0)),
                  _layer(bias, l)],
        out_specs=pl.BlockSpec((blk, NA_WIDTH), lambda b, i, f: (b * steps + i, 0)))
    return pl.pallas_call(
        functools.partial(_na_kernel, rows=rows, kr=kr), grid_spec=grid_spec,
        out_shape=jax.ShapeDtypeStruct((t, NA_WIDTH), jnp.bfloat16),
        compiler_params=_cparams(("parallel", "arbitrary")), name="na_attn",
    )(fast, q, k, v, bias)


def _mix_rows(rows, x_ref, oa_ref, ob_ref, ga_ref, gb_ref, wout_ref, gffn_ref, wr_ref, br_ref, x1_out, h2_out):
    mixed = jnp.concatenate([_rms_rows(oa_ref[rows, :].astype(jnp.float32), ga_ref[...]),
                             _rms_rows(ob_ref[rows, :].astype(jnp.float32), gb_ref[...])], axis=-1)
    x1 = x_ref[rows, :] + _dot(_bf16(mixed), wout_ref[...])
    x1_out[rows, :] = x1
    hi = _bf16(_rms_rows(x1, gffn_ref[...]))
    h2_out[rows, :D_MODEL] = hi
    logits = _dot(hi, wr_ref[...]) + br_ref[...]

    lane = lax.broadcasted_iota(jnp.int32, logits.shape, 1).astype(jnp.float32)
    neg = jnp.float32(-jnp.inf)
    big = jnp.float32(LANES)
    is_grp = lane < N_GROUPS
    lg = jnp.where(is_grp, logits, neg)
    gmax = jnp.max(lg, axis=-1, keepdims=True)
    gid = jnp.min(jnp.where(lg == gmax, lane, big), axis=-1, keepdims=True)
    pg_top = 1.0 / jnp.sum(jnp.where(is_grp, jnp.exp(logits - gmax), 0.0), axis=-1, keepdims=True)
    base = N_GROUPS + EXPERTS_PER_GROUP * gid
    in_sel = (lane >= base) & (lane < base + EXPERTS_PER_GROUP)
    le = jnp.where(in_sel, logits, neg)
    m1 = jnp.max(le, axis=-1, keepdims=True)
    i1 = jnp.min(jnp.where(le == m1, lane, big), axis=-1, keepdims=True)
    le2 = jnp.where(lane == i1, neg, le)
    m2 = jnp.max(le2, axis=-1, keepdims=True)
    i2 = jnp.min(jnp.where(le2 == m2, lane, big), axis=-1, keepdims=True)
    e2 = jnp.exp(m2 - m1)
    w1 = pg_top / (1.0 + e2)
    w2 = pg_top * e2 / (1.0 + e2)
    j1 = i1 - base + ROUTE_W0
    j2 = i2 - base + ROUTE_W0
    w1h = _bf16(w1).astype(jnp.float32)
    w2h = _bf16(w2).astype(jnp.float32)
    rec = jnp.where(lane == ROUTE_GID, gid,
                    jnp.where(lane == j1, w1h, jnp.where(lane == j2, w2h,
                    jnp.where(lane == j1 + ROUTE_LO, w1 - w1h, jnp.where(lane == j2 + ROUTE_LO, w2 - w2h, 0.0)))))
    h2_out[rows, D_MODEL:] = _bf16(rec)


def _mix_kernel(*refs):
    sub = TM_PROJ // MIX_SUBBLOCKS
    for sb in range(MIX_SUBBLOCKS):
        _mix_rows(slice(sb * sub, (sb + 1) * sub), *refs)


def _mix_call(x2d, oa, ob, sw, l):
    t = x2d.shape[0]
    tm = TM_PROJ
    row = lambda i: (i, 0)
    weights = [sw[n] for n in ("g_mla_out", "g_na_out", "w_out", "g_ffn", "w_r", "b_r")]
    return pl.pallas_call(
        _mix_kernel, grid=(t // tm,),
        in_specs=[pl.BlockSpec((tm, D_MODEL), row), pl.BlockSpec((tm, MLA_WIDTH), row),
                  pl.BlockSpec((tm, NA_WIDTH), row)] + [_layer(w, l) for w in weights],
        out_specs=[pl.BlockSpec((tm, D_MODEL), row), pl.BlockSpec((tm, H2_COLS), row)],
        out_shape=[jax.ShapeDtypeStruct((t, D_MODEL), jnp.float32),
                   jax.ShapeDtypeStruct((t, H2_COLS), jnp.bfloat16)],
        compiler_params=_cparams(("parallel",)), name="mix_router",
    )(x2d, oa, ob, *weights)


def _moe_partition(sub, h2_ref, xs_ref, ys_ref, cnt_ref):
    tm = TM_MOE
    rows = slice(sub * tm, (sub + 1) * tm)
    route = h2_ref[rows, D_MODEL:].astype(jnp.float32)
    lane = lax.broadcasted_iota(jnp.int32, route.shape, 1)
    gid = jnp.sum(jnp.where(lane == ROUTE_GID, route, 0.0), axis=-1, keepdims=True)
    onehot = jnp.where((lane.astype(jnp.float32) == gid) & (lane < N_GROUPS), 1.0, 0.0)
    r_i = lax.broadcasted_iota(jnp.int32, (RANK_BLOCK, RANK_BLOCK), 0)
    c_i = lax.broadcasted_iota(jnp.int32, (RANK_BLOCK, RANK_BLOCK), 1)
    tri = jnp.where(c_i < r_i, 1.0, 0.0).astype(jnp.bfloat16)
    ranks = []
    sizes = jnp.zeros((1, LANES), jnp.float32)
    for blk in range(tm // RANK_BLOCK):
        oh = onehot[blk * RANK_BLOCK:(blk + 1) * RANK_BLOCK]
        ranks.append(_dot(tri, _bf16(oh)) + sizes)
        sizes = sizes + jnp.sum(oh, axis=0, keepdims=True)
    rank = jnp.concatenate(ranks, axis=0)
    lane1 = lane[0:1]
    start_vec = jnp.zeros((1, LANES), jnp.float32)
    acc = jnp.int32(0)
    for g in range(N_GROUPS):
        n = jnp.sum(jnp.where(lane1 == g, sizes, 0.0)).astype(jnp.int32)
        cnt_ref[sub * N_GROUPS + g] = acc
        cnt_ref[MOE_SUBTILES * N_GROUPS + sub * N_GROUPS + g] = acc + n
        start_vec = jnp.where(lane1 == g, acc.astype(jnp.float32), start_vec)
        acc = acc + n
    dest = jnp.sum(onehot * (start_vec + rank), axis=-1, keepdims=True)
    dest_row = jnp.transpose(jnp.broadcast_to(dest, (tm, LANES)))[0:1, :]
    rowi = lax.broadcasted_iota(jnp.int32, (tm, tm), 0)
    perm = jnp.where(rowi == dest_row.astype(jnp.int32), 1.0, 0.0).astype(jnp.bfloat16)
    xs_ref[sub, :tm, :] = _bf16(_dot(perm, h2_ref[rows, :]))
    xs_ref[sub, tm:, :] = jnp.zeros((MOE_ROWS - tm, H2_COLS), jnp.bfloat16)
    ys_ref[sub] = jnp.zeros((MOE_ROWS, D_MODEL), jnp.bfloat16)
    return dest


def _moe_experts(sub, g, off, size, start_g, end_g, xs_ref, ys_ref, wg_ref, wu_ref, wd_ref):
    off = pl.multiple_of(off, BF16_ROWS)
    xc = xs_ref[sub, pl.ds(off, size), :D_MODEL]
    cw = xs_ref[sub, pl.ds(off, size), D_MODEL:].astype(jnp.float32)
    hid = []
    for j in range(EXPERTS_PER_GROUP):
        a = _dot(xc, wg_ref[g * EXPERTS_PER_GROUP + j])
        u = _dot(xc, wu_ref[g * EXPERTS_PER_GROUP + j])
        cj = cw[:, ROUTE_W0 + j:ROUTE_W0 + j + 1] + cw[:, ROUTE_W0 + ROUTE_LO + j:ROUTE_W0 + ROUTE_LO + j + 1]
        hid.append(a * (1.0 / (1.0 + jnp.exp(-a))) * u * cj)
    hid = _bf16(jnp.concatenate(hid, axis=-1))
    w_down = wd_ref[pl.ds(g * EXPERTS_PER_GROUP, EXPERTS_PER_GROUP)].reshape(EXPERTS_PER_GROUP * D_EXPERT, D_MODEL)
    rows = off + lax.broadcasted_iota(jnp.int32, (size, 1), 0)
    mine = (rows >= start_g) & (rows < end_g)
    old = ys_ref[sub, pl.ds(off, size), :].astype(jnp.float32)
    ys_ref[sub, pl.ds(off, size), :] = _bf16(jnp.where(mine, _dot(hid, w_down), old))


def _moe_kernel(x1_ref, h2_ref, wg_ref, wu_ref, wd_ref, o_ref, xs_ref, ys_ref, cnt_ref):
    tm = TM_MOE
    dests = [_moe_partition(sub, h2_ref, xs_ref, ys_ref, cnt_ref) for sub in range(MOE_SUBTILES)]

    def _run(u, carry):
        sub = lax.shift_right_logical(u, 2)
        g = lax.bitwise_and(u, N_GROUPS - 1)
        start_g = cnt_ref[u]
        end_g = cnt_ref[MOE_SUBTILES * N_GROUPS + u]
        first = lax.shift_left(lax.shift_right_logical(start_g, 4), 4)
        span = end_g - first
        args = (xs_ref, ys_ref, wg_ref, wu_ref, wd_ref)
        n_full = lax.shift_right_logical(span, MOE_LOOP_CHUNK.bit_length() - 1)

        def _chunk(c, inner):
            _moe_experts(sub, g, first + c * MOE_LOOP_CHUNK, MOE_LOOP_CHUNK, start_g, end_g, *args)
            return inner
        lax.fori_loop(0, n_full, _chunk, 0)
        tail = first + n_full * MOE_LOOP_CHUNK
        rest = end_g - tail
        lo = 0
        for size in MOE_TAIL_CHUNKS:
            @pl.when((rest > lo) & (rest <= size))
            def _(size=size):
                _moe_experts(sub, g, tail, size, start_g, end_g, *args)
            lo = size
        return carry

    lax.fori_loop(0, MOE_SUBTILES * N_GROUPS, _run, 0)

    col = lax.broadcasted_iota(jnp.int32, (tm, tm), 1)
    for sub, dest in enumerate(dests):
        rows = slice(sub * tm, (sub + 1) * tm)
        unperm = jnp.where(col == dest.astype(jnp.int32), 1.0, 0.0).astype(jnp.bfloat16)
        o_ref[rows, :] = x1_ref[rows, :] + _dot(unperm, ys_ref[sub, :tm, :])


def _moe_call(x1, h2, sw, l):
    t = x1.shape[0]
    tm = TM_MOE * MOE_SUBTILES
    row = lambda i: (i, 0)
    resident = lambda w: _layer(w, l, pipeline_mode=pl.Buffered(1))
    return pl.pallas_call(
        _moe_kernel, grid=(t // tm,),
        in_specs=[pl.BlockSpec((tm, D_MODEL), row), pl.BlockSpec((tm, H2_COLS), row),
                  resident(sw["w_g"]), resident(sw["w_u"]), resident(sw["w_d"])],
        out_specs=pl.BlockSpec((tm, D_MODEL), row),
        out_shape=jax.ShapeDtypeStruct((t, D_MODEL), jnp.float32),
        scratch_shapes=[pltpu.VMEM((MOE_SUBTILES, MOE_ROWS, H2_COLS), jnp.bfloat16),
                        pltpu.VMEM((MOE_SUBTILES, MOE_ROWS, D_MODEL), jnp.bfloat16),
                        pltpu.SMEM((2 * MOE_SUBTILES * N_GROUPS,), jnp.int32)],
        compiler_params=_cparams(("parallel",)), name="moe",
    )(x1, h2, sw["w_g"], sw["w_u"], sw["w_d"])


def _segment_matrix(width, segments):
    m = np.zeros((width, width), np.float32)
    for lo, hi in segments:
        m[lo:hi, lo:hi] = 1.0 / (hi - lo)
    return jnp.asarray(m, jnp.bfloat16)


def _constants(seq):
    t = np.arange(seq)
    row = (t // GRID_W).astype(np.float32)
    col = (t % GRID_W).astype(np.float32)
    n_freq = MLA_ROPE // 4
    inv = (np.float32(ROPE_THETA) ** (-np.arange(n_freq, dtype=np.float32) / n_freq)).astype(np.float32)
    ang_r = row[:, None] * inv[None, :]
    ang_c = col[:, None] * inv[None, :]

    def tables(base):
        cos = np.zeros((seq, LANES), np.float32)
        cos[:, :base] = 1.0
        sa = np.zeros((seq, LANES), np.float32)
        sb = np.zeros((seq, LANES), np.float32)
        for k, ang in enumerate((ang_r, ang_c)):
            o = base + 2 * n_freq * k
            cos[:, o:o + n_freq] = np.cos(ang)
            cos[:, o + n_freq:o + 2 * n_freq] = np.cos(ang)
            sa[:, o:o + n_freq] = -np.sin(ang)
            sb[:, o + n_freq:o + 2 * n_freq] = np.sin(ang)
        return jnp.asarray(cos), jnp.asarray(sa), jnp.asarray(sb)

    cos_q, sin_qa, sin_qb = tables(MLA_NOPE)
    cos_k, sin_ka, sin_kb = tables(0)
    seg_q = []
    for hb in range(0, MXU_TILE, HEAD_PAD):
        seg_q += [(hb, hb + MLA_NOPE), (hb + MLA_NOPE, hb + MLA_QK)]
    shift_lanes = np.zeros((1, MLA_PAD_WIDTH), np.float32)
    shift_lanes[0, SHIFT_LANE::HEAD_PAD] = 1.0
    return dict(cos_q=cos_q, sin_qa=sin_qa, sin_qb=sin_qb, cos_k=cos_k, sin_ka=sin_ka, sin_kb=sin_kb,
                m_q=_segment_matrix(MXU_TILE, seg_q),
                m_kpe=_segment_matrix(LANES, [(0, MLA_ROPE)]),
                m_na=_segment_matrix(MXU_TILE, [(o, o + NA_DH) for o in range(0, MXU_TILE, NA_DH)]),
                shift_lanes=jnp.asarray(shift_lanes), k_one=jnp.asarray(shift_lanes[:, :HEAD_PAD]))


def _na_bias_tables(rpb, shift):
    cols = np.arange(GRID_W)
    cs = np.clip(cols - NA_KC // 2, 0, GRID_W - NA_KC)
    col_mask = (cols[None, :] >= cs[:, None]) & (cols[None, :] < cs[:, None] + NA_KC)
    dc = np.clip(cols[None, :] - cols[:, None], -(NA_KC - 1), NA_KC - 1) + NA_KC - 1
    onehot = (dc[None] == np.arange(2 * NA_KC - 1)[:, None, None]).astype(np.float32)
    rpb_c = jnp.einsum("lhdj,jqk->lhdqk", rpb, jnp.asarray(onehot), precision=lax.Precision.HIGHEST)
    rpb_c = jnp.where(jnp.asarray(col_mask), rpb_c * LOG2E - shift[:, None, None, None, None], MASK_NEG)
    return jnp.concatenate([rpb_c[:, :, :-1], rpb_c[:, :, 1:]], axis=-1)


def _softmax_shifts(p):
    amax = lambda v: jnp.max(jnp.abs(v), axis=-1)
    gq, gk = p["g_mla_q"], p["g_mla_k"]
    qn = (MLA_QK ** -0.5 * LOG2E) * jnp.sqrt(MLA_NOPE * amax(gq[:, :MLA_NOPE]) ** 2 + MLA_ROPE * amax(gq[:, MLA_NOPE:]) ** 2)
    kn = jnp.sqrt(MLA_NOPE * amax(gk[:, :MLA_NOPE]) ** 2 + MLA_ROPE * amax(gk[:, MLA_NOPE:]) ** 2)
    b_mla = BOUND_SLACK * qn * kn
    mla_fast = 2.0 * b_mla <= MAX_SHIFT_GAP
    b_na = BOUND_SLACK * (NA_DH ** -0.5 * LOG2E) * NA_DH * amax(p["g_na_q"]) * amax(p["g_na_k"])
    bias_hi = jnp.max(p["na_rpb"], axis=(1, 2, 3)) * LOG2E
    bias_lo = jnp.min(p["na_rpb"], axis=(1, 2, 3)) * LOG2E
    na_fast = 2.0 * b_na + (bias_hi - bias_lo) <= MAX_SHIFT_GAP
    return (mla_fast, jnp.where(mla_fast, b_mla, 0.0)), (na_fast, jnp.where(na_fast, b_na + bias_hi, 0.0))


def _prep_weights(p, consts):
    bf = jnp.bfloat16
    nl = p["w_in"].shape[0]
    (mla_fast, mla_shift), (na_fast, na_shift) = _softmax_shifts(p)
    w_in = p["w_in"]
    split = Q_LORA + KV_LORA + MLA_ROPE
    w_in_p = jnp.zeros((nl, D_MODEL, PROJ_COLS), bf)
    w_in_p = w_in_p.at[:, :, :split].set(w_in[:, :, :split].astype(bf)).at[:, :, C_NAQ:].set(w_in[:, :, split:].astype(bf))
    w_qb = jnp.pad(p["w_q_b"].reshape(nl, Q_LORA, MLA_HEADS, MLA_QK), ((0, 0), (0, 0), (0, 0), (0, HEAD_PAD - MLA_QK)))
    w_kv = p["w_kv_b"].reshape(nl, KV_LORA, MLA_HEADS, MLA_NOPE + MLA_V)
    w_kc = w_kv[..., :MLA_NOPE].reshape(nl, KV_LORA, MLA_HEADS * MLA_NOPE)
    w_vt = w_kv[..., MLA_NOPE:].reshape(nl, KV_LORA, MLA_WIDTH).transpose(0, 2, 1)
    g_q = jnp.tile(jnp.pad(p["g_mla_q"] * (MLA_QK ** -0.5 * LOG2E), ((0, 0), (0, HEAD_PAD - MLA_QK))), (1, MLA_HEADS))
    w_r = jnp.concatenate([p["w_router_group"], p["w_router_expert"]], axis=2)
    w_r = jnp.pad(w_r, ((0, 0), (0, 0), (0, LANES - w_r.shape[2])))
    b_r = jnp.concatenate([p["b_router_group"], p["b_router_expert"]], axis=1)
    b_r = jnp.pad(b_r, ((0, 0), (0, LANES - b_r.shape[1])))
    row = lambda v: v[:, None, :]
    return dict(
        g_mix=row(p["g_mix_norm"]), w_in=w_in_p,
        g_q_a=row(p["g_q_a"]), w_qb=w_qb.reshape(nl, Q_LORA, MLA_PAD_WIDTH).astype(bf),
        g_kv_a=row(p["g_kv_a"]), w_kc=w_kc.astype(bf), w_vt=w_vt.astype(bf),
        g_q=row(g_q), g_kc=row(jnp.tile(p["g_mla_k"][:, :MLA_NOPE], (1, MLA_HEADS))),
        g_kpe=row(jnp.pad(p["g_mla_k"][:, MLA_NOPE:], ((0, 0), (0, LANES - MLA_ROPE)))),
        g_naq=row(jnp.tile(p["g_na_q"] * (NA_DH ** -0.5 * LOG2E), (1, NA_HEADS))),
        g_nak=row(jnp.tile(p["g_na_k"], (1, NA_HEADS))),
        na_bias=_na_bias_tables(p["na_rpb"], na_shift), na_fast=na_fast,
        q_shift=-mla_shift[:, None, None] * consts["shift_lanes"][None], mla_fast=mla_fast,
        g_mla_out=row(p["g_mla_out"]), g_na_out=row(p["g_na_out"]),
        w_out=p["w_out"].astype(bf), g_ffn=row(p["g_ffn_norm"]),
        w_r=w_r.astype(bf), b_r=row(b_r),
        w_g=p["w_gate"].astype(bf), w_u=p["w_up"].astype(bf), w_d=p["w_down"].astype(bf))


def kernel(x, g_mix_norm, w_in, g_q_a, w_q_b, g_kv_a, w_kv_b, g_mla_q, g_mla_k, g_na_q, g_na_k, na_rpb, g_mla_out, g_na_out, w_out, g_ffn_norm, w_router_group, b_router_group, w_router_expert, b_router_expert, w_gate, w_up, w_down):
    batch, seq, d = x.shape
    assert d == D_MODEL and seq % TM_PROJ == 0 and seq % TQ_MLA == 0 and (batch * seq) % (TM_MOE * MOE_SUBTILES) == 0
    rows = seq // GRID_W
    assert rows % NA_ROWS_PER_STEP == 0 and rows >= NA_KR_MAX and NA_KR_MAX % 2 == 0
    p = dict(g_mix_norm=g_mix_norm, w_in=w_in, g_q_a=g_q_a, w_q_b=w_q_b, g_kv_a=g_kv_a, w_kv_b=w_kv_b,
             g_mla_q=g_mla_q, g_mla_k=g_mla_k, g_na_q=g_na_q, g_na_k=g_na_k, na_rpb=na_rpb,
             g_mla_out=g_mla_out, g_na_out=g_na_out, w_out=w_out, g_ffn_norm=g_ffn_norm,
             w_router_group=w_router_group, b_router_group=b_router_group,
             w_router_expert=w_router_expert, b_router_expert=b_router_expert,
             w_gate=w_gate, w_up=w_up, w_down=w_down)
    consts = _constants(seq)
    sw = _prep_weights(p, consts)
    xf = x.reshape(batch * seq, d)
    for l in range(w_in.shape[0]):
        q, k, vt, naq, nak, nav = _proj_call(xf, sw, l, consts, seq)
        o_a = lax.cond(sw["mla_fast"][l],
                       functools.partial(_mla_call, batch=batch, seq=seq, use_max=False),
                       functools.partial(_mla_call, batch=batch, seq=seq, use_max=True), q, k, vt)
        o_b = _na_call(sw["na_fast"][l:l + 1].astype(jnp.int32), naq, nak, nav, sw["na_bias"], l=l, batch=batch, seq=seq)
        x1, h2 = _mix_call(xf, o_a, o_b, sw, l)
        xf = _moe_call(x1, h2, sw, l)
    return xf.reshape(batch, seq, d)
```
